```python
import jax, jax.numpy as jnp
from jax import lax
import numpy as np

D_MODEL = 1024
BATCH = 4
SEQ = 8192
DEPTH = 2

GRID_W = 64
CTX_LEN = 256
N_MIXERS = 2
N_MLA_LAYERS = (DEPTH + 1) // 2
N_CONV_LAYERS = DEPTH // 2
MLA_HEADS = 8
QK_NOPE = 128
QK_ROPE = 64
QK_DIM = QK_NOPE + QK_ROPE
V_DIM = 128
Q_LORA = 384
KV_LORA = 256
ROPE_FREQS = QK_ROPE // 4
ROPE_THETA = 10000.0
ATTN_SCALE = QK_DIM ** -0.5
Q_BLOCK = 128
CONV_WIDTH = 3
N_EXPERTS = 16
EC_CAPACITY = 2
D_EXPERT = 1024
EPS = 1e-6
ADA_INIT = 0.5

kernel_name = "hybrid_mla_shortconv_ecmoe_dit"


def rmsnorm(x, g):
    xf = x.astype(jnp.float32)
    y = xf * lax.rsqrt(jnp.mean(xf * xf, axis=-1, keepdims=True) + EPS)
    return (y * g.astype(jnp.float32)).astype(x.dtype)


def modulate(x, g, shift, scale):
    return rmsnorm(x, g) * (1 + scale) + shift


def adaln(cond, w, b):
    m = jax.nn.silu(cond) @ w + b
    return jnp.split(m, 6, axis=-1)


def axial_rope_angles(n):
    rows = n // GRID_W
    row = jnp.broadcast_to(jnp.arange(rows, dtype=jnp.float32)[:, None], (rows, GRID_W)).reshape(-1)
    col = jnp.broadcast_to(jnp.arange(GRID_W, dtype=jnp.float32)[None, :], (rows, GRID_W)).reshape(-1)
    inv = ROPE_THETA ** (-jnp.arange(ROPE_FREQS, dtype=jnp.float32) / ROPE_FREQS)
    return jnp.stack([row, col], axis=-1)[:, :, None] * inv


def apply_axial_rope(x, ang):
    cos = jnp.cos(ang)[None, :, None].astype(x.dtype)
    sin = jnp.sin(ang)[None, :, None].astype(x.dtype)
    xr = x.reshape(x.shape[:-1] + (2, 2, ROPE_FREQS))
    x1, x2 = xr[..., 0, :], xr[..., 1, :]
    out = jnp.stack([x1 * cos - x2 * sin, x2 * cos + x1 * sin], axis=-2)
    return out.reshape(x.shape)


def mla_queries(q_lat, q_norm, w_qb, q_gain, ang):
    b, n, _ = q_lat.shape
    q = (rmsnorm(q_lat, q_norm) @ w_qb).reshape(b, n, MLA_HEADS, QK_DIM)
    q = rmsnorm(q, q_gain)
    if ang is not None:
        q = jnp.concatenate([q[..., :QK_NOPE], apply_axial_rope(q[..., QK_NOPE:], ang)], axis=-1)
    return q


def mla_keys_values(kv_lat, k_rope, kv_norm, w_kvb, k_gain, ang):
    b, n, _ = kv_lat.shape
    kv = (rmsnorm(kv_lat, kv_norm) @ w_kvb).reshape(b, n, MLA_HEADS, QK_NOPE + V_DIM)
    k_nope, v = kv[..., :QK_NOPE], kv[..., QK_NOPE:]
    k_pe = jnp.broadcast_to(k_rope[:, :, None, :], (b, n, MLA_HEADS, QK_ROPE))
    k = rmsnorm(jnp.concatenate([k_nope, k_pe], axis=-1), k_gain)
    if ang is not None:
        k = jnp.concatenate([k[..., :QK_NOPE], apply_axial_rope(k[..., QK_NOPE:], ang)], axis=-1)
    return k, v


def attend(q, k, v):
    s = jnp.einsum('bqhd,bkhd->bhqk', q, k, preferred_element_type=jnp.float32) * ATTN_SCALE
    p = jax.nn.softmax(s, axis=-1).astype(v.dtype)
    return jnp.einsum('bhqk,bkhv->bqhv', p, v)


def latent_attention(q, k, v):
    b, n, h, d = q.shape
    nb = n // Q_BLOCK
    qb = jnp.moveaxis(q.reshape(b, nb, Q_BLOCK, h, d), 1, 0)
    o = lax.map(lambda qi: attend(qi, k, v), qb)
    return jnp.moveaxis(o, 0, 1).reshape(b, n, h * V_DIM)


def short_conv_mixer(h, w_in, conv_w, w_out):
    b_gate, c_gate, val = jnp.split(h @ w_in, 3, axis=-1)
    u = c_gate * val
    y = lax.conv_general_dilated(
        u, conv_w[:, None, :].astype(u.dtype), window_strides=(1,),
        padding=((CONV_WIDTH // 2, CONV_WIDTH // 2),),
        dimension_numbers=('NWC', 'WIO', 'NWC'), feature_group_count=D_MODEL)
    return (b_gate * y) @ w_out


def ec_moe(h, w_router, w_gate, w_up, w_down):
    b, n, d = h.shape
    cap = EC_CAPACITY * n // N_EXPERTS
    logits = jnp.einsum('bnd,de->ben', h, w_router, preferred_element_type=jnp.float32)
    aff = jax.nn.softmax(logits, axis=1)
    g, idx = lax.top_k(aff, cap)
    xs = jax.vmap(lambda hb, ib: hb[ib])(h, idx)
    a = jnp.einsum('becd,edf->becf', xs, w_gate)
    u = jnp.einsum('becd,edf->becf', xs, w_up)
    y = jnp.einsum('becf,efd->becd', jax.nn.silu(a) * u, w_down) * g[..., None].astype(h.dtype)
    return jax.vmap(lambda yb, ib: jnp.zeros((n, d), yb.dtype).at[ib.reshape(-1)].add(yb.reshape(-1, d)))(y, idx)


def setup_inputs(seed: int = 0) -> dict:
    key = jax.random.key(seed)
    ks = jax.random.split(key, 24)

    def nrm(k, shape, scale):
        return jax.random.normal(k, shape, jnp.float32) * scale

    return {
        "x": nrm(ks[0], (BATCH, SEQ, D_MODEL), 1.0),
        "c": nrm(ks[1], (BATCH, D_MODEL), 1.0),
        "ctx": nrm(ks[2], (BATCH, CTX_LEN, D_MODEL), 1.0),
        "c_ctx": nrm(ks[3], (D_MODEL,), 1.0),
        "norm_mix": 1.0 + nrm(ks[4], (DEPTH, D_MODEL), 0.1),
        "norm_ffn": 1.0 + nrm(ks[5], (DEPTH, D_MODEL), 0.1),
        "ada_w": nrm(ks[6], (DEPTH, D_MODEL, 6 * D_MODEL), ADA_INIT * D_MODEL ** -0.5),
        "ada_b": nrm(ks[7], (DEPTH, 6 * D_MODEL), 0.01),
        "mla_w_in": nrm(ks[8], (N_MLA_LAYERS, D_MODEL, Q_LORA + KV_LORA + QK_ROPE), D_MODEL ** -0.5),
        "mla_q_norm": 1.0 + nrm(ks[9], (N_MLA_LAYERS, Q_LORA), 0.1),
        "mla_w_qb": nrm(ks[10], (N_MLA_LAYERS, Q_LORA, MLA_HEADS * QK_DIM), Q_LORA ** -0.5),
        "mla_kv_norm": 1.0 + nrm(ks[11], (N_MLA_LAYERS, KV_LORA), 0.1),
        "mla_w_kvb": nrm(ks[12], (N_MLA_LAYERS, KV_LORA, MLA_HEADS * (QK_NOPE + V_DIM)), KV_LORA ** -0.5),
        "mla_q_gain": 1.0 + nrm(ks[13], (N_MLA_LAYERS, QK_DIM), 0.1),
        "mla_k_gain": 1.0 + nrm(ks[14], (N_MLA_LAYERS, QK_DIM), 0.1),
        "mla_w_out": nrm(ks[15], (N_MLA_LAYERS, MLA_HEADS * V_DIM, D_MODEL), (MLA_HEADS * V_DIM) ** -0.5),
        "conv_w_in": nrm(ks[16], (N_CONV_LAYERS, D_MODEL, 3 * D_MODEL), D_MODEL ** -0.5),
        "conv_w": nrm(ks[17], (N_CONV_LAYERS, CONV_WIDTH, D_MODEL), CONV_WIDTH ** -0.5),
        "conv_w_out": nrm(ks[18], (N_CONV_LAYERS, D_MODEL, D_MODEL), D_MODEL ** -0.5),
        "router_w": nrm(ks[19], (DEPTH, D_MODEL, N_EXPERTS), D_MODEL ** -0.5),
        "exp_w_gate": nrm(ks[20], (DEPTH, N_EXPERTS, D_MODEL, D_EXPERT), D_MODEL ** -0.5),
        "exp_w_up": nrm(ks[21], (DEPTH, N_EXPERTS, D_MODEL, D_EXPERT), D_MODEL ** -0.5),
        "exp_w_down": nrm(ks[22], (DEPTH, N_EXPERTS, D_EXPERT, D_MODEL), D_EXPERT ** -0.5),
    }


def reference(x, c, ctx, c_ctx, norm_mix, norm_ffn, ada_w, ada_b,
              mla_w_in, mla_q_norm, mla_w_qb, mla_kv_norm, mla_w_kvb, mla_q_gain, mla_k_gain, mla_w_out,
              conv_w_in, conv_w, conv_w_out, router_w, exp_w_gate, exp_w_up, exp_w_down):
    b, n, _ = x.shape
    ang = axial_rope_angles(n)
    for i in range(DEPTH):
        m = i // N_MIXERS
        sh_m, sc_m, g_m, sh_f, sc_f, g_f = adaln(c[:, None, :], ada_w[i], ada_b[i])
        ctx_live = any(j % N_MIXERS == 0 for j in range(i + 1, DEPTH))
        ctx_read_here = (i % N_MIXERS == 0)
        if ctx_live or ctx_read_here:
            csh_m, csc_m, cg_m, csh_f, csc_f, cg_f = adaln(c_ctx[None, None, :], ada_w[i], ada_b[i])
            hc = modulate(ctx, norm_mix[i], csh_m, csc_m)
        hx = modulate(x, norm_mix[i], sh_m, sc_m)

        if i % N_MIXERS == 0:
            q_lat, kv_lat, k_rope = jnp.split(hx @ mla_w_in[m], [Q_LORA, Q_LORA + KV_LORA], axis=-1)
            qx = mla_queries(q_lat, mla_q_norm[m], mla_w_qb[m], mla_q_gain[m], ang)
            kx, vx = mla_keys_values(kv_lat, k_rope, mla_kv_norm[m], mla_w_kvb[m], mla_k_gain[m], ang)
            kv_lat_c, k_rope_c = jnp.split(hc @ mla_w_in[m][:, Q_LORA:], [KV_LORA], axis=-1)
            kc, vc = mla_keys_values(kv_lat_c, k_rope_c, mla_kv_norm[m], mla_w_kvb[m], mla_k_gain[m], None)
            ox = latent_attention(qx, jnp.concatenate([kc, kx], axis=1), jnp.concatenate([vc, vx], axis=1))
            x = x + g_m * (ox @ mla_w_out[m])
            if ctx_live:
                qc = mla_queries(hc @ mla_w_in[m][:, :Q_LORA], mla_q_norm[m], mla_w_qb[m], mla_q_gain[m], None)
                oc = attend(qc, kc, vc).reshape(b, ctx.shape[1], MLA_HEADS * V_DIM)
                ctx = ctx + cg_m * (oc @ mla_w_out[m])
        else:
            x = x + g_m * short_conv_mixer(hx, conv_w_in[m], conv_w[m], conv_w_out[m])
            if ctx_live:
                ctx = ctx + cg_m * short_conv_mixer(hc, conv_w_in[m], conv_w[m], conv_w_out[m])

        x = x + g_f * ec_moe(modulate(x, norm_ffn[i], sh_f, sc_f), router_w[i], exp_w_gate[i], exp_w_up[i], exp_w_down[i])
        if ctx_live:
            ctx = ctx + cg_f * ec_moe(modulate(ctx, norm_ffn[i], csh_f, csc_f), router_w[i], exp_w_gate[i], exp_w_up[i], exp_w_down[i])
    return x
```

```python
import functools

import jax
import jax.numpy as jnp
from jax import lax
from jax.experimental import pallas as pl
from jax.experimental.pallas import tpu as pltpu

F32 = jnp.float32
BF16 = jnp.bfloat16
HIGHEST = lax.Precision.HIGHEST

GRID_W = 64
N_MIXERS = 2
MLA_HEADS = 8
QK_NOPE = 128
QK_ROPE = 64
QK_DIM = QK_NOPE + QK_ROPE
V_DIM = 128
Q_LORA = 384
KV_LORA = 256
ROPE_FREQS = QK_ROPE // 4
ROPE_THETA = 10000.0
ATTN_SCALE = QK_DIM ** -0.5
N_EXPERTS = 16
EC_CAPACITY = 2
EPS = 1e-6

LANES = 128
VMEM_LIMIT = 56 * 1024 * 1024


def _cparams(sem):
    return pltpu.CompilerParams(dimension_semantics=sem, vmem_limit_bytes=VMEM_LIMIT)


def _rms(x):
    return x * lax.rsqrt(jnp.mean(x * x, axis=-1, keepdims=True) + EPS)


def _modulate(x, g, shift, scale):
    return (_rms(x) * g) * (1.0 + scale) + shift


def _silu(a):
    return a * jax.nn.sigmoid(a)


def _adaln_kernel(c_ref, w_ref, b_ref, o_ref):
    s = _silu(c_ref[...])
    o_ref[0] = jnp.dot(s, w_ref[0], precision=HIGHEST, preferred_element_type=F32) + b_ref[0]


def _adaln(cond, ada_w, ada_b):
    depth, d, d6 = ada_w.shape
    tn = 1536
    return pl.pallas_call(
        _adaln_kernel,
        grid=(depth, d6 // tn),
        in_specs=[
            pl.BlockSpec((8, d), lambda l, j: (0, 0)),
            pl.BlockSpec((1, d, tn), lambda l, j: (l, 0, j)),
            pl.BlockSpec((1, 1, tn), lambda l, j: (l, 0, j)),
        ],
        out_specs=pl.BlockSpec((1, 8, tn), lambda l, j: (l, 0, j)),
        out_shape=jax.ShapeDtypeStruct((depth, 8, d6), F32),
        compiler_params=_cparams(("parallel", "parallel")),
        name="adaln",
    )(cond, ada_w, ada_b.reshape(depth, 1, d6))


def _mla_pre_kernel(x_ref, mod_ref, gmix_ref, win_ref, qnorm_ref, wqb_ref, kvnorm_ref, wkvb_ref,
                    gq_ref, gk_ref, cos_ref, sin_ref, *out_refs, with_q):
    if with_q:
        qn_ref, qr_ref, k_ref, v_ref = out_refs
    else:
        k_ref, v_ref = out_refs
    mod = mod_ref[0]
    hx = _modulate(x_ref[0], gmix_ref[...], mod[0:1], mod[1:2])
    lat = jnp.dot(hx.astype(BF16), win_ref[...], preferred_element_type=F32)
    cos = cos_ref[...]
    sin = sin_ref[...]
    lo = lax.broadcasted_iota(jnp.int32, (1, LANES), 1) < QK_ROPE

    def half_sums(v):
        v2 = v * v
        return (jnp.sum(jnp.where(lo, v2, 0.0), axis=-1, keepdims=True),
                jnp.sum(jnp.where(lo, 0.0, v2), axis=-1, keepdims=True))

    if with_q:
        gq = gq_ref[...]
        qn_in = (_rms(lat[:, :Q_LORA]) * qnorm_ref[...]).astype(BF16)
        qf = jnp.dot(qn_in, wqb_ref[...], preferred_element_type=F32)
        nope_w = MLA_HEADS * QK_NOPE
        pair_w = (MLA_HEADS // 2) * LANES
        for p in range(MLA_HEADS // 2):
            rp = qf[:, nope_w + p * LANES: nope_w + (p + 1) * LANES]
            sw = qf[:, nope_w + pair_w + p * LANES: nope_w + pair_w + (p + 1) * LANES]
            s_pair = half_sums(rp)
            r_pair = []
            for hh in range(2):
                h = 2 * p + hh
                nope = qf[:, h * QK_NOPE:(h + 1) * QK_NOPE]
                ms = (jnp.sum(nope * nope, axis=-1, keepdims=True) + s_pair[hh]) * (1.0 / QK_DIM)
                r = lax.rsqrt(ms + EPS) * ATTN_SCALE
                r_pair.append(r)
                qn_ref[0, h] = ((nope * r) * gq[0:1]).astype(BF16)
            roped = (rp * gq[1:2]) * cos + (sw * gq[2:3]) * sin
            qr_ref[0, p] = (roped * jnp.where(lo, r_pair[0], r_pair[1])).astype(BF16)

    gk = gk_ref[...]
    kv_lo = Q_LORA
    kvn_in = (_rms(lat[:, kv_lo:kv_lo + KV_LORA]) * kvnorm_ref[...]).astype(BF16)
    kv = jnp.dot(kvn_in, wkvb_ref[...], preferred_element_type=F32)
    pe_lo = Q_LORA + KV_LORA
    kr2 = lat[:, pe_lo:pe_lo + LANES]
    ks2 = lat[:, pe_lo + LANES:pe_lo + 2 * LANES]
    s_pe = half_sums(kr2)[0]
    k_roped = (kr2 * gk[1:2]) * cos + (ks2 * gk[2:3]) * sin
    for h in range(MLA_HEADS):
        nope = kv[:, h * QK_NOPE:(h + 1) * QK_NOPE]
        ms = (jnp.sum(nope * nope, axis=-1, keepdims=True) + s_pe) * (1.0 / QK_DIM)
        r = lax.rsqrt(ms + EPS)
        k_ref[0, h, :, 0:QK_NOPE] = ((nope * r) * gk[0:1]).astype(BF16)
        keep = lo if h % 2 == 0 else jnp.logical_not(lo)
        k_ref[0, h, :, QK_NOPE:QK_NOPE + LANES] = jnp.where(keep, k_roped * r, 0.0).astype(BF16)
        v_off = MLA_HEADS * QK_NOPE + h * V_DIM
        v_ref[0, h] = kv[:, v_off:v_off + V_DIM].astype(BF16)


def _mla_pre(x, mod, shared_mod, gmix, w, cos, sin, *, with_q, tm):
    b, n, d = x.shape
    tm = min(tm, n)
    const = lambda shape: pl.BlockSpec(shape, lambda bi, i: (0,) * len(shape))
    mod_map = (lambda bi, i: (0, 0, 0)) if shared_mod else (lambda bi, i: (bi, 0, 0))
    in_specs = [
        pl.BlockSpec((1, tm, d), lambda bi, i: (bi, i, 0)),
        pl.BlockSpec((1, 6, d), mod_map),
        const((1, d)),
        const(w["w_in"].shape),
        const((1, Q_LORA)),
        const(w["w_qb"].shape),
        const((1, KV_LORA)),
        const(w["w_kvb"].shape),
        const((3, LANES)),
        const((3, LANES)),
        pl.BlockSpec((tm, LANES), lambda bi, i: (i, 0)),
        pl.BlockSpec((tm, LANES), lambda bi, i: (i, 0)),
    ]
    head_spec = lambda nh, w_: pl.BlockSpec((1, nh, tm, w_), lambda bi, i: (bi, 0, i, 0))
    out_specs = [head_spec(MLA_HEADS, 2 * LANES), head_spec(MLA_HEADS, V_DIM)]
    out_shape = [jax.ShapeDtypeStruct((b, MLA_HEADS, n, 2 * LANES), BF16),
                 jax.ShapeDtypeStruct((b, MLA_HEADS, n, V_DIM), BF16)]
    if with_q:
        out_specs = [head_spec(MLA_HEADS, QK_NOPE), head_spec(MLA_HEADS // 2, LANES)] + out_specs
        out_shape = [jax.ShapeDtypeStruct((b, MLA_HEADS, n, QK_NOPE), BF16),
                     jax.ShapeDtypeStruct((b, MLA_HEADS // 2, n, LANES), BF16)] + out_shape
    return pl.pallas_call(
        functools.partial(_mla_pre_kernel, with_q=with_q),
        grid=(b, n // tm),
        in_specs=in_specs,
        out_specs=out_specs,
        out_shape=out_shape,
        compiler_params=_cparams(("parallel", "parallel")),
        name="mla_pre_q" if with_q else "mla_pre_ctx",
    )(x, mod, gmix, w["w_in"], w["q_norm"], w["w_qb"], w["kv_norm"], w["w_kvb"],
      w["gq"], w["gk"], cos, sin)


def _attn_kernel(qn_ref, qr_ref, kc_ref, vc_ref, kx_ref, vx_ref, o_ref, q_s, m_s, l_s, acc_s):
    j = pl.program_id(3)

    def step(k, v):
        s = lax.dot_general(q_s[...], k, (((1,), (1,)), ((), ())), preferred_element_type=F32)
        m_prev = m_s[...]
        m_new = jnp.maximum(m_prev, jnp.max(s, axis=-1, keepdims=True))
        alpha = jnp.exp(m_prev - m_new)
        p = jnp.exp(s - m_new)
        l_s[...] = alpha * l_s[...] + jnp.sum(p, axis=-1, keepdims=True)
        acc_s[...] = alpha * acc_s[...] + jnp.dot(p.astype(BF16), v, preferred_element_type=F32)
        m_s[...] = m_new

    @pl.when(j == 0)
    def _():
        q_s[:, 0:QK_NOPE] = qn_ref[0, 0]
        q_s[:, QK_NOPE:QK_NOPE + LANES] = qr_ref[0, 0]
        m_s[...] = jnp.full(m_s.shape, -jnp.inf, F32)
        l_s[...] = jnp.zeros(l_s.shape, F32)
        acc_s[...] = jnp.zeros(acc_s.shape, F32)
        step(kc_ref[0, 0], vc_ref[0, 0])

    @pl.when(j > 0)
    def _():
        step(kx_ref[0, 0], vx_ref[0, 0])

    @pl.when(j == pl.num_programs(3) - 1)
    def _():
        o_ref[0] = (acc_s[...] / l_s[...]).astype(o_ref.dtype)


def _attention(qn, qr, kc, vc, kx, vx, *, tq, tk):
    b, h, n, _ = qn.shape
    nc = kc.shape[2]
    tq = min(tq, n)
    tk = min(tk, n)
    kx_map = lambda bi, hi, i, j: (bi, hi, jnp.maximum(j - 1, 0), 0)
    return pl.pallas_call(
        _attn_kernel,
        grid=(b, h, n // tq, 1 + n // tk),
        in_specs=[
            pl.BlockSpec((1, 1, tq, QK_NOPE), lambda bi, hi, i, j: (bi, hi, i, 0)),
            pl.BlockSpec((1, 1, tq, LANES), lambda bi, hi, i, j: (bi, hi // 2, i, 0)),
            pl.BlockSpec((1, 1, nc, 2 * LANES), lambda bi, hi, i, j: (bi, hi, 0, 0)),
            pl.BlockSpec((1, 1, nc, V_DIM), lambda bi, hi, i, j: (bi, hi, 0, 0)),
            pl.BlockSpec((1, 1, tk, 2 * LANES), kx_map),
            pl.BlockSpec((1, 1, tk, V_DIM), kx_map),
        ],
        out_specs=pl.BlockSpec((1, tq, V_DIM), lambda bi, hi, i, j: (bi, i, hi)),
        out_shape=jax.ShapeDtypeStruct((b, n, h * V_DIM), BF16),
        scratch_shapes=[
            pltpu.VMEM((tq, 2 * LANES), BF16),
            pltpu.VMEM((tq, 1), F32),
            pltpu.VMEM((tq, 1), F32),
            pltpu.VMEM((tq, V_DIM), F32),
        ],
        compiler_params=_cparams(("parallel", "parallel", "parallel", "arbitrary")),
        name="flash_attn",
    )(qn, qr, kc, vc, kx, vx)


def _ffn_pre(x_new, mod, gffn_ref, rw_ref, h_ref, aff_ref):
    h2 = _modulate(x_new, gffn_ref[...], mod[3:4], mod[4:5])
    h_ref[0] = h2.astype(BF16)
    logits = jnp.dot(h2, rw_ref[...], precision=HIGHEST, preferred_element_type=F32)
    is_expert = lax.broadcasted_iota(jnp.int32, (1, LANES), 1) < N_EXPERTS
    logits = jnp.where(is_expert, logits, -jnp.inf)
    e = jnp.exp(logits - jnp.max(logits, axis=-1, keepdims=True))
    aff = e / jnp.sum(e, axis=-1, keepdims=True)
    aff_ref[0] = aff.T[:N_EXPERTS]


def _attn_out_kernel(o_ref, x_ref, mod_ref, wout_ref, gffn_ref, rw_ref, x1_ref, h_ref, aff_ref):
    mod = mod_ref[0]
    o = jnp.dot(o_ref[0], wout_ref[...], preferred_element_type=F32)
    x1 = x_ref[0] + mod[2:3] * o
    x1_ref[0] = x1
    _ffn_pre(x1, mod, gffn_ref, rw_ref, h_ref, aff_ref)


def _attn_out(ox, x, mod, w_out, gffn, rw, *, tm):
    b, n, d = x.shape
    tm = min(tm, n)
    const = lambda shape: pl.BlockSpec(shape, lambda bi, i: (0,) * len(shape))
    tok = lambda: pl.BlockSpec((1, tm, d), lambda bi, i: (bi, i, 0))
    return pl.pallas_call(
        _attn_out_kernel,
        grid=(b, n // tm),
        in_specs=[tok(), tok(), pl.BlockSpec((1, 6, d), lambda bi, i: (bi, 0, 0)),
                  const(w_out.shape), const((1, d)), const(rw.shape)],
        out_specs=[tok(), tok(), pl.BlockSpec((1, N_EXPERTS, tm), lambda bi, i: (bi, 0, i))],
        out_shape=[jax.ShapeDtypeStruct((b, n, d), F32),
                   jax.ShapeDtypeStruct((b, n, d), BF16),
                   jax.ShapeDtypeStruct((b, N_EXPERTS, n), F32)],
        compiler_params=_cparams(("parallel", "parallel")),
        name="attn_out",
    )(ox, x, mod, w_out, gffn, rw)


def _conv_kernel(xm_ref, xp_ref, xn_ref, mm_ref, mp_ref, mn_ref, gprev_ref, mod_ref, gmix_ref,
                 win_ref, cw_ref, wout_ref, gffn_ref, rw_ref, x3_ref, h_ref, aff_ref):
    i = pl.program_id(1)
    tm, d = xm_ref.shape[1], xm_ref.shape[2]
    halo = xp_ref.shape[1]
    gprev = gprev_ref[0]
    mod = mod_ref[0]
    xm = xm_ref[0] + gprev * mm_ref[0]
    xe = jnp.concatenate([xp_ref[0] + gprev * mp_ref[0], xm, xn_ref[0] + gprev * mn_ref[0]], axis=0)
    hx = _modulate(xe, gmix_ref[...], mod[0:1], mod[1:2]).astype(BF16)
    proj = jnp.dot(hx, win_ref[...], preferred_element_type=F32)
    u = proj[:, d:2 * d] * proj[:, 2 * d:3 * d]
    row = lax.broadcasted_iota(jnp.int32, (tm + 2 * halo, 1), 0)
    outside = jnp.logical_or(jnp.logical_and(i == 0, row < halo),
                             jnp.logical_and(i == pl.num_programs(1) - 1, row >= tm + halo))
    u = jnp.where(outside, 0.0, u)
    rows = tm + 2 * halo
    u_prev = pltpu.roll(u, 1, axis=0)[halo:halo + tm]
    u_next = pltpu.roll(u, rows - 1, axis=0)[halo:halo + tm]
    cw = cw_ref[...]
    y = cw[0:1] * u_prev + cw[1:2] * u[halo:halo + tm] + cw[2:3] * u_next
    z = (proj[halo:halo + tm, 0:d] * y).astype(BF16)
    x3 = xm + mod[2:3] * jnp.dot(z, wout_ref[...], preferred_element_type=F32)
    x3_ref[0] = x3
    _ffn_pre(x3, mod, gffn_ref, rw_ref, h_ref, aff_ref)


def _conv_mixer(x, moe, gprev, mod, gmix, w_in, cw, w_out, gffn, rw, *, tm):
    b, n, d = x.shape
    tm = min(tm, n)
    halo = 8
    nb = tm // halo
    last = n // halo - 1
    const = lambda shape: pl.BlockSpec(shape, lambda bi, i: (0,) * len(shape))
    tok = lambda: pl.BlockSpec((1, tm, d), lambda bi, i: (bi, i, 0))
    prev = lambda: pl.BlockSpec((1, halo, d), lambda bi, i: (bi, jnp.maximum(i * nb - 1, 0), 0))
    nxt = lambda: pl.BlockSpec((1, halo, d), lambda bi, i: (bi, jnp.minimum((i + 1) * nb, last), 0))
    return pl.pallas_call(
        _conv_kernel,
        grid=(b, n // tm),
        in_specs=[tok(), prev(), nxt(), tok(), prev(), nxt(),
                  pl.BlockSpec((1, 1, d), lambda bi, i: (bi, 0, 0)),
                  pl.BlockSpec((1, 6, d), lambda bi, i: (bi, 0, 0)),
                  const((1, d)), const(w_in.shape), const(cw.shape), const(w_out.shape),
                  const((1, d)), const(rw.shape)],
        out_specs=[tok(), tok(), pl.BlockSpec((1, N_EXPERTS, tm), lambda bi, i: (bi, 0, i))],
        out_shape=[jax.ShapeDtypeStruct((b, n, d), F32),
                   jax.ShapeDtypeStruct((b, n, d), BF16),
                   jax.ShapeDtypeStruct((b, N_EXPERTS, n), F32)],
        compiler_params=_cparams(("parallel", "parallel")),
        name="conv_mixer",
    )(x, x, x, moe, moe, moe, gprev, mod, gmix, w_in, cw, w_out, gffn, rw)


def _moe_kernel(xs_ref, g_ref, wg_ref, wu_ref, wd_ref, y_ref, *, tr, tf):
    c = xs_ref.shape[2]
    f_total = wg_ref.shape[2]

    def rows(r, carry):
        r0 = pl.multiple_of(r * tr, tr)
        xs = xs_ref[0, 0, pl.ds(r0, tr), :]
        acc = jnp.zeros((tr, wd_ref.shape[2]), F32)
        for f in range(f_total // tf):
            a = jnp.dot(xs, wg_ref[0, :, f * tf:(f + 1) * tf], preferred_element_type=F32)
            u = jnp.dot(xs, wu_ref[0, :, f * tf:(f + 1) * tf], preferred_element_type=F32)
            hm = (_silu(a) * u).astype(BF16)
            acc = acc + jnp.dot(hm, wd_ref[0, f * tf:(f + 1) * tf, :], preferred_element_type=F32)
        y_ref[0, 0, pl.ds(r0, tr), :] = acc * g_ref[0, 0, pl.ds(r0, tr), :]
        return carry

    lax.fori_loop(0, c // tr, rows, 0)


def _moe_ffn(xs, g, wg, wu, wd):
    b, e, c, d = xs.shape
    f = wg.shape[2]
    tr = min(256, c)
    tf = min(512, f)
    return pl.pallas_call(
        functools.partial(_moe_kernel, tr=tr, tf=tf),
        grid=(e, b),
        in_specs=[
            pl.BlockSpec((1, 1, c, d), lambda ei, bi: (bi, ei, 0, 0)),
            pl.BlockSpec((1, 1, c, 1), lambda ei, bi: (bi, ei, 0, 0)),
            pl.BlockSpec((1, d, f), lambda ei, bi: (ei, 0, 0)),
            pl.BlockSpec((1, d, f), lambda ei, bi: (ei, 0, 0)),
            pl.BlockSpec((1, f, d), lambda ei, bi: (ei, 0, 0)),
        ],
        out_specs=pl.BlockSpec((1, 1, c, d), lambda ei, bi: (bi, ei, 0, 0)),
        out_shape=jax.ShapeDtypeStruct((b, e, c, d), F32),
        compiler_params=_cparams(("parallel", "parallel")),
        name="moe_ffn",
    )(xs, g.reshape(b, e, c, 1), wg, wu, wd)


def _ec_moe(h, aff, wg, wu, wd):
    b, n, d = h.shape
    cap = EC_CAPACITY * n // N_EXPERTS
    g, idx = lax.top_k(aff, cap)
    xs = jax.vmap(lambda hb, ib: hb[ib])(h, idx)
    y = _moe_ffn(xs, g, wg.astype(BF16), wu.astype(BF16), wd.astype(BF16))
    return jax.vmap(lambda yb, ib: jnp.zeros((n, d), yb.dtype).at[ib.reshape(-1)].add(yb.reshape(-1, d)))(y, idx)


def _rope_tables(n):
    rows = n // GRID_W
    row = jnp.broadcast_to(jnp.arange(rows, dtype=F32)[:, None], (rows, GRID_W)).reshape(-1)
    col = jnp.broadcast_to(jnp.arange(GRID_W, dtype=F32)[None, :], (rows, GRID_W)).reshape(-1)
    inv = ROPE_THETA ** (-jnp.arange(ROPE_FREQS, dtype=F32) / ROPE_FREQS)
    ar, ac = row[:, None] * inv, col[:, None] * inv
    cos = jnp.concatenate([jnp.cos(ar), jnp.cos(ar), jnp.cos(ac), jnp.cos(ac)], axis=-1)
    sin = jnp.concatenate([-jnp.sin(ar), jnp.sin(ar), -jnp.sin(ac), jnp.sin(ac)], axis=-1)
    return jnp.tile(cos, (1, 2)), jnp.tile(sin, (1, 2))


def _swap_perm():
    f = ROPE_FREQS
    base = jnp.arange(QK_ROPE)
    return jnp.where((base // f) % 2 == 0, base + f, base - f)


def _mla_weights(w_in, q_norm, w_qb, kv_norm, w_kvb, q_gain, k_gain):
    perm = _swap_perm()
    pe = w_in[:, Q_LORA + KV_LORA:]
    pe_sw = pe[:, perm]
    w_in_x = jnp.concatenate([w_in[:, :Q_LORA + KV_LORA], pe, pe, pe_sw, pe_sw], axis=1)
    wq = w_qb.reshape(Q_LORA, MLA_HEADS, QK_DIM)
    wq_rope = wq[:, :, QK_NOPE:]
    w_qb_x = jnp.concatenate([
        wq[:, :, :QK_NOPE].reshape(Q_LORA, -1),
        wq_rope.reshape(Q_LORA, -1),
        wq_rope[:, :, perm].reshape(Q_LORA, -1)], axis=1)
    wkv = w_kvb.reshape(KV_LORA, MLA_HEADS, QK_NOPE + V_DIM)
    w_kvb_x = jnp.concatenate([wkv[:, :, :QK_NOPE].reshape(KV_LORA, -1),
                               wkv[:, :, QK_NOPE:].reshape(KV_LORA, -1)], axis=1)

    def gains(g):
        gr = g[QK_NOPE:]
        return jnp.stack([g[:QK_NOPE], jnp.tile(gr, 2), jnp.tile(gr[perm], 2)])

    return {
        "w_in": w_in_x.astype(BF16), "q_norm": q_norm[None, :], "w_qb": w_qb_x.astype(BF16),
        "kv_norm": kv_norm[None, :], "w_kvb": w_kvb_x.astype(BF16),
        "gq": gains(q_gain), "gk": gains(k_gain),
    }


def _pad_router(rw):
    return jnp.pad(rw, ((0, 0), (0, LANES - rw.shape[1])))


def kernel(x, c, ctx, c_ctx, norm_mix, norm_ffn, ada_w, ada_b, mla_w_in, mla_q_norm, mla_w_qb, mla_kv_norm, mla_w_kvb, mla_q_gain, mla_k_gain, mla_w_out, conv_w_in, conv_w, conv_w_out, router_w, exp_w_gate, exp_w_up, exp_w_down):
    b, n, d = x.shape
    nc = ctx.shape[1]
    depth = ada_w.shape[0]
    assert depth == 2 and b < 8

    cond = jnp.concatenate([c, c_ctx[None, :], jnp.zeros((8 - b - 1, d), F32)], axis=0)
    mod_all = _adaln(cond, ada_w, ada_b).reshape(depth, 8, 6, d)
    mod0, mod1 = mod_all[0, :b], mod_all[1, :b]
    mod0_ctx = mod_all[0, b:b + 1]

    w = _mla_weights(mla_w_in[0], mla_q_norm[0], mla_w_qb[0], mla_kv_norm[0], mla_w_kvb[0],
                     mla_q_gain[0], mla_k_gain[0])
    cos, sin = _rope_tables(n)
    gmix0 = norm_mix[0][None, :]
    qn, qr, kx, vx = _mla_pre(x, mod0, False, gmix0, w, cos, sin, with_q=True, tm=256)
    kc, vc = _mla_pre(ctx, mod0_ctx, True, gmix0, w, jnp.ones((nc, LANES), F32),
                      jnp.zeros((nc, LANES), F32), with_q=False, tm=256)
    ox = _attention(qn, qr, kc, vc, kx, vx, tq=256, tk=512)
    x1, h0, aff0 = _attn_out(ox, x, mod0, mla_w_out[0].astype(BF16), norm_ffn[0][None, :],
                             _pad_router(router_w[0]), tm=256)
    moe0 = _ec_moe(h0, aff0, exp_w_gate[0], exp_w_up[0], exp_w_down[0])

    x3, h1, aff1 = _conv_mixer(x1, moe0, mod0[:, 5:6], mod1, norm_mix[1][None, :],
                               conv_w_in[0].astype(BF16), conv_w[0], conv_w_out[0].astype(BF16),
                               norm_ffn[1][None, :], _pad_router(router_w[1]), tm=256)
    moe1 = _ec_moe(h1, aff1, exp_w_gate[1], exp_w_up[1], exp_w_down[1])
    return x3 + mod1[:, 5:6] * moe1
```

```python
import functools

import jax
import jax.numpy as jnp
from jax import lax
from jax.experimental import pallas as pl
from jax.experimental.pallas import tpu as pltpu

F32 = jnp.float32
BF16 = jnp.bfloat16
HIGHEST = lax.Precision.HIGHEST

GRID_W = 64
N_MIXERS = 2
MLA_HEADS = 8
QK_NOPE = 128
QK_ROPE = 64
QK_DIM = QK_NOPE + QK_ROPE
V_DIM = 128
Q_LORA = 384
KV_LORA = 256
ROPE_FREQS = QK_ROPE // 4
ROPE_THETA = 10000.0
ATTN_SCALE = QK_DIM ** -0.5
LOG2_E = 1.4426950408889634
N_EXPERTS = 16
EC_CAPACITY = 2
EPS = 1e-6

LANES = 128
VMEM_LIMIT = 56 * 1024 * 1024


def _cparams(sem):
    return pltpu.CompilerParams(dimension_semantics=sem, vmem_limit_bytes=VMEM_LIMIT)


def _rms(x):
    return x * lax.rsqrt(jnp.mean(x * x, axis=-1, keepdims=True) + EPS)


def _modulate(x, g, shift, scale):
    return (_rms(x) * g) * (1.0 + scale) + shift


def _silu(a):
    return a * jax.nn.sigmoid(a)


def _adaln_kernel(c_ref, w_ref, b_ref, o_ref):
    s = _silu(c_ref[...])
    o_ref[0] = jnp.dot(s, w_ref[0], precision=HIGHEST, preferred_element_type=F32) + b_ref[0]


def _adaln(cond, ada_w, ada_b):
    depth, d, d6 = ada_w.shape
    tn = 1536
    return pl.pallas_call(
        _adaln_kernel,
        grid=(depth, d6 // tn),
        in_specs=[
            pl.BlockSpec((8, d), lambda l, j: (0, 0)),
            pl.BlockSpec((1, d, tn), lambda l, j: (l, 0, j)),
            pl.BlockSpec((1, 1, tn), lambda l, j: (l, 0, j)),
        ],
        out_specs=pl.BlockSpec((1, 8, tn), lambda l, j: (l, 0, j)),
        out_shape=jax.ShapeDtypeStruct((depth, 8, d6), F32),
        compiler_params=_cparams(("parallel", "parallel")),
        name="adaln",
    )(cond, ada_w, ada_b.reshape(depth, 1, d6))


def _mla_pre_kernel(x_ref, mod_ref, gmix_ref, win_ref, qnorm_ref, wqb_ref, kvnorm_ref, wkvb_ref,
                    gq_ref, gk_ref, cos_ref, sin_ref, *out_refs, with_q):
    if with_q:
        qn_ref, qr_ref, k_ref, v_ref = out_refs
    else:
        k_ref, v_ref = out_refs
    mod = mod_ref[0]
    hx = _modulate(x_ref[0], gmix_ref[...], mod[0:1], mod[1:2])
    lat = jnp.dot(hx.astype(BF16), win_ref[...], preferred_element_type=F32)
    cos = cos_ref[...]
    sin = sin_ref[...]
    lo = lax.broadcasted_iota(jnp.int32, (1, LANES), 1) < QK_ROPE

    def half_sums(v):
        v2 = v * v
        return (jnp.sum(jnp.where(lo, v2, 0.0), axis=-1, keepdims=True),
                jnp.sum(jnp.where(lo, 0.0, v2), axis=-1, keepdims=True))

    if with_q:
        gq = gq_ref[...]
        qn_in = (_rms(lat[:, :Q_LORA]) * qnorm_ref[...]).astype(BF16)
        qf = jnp.dot(qn_in, wqb_ref[...], preferred_element_type=F32)
        nope_w = MLA_HEADS * QK_NOPE
        pair_w = (MLA_HEADS // 2) * LANES
        for p in range(MLA_HEADS // 2):
            rp = qf[:, nope_w + p * LANES: nope_w + (p + 1) * LANES]
            sw = qf[:, nope_w + pair_w + p * LANES: nope_w + pair_w + (p + 1) * LANES]
            s_pair = half_sums(rp)
            r_pair = []
            for hh in range(2):
                h = 2 * p + hh
                nope = qf[:, h * QK_NOPE:(h + 1) * QK_NOPE]
                ms = (jnp.sum(nope * nope, axis=-1, keepdims=True) + s_pair[hh]) * (1.0 / QK_DIM)
                r = lax.rsqrt(ms + EPS) * (ATTN_SCALE * LOG2_E)
                r_pair.append(r)
                qn_ref[0, h] = ((nope * r) * gq[0:1]).astype(BF16)
            roped = (rp * gq[1:2]) * cos + (sw * gq[2:3]) * sin
            qr_ref[0, p] = (roped * jnp.where(lo, r_pair[0], r_pair[1])).astype(BF16)

    gk = gk_ref[...]
    kv_lo = Q_LORA
    kvn_in = (_rms(lat[:, kv_lo:kv_lo + KV_LORA]) * kvnorm_ref[...]).astype(BF16)
    kv = jnp.dot(kvn_in, wkvb_ref[...], preferred_element_type=F32)
    pe_lo = Q_LORA + KV_LORA
    kr2 = lat[:, pe_lo:pe_lo + LANES]
    ks2 = lat[:, pe_lo + LANES:pe_lo + 2 * LANES]
    s_pe = half_sums(kr2)[0]
    k_roped = (kr2 * gk[1:2]) * cos + (ks2 * gk[2:3]) * sin
    for h in range(MLA_HEADS):
        nope = kv[:, h * QK_NOPE:(h + 1) * QK_NOPE]
        ms = (jnp.sum(nope * nope, axis=-1, keepdims=True) + s_pe) * (1.0 / QK_DIM)
        r = lax.rsqrt(ms + EPS)
        k_ref[0, h, :, 0:QK_NOPE] = ((nope * r) * gk[0:1]).astype(BF16)
        keep = lo if h % 2 == 0 else jnp.logical_not(lo)
        k_ref[0, h, :, QK_NOPE:QK_NOPE + LANES] = jnp.where(keep, k_roped * r, 0.0).astype(BF16)
        v_off = MLA_HEADS * QK_NOPE + h * V_DIM
        v_ref[0, h, :, 0:V_DIM] = kv[:, v_off:v_off + V_DIM].astype(BF16)
        v_ref[0, h, :, V_DIM:2 * V_DIM] = jnp.ones((kv.shape[0], V_DIM), BF16)


def _mla_pre(x, mod, shared_mod, gmix, w, cos, sin, *, with_q, tm):
    b, n, d = x.shape
    tm = min(tm, n)
    const = lambda shape: pl.BlockSpec(shape, lambda bi, i: (0,) * len(shape))
    mod_map = (lambda bi, i: (0, 0, 0)) if shared_mod else (lambda bi, i: (bi, 0, 0))
    in_specs = [
        pl.BlockSpec((1, tm, d), lambda bi, i: (bi, i, 0)),
        pl.BlockSpec((1, 6, d), mod_map),
        const((1, d)),
        const(w["w_in"].shape),
        const((1, Q_LORA)),
        const(w["w_qb"].shape),
        const((1, KV_LORA)),
        const(w["w_kvb"].shape),
        const((3, LANES)),
        const((3, LANES)),
        pl.BlockSpec((tm, LANES), lambda bi, i: (i, 0)),
        pl.BlockSpec((tm, LANES), lambda bi, i: (i, 0)),
    ]
    head_spec = lambda nh, w_: pl.BlockSpec((1, nh, tm, w_), lambda bi, i: (bi, 0, i, 0))
    out_specs = [head_spec(MLA_HEADS, 2 * LANES), head_spec(MLA_HEADS, 2 * V_DIM)]
    out_shape = [jax.ShapeDtypeStruct((b, MLA_HEADS, n, 2 * LANES), BF16),
                 jax.ShapeDtypeStruct((b, MLA_HEADS, n, 2 * V_DIM), BF16)]
    if with_q:
        out_specs = [head_spec(MLA_HEADS, QK_NOPE), head_spec(MLA_HEADS // 2, LANES)] + out_specs
        out_shape = [jax.ShapeDtypeStruct((b, MLA_HEADS, n, QK_NOPE), BF16),
                     jax.ShapeDtypeStruct((b, MLA_HEADS // 2, n, LANES), BF16)] + out_shape
    return pl.pallas_call(
        functools.partial(_mla_pre_kernel, with_q=with_q),
        grid=(b, n // tm),
        in_specs=in_specs,
        out_specs=out_specs,
        out_shape=out_shape,
        compiler_params=_cparams(("parallel", "parallel")),
        name="mla_pre_q" if with_q else "mla_pre_ctx",
    )(x, mod, gmix, w["w_in"], w["q_norm"], w["w_qb"], w["kv_norm"], w["w_kvb"],
      w["gq"], w["gk"], cos, sin)


def _attn_kernel(qn_ref, qr_ref, kc_ref, vc_ref, kx_ref, vx_ref, o_ref, m_s, acc_s, *, tk):
    q = jnp.concatenate([qn_ref[0, 0], qr_ref[0, 0]], axis=-1)
    n = kx_ref.shape[2]

    def block(k, v):
        s = lax.dot_general(q, k, (((1,), (1,)), ((), ())), preferred_element_type=F32)
        m_prev = m_s[...]
        m_new = jnp.maximum(m_prev, jnp.max(s, axis=-1, keepdims=True))
        p = jnp.exp2(s - m_new).astype(BF16)
        acc_s[...] = jnp.exp2(m_prev - m_new) * acc_s[...] + jnp.dot(p, v, preferred_element_type=F32)
        m_s[...] = m_new

    m_s[...] = jnp.full(m_s.shape, -jnp.inf, F32)
    acc_s[...] = jnp.zeros(acc_s.shape, F32)
    block(kc_ref[0, 0], vc_ref[0, 0])

    def body(j, carry):
        k0 = pl.multiple_of(j * tk, tk)
        block(kx_ref[0, 0, pl.ds(k0, tk), :], vx_ref[0, 0, pl.ds(k0, tk), :])
        return carry

    lax.fori_loop(0, n // tk, body, 0)
    acc = acc_s[...]
    o_ref[0] = (acc[:, :V_DIM] / acc[:, V_DIM:]).astype(o_ref.dtype)


def _attention(qn, qr, kc, vc, kx, vx, *, tq, tk):
    b, h, n, _ = qn.shape
    nc = kc.shape[2]
    tq = min(tq, n)
    tk = min(tk, n)
    whole = lambda rows: pl.BlockSpec((1, 1, rows, 2 * LANES), lambda bi, hi, i: (bi, hi, 0, 0))
    return pl.pallas_call(
        functools.partial(_attn_kernel, tk=tk),
        grid=(b, h, n // tq),
        in_specs=[
            pl.BlockSpec((1, 1, tq, QK_NOPE), lambda bi, hi, i: (bi, hi, i, 0)),
            pl.BlockSpec((1, 1, tq, LANES), lambda bi, hi, i: (bi, hi // 2, i, 0)),
            whole(nc), whole(nc), whole(n), whole(n),
        ],
        out_specs=pl.BlockSpec((1, tq, V_DIM), lambda bi, hi, i: (bi, i, hi)),
        out_shape=jax.ShapeDtypeStruct((b, n, h * V_DIM), BF16),
        scratch_shapes=[
            pltpu.VMEM((tq, 1), F32),
            pltpu.VMEM((tq, 2 * V_DIM), F32),
        ],
        compiler_params=_cparams(("parallel", "parallel", "arbitrary")),
        name="flash_attn",
    )(qn, qr, kc, vc, kx, vx)


def _ffn_pre(x_new, mod, gffn_ref, rw_ref, h_ref, aff_ref):
    h2 = _modulate(x_new, gffn_ref[...], mod[3:4], mod[4:5])
    h_ref[0] = h2.astype(BF16)
    logits = jnp.dot(h2, rw_ref[...], precision=HIGHEST, preferred_element_type=F32)
    is_expert = lax.broadcasted_iota(jnp.int32, (1, LANES), 1) < N_EXPERTS
    logits = jnp.where(is_expert, logits, -jnp.inf)
    e = jnp.exp(logits - jnp.max(logits, axis=-1, keepdims=True))
    aff = e / jnp.sum(e, axis=-1, keepdims=True)
    aff_ref[0] = aff.T[:N_EXPERTS]


def _attn_out_kernel(o_ref, x_ref, mod_ref, wout_ref, gffn_ref, rw_ref, x1_ref, h_ref, aff_ref):
    mod = mod_ref[0]
    o = jnp.dot(o_ref[0], wout_ref[...], preferred_element_type=F32)
    x1 = x_ref[0] + mod[2:3] * o
    x1_ref[0] = x1
    _ffn_pre(x1, mod, gffn_ref, rw_ref, h_ref, aff_ref)


def _attn_out(ox, x, mod, w_out, gffn, rw, *, tm):
    b, n, d = x.shape
    tm = min(tm, n)
    const = lambda shape: pl.BlockSpec(shape, lambda bi, i: (0,) * len(shape))
    tok = lambda: pl.BlockSpec((1, tm, d), lambda bi, i: (bi, i, 0))
    return pl.pallas_call(
        _attn_out_kernel,
        grid=(b, n // tm),
        in_specs=[tok(), tok(), pl.BlockSpec((1, 6, d), lambda bi, i: (bi, 0, 0)),
                  const(w_out.shape), const((1, d)), const(rw.shape)],
        out_specs=[tok(), tok(), pl.BlockSpec((1, N_EXPERTS, tm), lambda bi, i: (bi, 0, i))],
        out_shape=[jax.ShapeDtypeStruct((b, n, d), F32),
                   jax.ShapeDtypeStruct((b, n, d), BF16),
                   jax.ShapeDtypeStruct((b, N_EXPERTS, n), F32)],
        compiler_params=_cparams(("parallel", "parallel")),
        name="attn_out",
    )(ox, x, mod, w_out, gffn, rw)


def _conv_kernel(xm_ref, xp_ref, xn_ref, mm_ref, mp_ref, mn_ref, gprev_ref, mod_ref, gmix_ref,
                 win_ref, cw_ref, wout_ref, gffn_ref, rw_ref, x3_ref, h_ref, aff_ref):
    i = pl.program_id(1)
    tm, d = xm_ref.shape[1], xm_ref.shape[2]
    halo = xp_ref.shape[1]
    gprev = gprev_ref[0]
    mod = mod_ref[0]
    xm = xm_ref[0] + gprev * mm_ref[0]
    xe = jnp.concatenate([xp_ref[0] + gprev * mp_ref[0], xm, xn_ref[0] + gprev * mn_ref[0]], axis=0)
    hx = _modulate(xe, gmix_ref[...], mod[0:1], mod[1:2]).astype(BF16)
    proj = jnp.dot(hx, win_ref[...], preferred_element_type=F32)
    u = proj[:, d:2 * d] * proj[:, 2 * d:3 * d]
    row = lax.broadcasted_iota(jnp.int32, (tm + 2 * halo, 1), 0)
    outside = jnp.logical_or(jnp.logical_and(i == 0, row < halo),
                             jnp.logical_and(i == pl.num_programs(1) - 1, row >= tm + halo))
    u = jnp.where(outside, 0.0, u)
    rows = tm + 2 * halo
    u_prev = pltpu.roll(u, 1, axis=0)[halo:halo + tm]
    u_next = pltpu.roll(u, rows - 1, axis=0)[halo:halo + tm]
    cw = cw_ref[...]
    y = cw[0:1] * u_prev + cw[1:2] * u[halo:halo + tm] + cw[2:3] * u_next
    z = (proj[halo:halo + tm, 0:d] * y).astype(BF16)
    x3 = xm + mod[2:3] * jnp.dot(z, wout_ref[...], preferred_element_type=F32)
    x3_ref[0] = x3
    _ffn_pre(x3, mod, gffn_ref, rw_ref, h_ref, aff_ref)


def _conv_mixer(x, moe, gprev, mod, gmix, w_in, cw, w_out, gffn, rw, *, tm):
    b, n, d = x.shape
    tm = min(tm, n)
    halo = 8
    nb = tm // halo
    last = n // halo - 1
    const = lambda shape: pl.BlockSpec(shape, lambda bi, i: (0,) * len(shape))
    tok = lambda: pl.BlockSpec((1, tm, d), lambda bi, i: (bi, i, 0))
    prev = lambda: pl.BlockSpec((1, halo, d), lambda bi, i: (bi, jnp.maximum(i * nb - 1, 0), 0))
    nxt = lambda: pl.BlockSpec((1, halo, d), lambda bi, i: (bi, jnp.minimum((i + 1) * nb, last), 0))
    return pl.pallas_call(
        _conv_kernel,
        grid=(b, n // tm),
        in_specs=[tok(), prev(), nxt(), tok(), prev(), nxt(),
                  pl.BlockSpec((1, 1, d), lambda bi, i: (bi, 0, 0)),
                  pl.BlockSpec((1, 6, d), lambda bi, i: (bi, 0, 0)),
                  const((1, d)), const(w_in.shape), const(cw.shape), const(w_out.shape),
                  const((1, d)), const(rw.shape)],
        out_specs=[tok(), tok(), pl.BlockSpec((1, N_EXPERTS, tm), lambda bi, i: (bi, 0, i))],
        out_shape=[jax.ShapeDtypeStruct((b, n, d), F32),
                   jax.ShapeDtypeStruct((b, n, d), BF16),
                   jax.ShapeDtypeStruct((b, N_EXPERTS, n), F32)],
        compiler_params=_cparams(("parallel", "parallel")),
        name="conv_mixer",
    )(x, x, x, moe, moe, moe, gprev, mod, gmix, w_in, cw, w_out, gffn, rw)


def _moe_kernel(xs_ref, g_ref, wg_ref, wu_ref, wd_ref, y_ref, *, tr, tf):
    c = xs_ref.shape[2]
    f_total = wg_ref.shape[2]

    def rows(r, carry):
        r0 = pl.multiple_of(r * tr, tr)
        xs = xs_ref[0, 0, pl.ds(r0, tr), :]
        acc = jnp.zeros((tr, wd_ref.shape[2]), F32)
        for f in range(f_total // tf):
            a = jnp.dot(xs, wg_ref[0, :, f * tf:(f + 1) * tf], preferred_element_type=F32)
            u = jnp.dot(xs, wu_ref[0, :, f * tf:(f + 1) * tf], preferred_element_type=F32)
            hm = (_silu(a) * u).astype(BF16)
            acc = acc + jnp.dot(hm, wd_ref[0, f * tf:(f + 1) * tf, :], preferred_element_type=F32)
        y_ref[0, 0, pl.ds(r0, tr), :] = acc * g_ref[0, 0, pl.ds(r0, tr), :]
        return carry

    lax.fori_loop(0, c // tr, rows, 0)


def _moe_ffn(xs, g, wg, wu, wd):
    b, e, c, d = xs.shape
    f = wg.shape[2]
    tr = min(256, c)
    tf = min(512, f)
    return pl.pallas_call(
        functools.partial(_moe_kernel, tr=tr, tf=tf),
        grid=(e, b),
        in_specs=[
            pl.BlockSpec((1, 1, c, d), lambda ei, bi: (bi, ei, 0, 0)),
            pl.BlockSpec((1, 1, c, 1), lambda ei, bi: (bi, ei, 0, 0)),
            pl.BlockSpec((1, d, f), lambda ei, bi: (ei, 0, 0)),
            pl.BlockSpec((1, d, f), lambda ei, bi: (ei, 0, 0)),
            pl.BlockSpec((1, f, d), lambda ei, bi: (ei, 0, 0)),
        ],
        out_specs=pl.BlockSpec((1, 1, c, d), lambda ei, bi: (bi, ei, 0, 0)),
        out_shape=jax.ShapeDtypeStruct((b, e, c, d), F32),
        compiler_params=_cparams(("parallel", "parallel")),
        name="moe_ffn",
    )(xs, g.reshape(b, e, c, 1), wg, wu, wd)


def _ec_moe(h, aff, wg, wu, wd):
    b, n, d = h.shape
    cap = EC_CAPACITY * n // N_EXPERTS
    g, idx = lax.top_k(aff, cap)
    xs = jax.vmap(lambda hb, ib: hb[ib])(h, idx)
    y = _moe_ffn(xs, g, wg.astype(BF16), wu.astype(BF16), wd.astype(BF16))
    return jax.vmap(lambda yb, ib: jnp.zeros((n, d), yb.dtype).at[ib.reshape(-1)].add(yb.reshape(-1, d)))(y, idx)


def _rope_tables(n):
    rows = n // GRID_W
    row = jnp.broadcast_to(jnp.arange(rows, dtype=F32)[:, None], (rows, GRID_W)).reshape(-1)
    col = jnp.broadcast_to(jnp.arange(GRID_W, dtype=F32)[None, :], (rows, GRID_W)).reshape(-1)
    inv = ROPE_THETA ** (-jnp.arange(ROPE_FREQS, dtype=F32) / ROPE_FREQS)
    ar, ac = row[:, None] * inv, col[:, None] * inv
    cos = jnp.concatenate([jnp.cos(ar), jnp.cos(ar), jnp.cos(ac), jnp.cos(ac)], axis=-1)
    sin = jnp.concatenate([-jnp.sin(ar), jnp.sin(ar), -jnp.sin(ac), jnp.sin(ac)], axis=-1)
    return jnp.tile(cos, (1, 2)), jnp.tile(sin, (1, 2))


def _swap_perm():
    f = ROPE_FREQS
    base = jnp.arange(QK_ROPE)
    return jnp.where((base // f) % 2 == 0, base + f, base - f)


def _mla_weights(w_in, q_norm, w_qb, kv_norm, w_kvb, q_gain, k_gain):
    perm = _swap_perm()
    pe = w_in[:, Q_LORA + KV_LORA:]
    pe_sw = pe[:, perm]
    w_in_x = jnp.concatenate([w_in[:, :Q_LORA + KV_LORA], pe, pe, pe_sw, pe_sw], axis=1)
    wq = w_qb.reshape(Q_LORA, MLA_HEADS, QK_DIM)
    wq_rope = wq[:, :, QK_NOPE:]
    w_qb_x = jnp.concatenate([
        wq[:, :, :QK_NOPE].reshape(Q_LORA, -1),
        wq_rope.reshape(Q_LORA, -1),
        wq_rope[:, :, perm].reshape(Q_LORA, -1)], axis=1)
    wkv = w_kvb.reshape(KV_LORA, MLA_HEADS, QK_NOPE + V_DIM)
    w_kvb_x = jnp.concatenate([wkv[:, :, :QK_NOPE].reshape(KV_LORA, -1),
                               wkv[:, :, QK_NOPE:].reshape(KV_LORA, -1)], axis=1)

    def gains(g):
        gr = g[QK_NOPE:]
        return jnp.stack([g[:QK_NOPE], jnp.tile(gr, 2), jnp.tile(gr[perm], 2)])

    return {
        "w_in": w_in_x.astype(BF16), "q_norm": q_norm[None, :], "w_qb": w_qb_x.astype(BF16),
        "kv_norm": kv_norm[None, :], "w_kvb": w_kvb_x.astype(BF16),
        "gq": gains(q_gain), "gk": gains(k_gain),
    }


def _pad_router(rw):
    return jnp.pad(rw, ((0, 0), (0, LANES - rw.shape[1])))


def kernel(x, c, ctx, c_ctx, norm_mix, norm_ffn, ada_w, ada_b, mla_w_in, mla_q_norm, mla_w_qb, mla_kv_norm, mla_w_kvb, mla_q_gain, mla_k_gain, mla_w_out, conv_w_in, conv_w, conv_w_out, router_w, exp_w_gate, exp_w_up, exp_w_down):
    b, n, d = x.shape
    nc = ctx.shape[1]
    depth = ada_w.shape[0]
    assert depth == 2 and b < 8

    cond = jnp.concatenate([c, c_ctx[None, :], jnp.zeros((8 - b - 1, d), F32)], axis=0)
    mod_all = _adaln(cond, ada_w, ada_b).reshape(depth, 8, 6, d)
    mod0, mod1 = mod_all[0, :b], mod_all[1, :b]
    mod0_ctx = mod_all[0, b:b + 1]

    w = _mla_weights(mla_w_in[0], mla_q_norm[0], mla_w_qb[0], mla_kv_norm[0], mla_w_kvb[0],
                     mla_q_gain[0], mla_k_gain[0])
    cos, sin = _rope_tables(n)
    gmix0 = norm_mix[0][None, :]
    qn, qr, kx, vx = _mla_pre(x, mod0, False, gmix0, w, cos, sin, with_q=True, tm=256)
    kc, vc = _mla_pre(ctx, mod0_ctx, True, gmix0, w, jnp.ones((nc, LANES), F32),
                      jnp.zeros((nc, LANES), F32), with_q=False, tm=256)
    ox = _attention(qn, qr, kc, vc, kx, vx, tq=512, tk=1024)
    x1, h0, aff0 = _attn_out(ox, x, mod0, mla_w_out[0].astype(BF16), norm_ffn[0][None, :],
                             _pad_router(router_w[0]), tm=256)
    moe0 = _ec_moe(h0, aff0, exp_w_gate[0], exp_w_up[0], exp_w_down[0])

    x3, h1, aff1 = _conv_mixer(x1, moe0, mod0[:, 5:6], mod1, norm_mix[1][None, :],
                               conv_w_in[0].astype(BF16), conv_w[0], conv_w_out[0].astype(BF16),
                               norm_ffn[1][None, :], _pad_router(router_w[1]), tm=256)
    moe1 = _ec_moe(h1, aff1, exp_w_gate[1], exp_w_up[1], exp_w_down[1])
    return x3 + mod1[:, 5:6] * moe1
```

```python
import functools

import jax
import jax.numpy as jnp
from jax import lax
from jax.experimental import pallas as pl
from jax.experimental.pallas import tpu as pltpu

F32 = jnp.float32
BF16 = jnp.bfloat16
HIGHEST = lax.Precision.HIGHEST

GRID_W = 64
N_MIXERS = 2
MLA_HEADS = 8
QK_NOPE = 128
QK_ROPE = 64
QK_DIM = QK_NOPE + QK_ROPE
V_DIM = 128
Q_LORA = 384
KV_LORA = 256
ROPE_FREQS = QK_ROPE // 4
ROPE_THETA = 10000.0
ATTN_SCALE = QK_DIM ** -0.5
LOG2_E = 1.4426950408889634
N_EXPERTS = 16
EC_CAPACITY = 2
EPS = 1e-6

LANES = 128
VMEM_LIMIT = 56 * 1024 * 1024


def _cparams(sem):
    return pltpu.CompilerParams(dimension_semantics=sem, vmem_limit_bytes=VMEM_LIMIT)


def _rms(x):
    return x * lax.rsqrt(jnp.mean(x * x, axis=-1, keepdims=True) + EPS)


def _modulate(x, g, shift, scale):
    return (_rms(x) * g) * (1.0 + scale) + shift


def _silu(a):
    return a * jax.nn.sigmoid(a)


def _adaln_kernel(c_ref, w_ref, b_ref, o_ref):
    s = _silu(c_ref[...])
    o_ref[0] = jnp.dot(s, w_ref[0], precision=HIGHEST, preferred_element_type=F32) + b_ref[0]


def _adaln(cond, ada_w, ada_b):
    depth, d, d6 = ada_w.shape
    tn = 1536
    return pl.pallas_call(
        _adaln_kernel,
        grid=(depth, d6 // tn),
        in_specs=[
            pl.BlockSpec((8, d), lambda l, j: (0, 0)),
            pl.BlockSpec((1, d, tn), lambda l, j: (l, 0, j)),
            pl.BlockSpec((1, 1, tn), lambda l, j: (l, 0, j)),
        ],
        out_specs=pl.BlockSpec((1, 8, tn), lambda l, j: (l, 0, j)),
        out_shape=jax.ShapeDtypeStruct((depth, 8, d6), F32),
        compiler_params=_cparams(("parallel", "parallel")),
        name="adaln",
    )(cond, ada_w, ada_b.reshape(depth, 1, d6))


def _mla_pre_kernel(x_ref, mod_ref, gmix_ref, win_ref, qnorm_ref, wqb_ref, kvnorm_ref, wkvb_ref,
                    gq_ref, gk_ref, cos_ref, sin_ref, *out_refs, with_q):
    if with_q:
        qn_ref, qr_ref, k_ref, v_ref = out_refs
    else:
        k_ref, v_ref = out_refs
    mod = mod_ref[0]
    hx = _modulate(x_ref[0], gmix_ref[...], mod[0:1], mod[1:2])
    lat = jnp.dot(hx.astype(BF16), win_ref[...], preferred_element_type=F32)
    cos = cos_ref[...]
    sin = sin_ref[...]
    lo = lax.broadcasted_iota(jnp.int32, (1, LANES), 1) < QK_ROPE

    def half_sums(v):
        v2 = v * v
        return (jnp.sum(jnp.where(lo, v2, 0.0), axis=-1, keepdims=True),
                jnp.sum(jnp.where(lo, 0.0, v2), axis=-1, keepdims=True))

    if with_q:
        gq = gq_ref[...]
        qn_in = (_rms(lat[:, :Q_LORA]) * qnorm_ref[...]).astype(BF16)
        qf = jnp.dot(qn_in, wqb_ref[...], preferred_element_type=F32)
        nope_w = MLA_HEADS * QK_NOPE
        pair_w = (MLA_HEADS // 2) * LANES
        for p in range(MLA_HEADS // 2):
            rp = qf[:, nope_w + p * LANES: nope_w + (p + 1) * LANES]
            sw = qf[:, nope_w + pair_w + p * LANES: nope_w + pair_w + (p + 1) * LANES]
            s_pair = half_sums(rp)
            r_pair = []
            for hh in range(2):
                h = 2 * p + hh
                nope = qf[:, h * QK_NOPE:(h + 1) * QK_NOPE]
                ms = (jnp.sum(nope * nope, axis=-1, keepdims=True) + s_pair[hh]) * (1.0 / QK_DIM)
                r = lax.rsqrt(ms + EPS) * (ATTN_SCALE * LOG2_E)
                r_pair.append(r)
                qn_ref[0, h] = ((nope * r) * gq[0:1]).astype(BF16)
            roped = (rp * gq[1:2]) * cos + (sw * gq[2:3]) * sin
            qr_ref[0, p] = (roped * jnp.where(lo, r_pair[0], r_pair[1])).astype(BF16)

    gk = gk_ref[...]
    kv_lo = Q_LORA
    kvn_in = (_rms(lat[:, kv_lo:kv_lo + KV_LORA]) * kvnorm_ref[...]).astype(BF16)
    kv = jnp.dot(kvn_in, wkvb_ref[...], preferred_element_type=F32)
    pe_lo = Q_LORA + KV_LORA
    kr2 = lat[:, pe_lo:pe_lo + LANES]
    ks2 = lat[:, pe_lo + LANES:pe_lo + 2 * LANES]
    s_pe = half_sums(kr2)[0]
    k_roped = (kr2 * gk[1:2]) * cos + (ks2 * gk[2:3]) * sin
    for h in range(MLA_HEADS):
        nope = kv[:, h * QK_NOPE:(h + 1) * QK_NOPE]
        ms = (jnp.sum(nope * nope, axis=-1, keepdims=True) + s_pe) * (1.0 / QK_DIM)
        r = lax.rsqrt(ms + EPS)
        k_ref[0, h, :, 0:QK_NOPE] = ((nope * r) * gk[0:1]).astype(BF16)
        keep = lo if h % 2 == 0 else jnp.logical_not(lo)
        k_ref[0, h, :, QK_NOPE:QK_NOPE + LANES] = jnp.where(keep, k_roped * r, 0.0).astype(BF16)
        v_off = MLA_HEADS * QK_NOPE + h * V_DIM
        v_ref[0, h, :, 0:V_DIM] = kv[:, v_off:v_off + V_DIM].astype(BF16)
        v_ref[0, h, :, V_DIM:2 * V_DIM] = jnp.ones((kv.shape[0], V_DIM), BF16)


def _mla_pre(x, mod, shared_mod, gmix, w, cos, sin, *, with_q, tm):
    b, n, d = x.shape
    tm = min(tm, n)
    const = lambda shape: pl.BlockSpec(shape, lambda bi, i: (0,) * len(shape))
    mod_map = (lambda bi, i: (0, 0, 0)) if shared_mod else (lambda bi, i: (bi, 0, 0))
    in_specs = [
        pl.BlockSpec((1, tm, d), lambda bi, i: (bi, i, 0)),
        pl.BlockSpec((1, 6, d), mod_map),
        const((1, d)),
        const(w["w_in"].shape),
        const((1, Q_LORA)),
        const(w["w_qb"].shape),
        const((1, KV_LORA)),
        const(w["w_kvb"].shape),
        const((3, LANES)),
        const((3, LANES)),
        pl.BlockSpec((tm, LANES), lambda bi, i: (i, 0)),
        pl.BlockSpec((tm, LANES), lambda bi, i: (i, 0)),
    ]
    head_spec = lambda nh, w_: pl.BlockSpec((1, nh, tm, w_), lambda bi, i: (bi, 0, i, 0))
    out_specs = [head_spec(MLA_HEADS, 2 * LANES), head_spec(MLA_HEADS, 2 * V_DIM)]
    out_shape = [jax.ShapeDtypeStruct((b, MLA_HEADS, n, 2 * LANES), BF16),
                 jax.ShapeDtypeStruct((b, MLA_HEADS, n, 2 * V_DIM), BF16)]
    if with_q:
        out_specs = [head_spec(MLA_HEADS, QK_NOPE), head_spec(MLA_HEADS // 2, LANES)] + out_specs
        out_shape = [jax.ShapeDtypeStruct((b, MLA_HEADS, n, QK_NOPE), BF16),
                     jax.ShapeDtypeStruct((b, MLA_HEADS // 2, n, LANES), BF16)] + out_shape
    return pl.pallas_call(
        functools.partial(_mla_pre_kernel, with_q=with_q),
        grid=(b, n // tm),
        in_specs=in_specs,
        out_specs=out_specs,
        out_shape=out_shape,
        compiler_params=_cparams(("parallel", "parallel")),
        name="mla_pre_q" if with_q else "mla_pre_ctx",
    )(x, mod, gmix, w["w_in"], w["q_norm"], w["w_qb"], w["kv_norm"], w["w_kvb"],
      w["gq"], w["gk"], cos, sin)


def _attn_kernel(qn_ref, qr_ref, kc_ref, vc_ref, kx_ref, vx_ref, o_ref, m_s, acc_s, sc_s, sa_s, sb_s, *, tk):
    q = jnp.concatenate([qn_ref[0, 0], qr_ref[0, 0]], axis=-1)
    nblk = kx_ref.shape[2] // tk

    def scores(k):
        return lax.dot_general(q, k, (((1,), (1,)), ((), ())), preferred_element_type=F32)

    def kblk(j):
        return kx_ref[0, 0, pl.ds(pl.multiple_of(j * tk, tk), tk), :]

    def vblk(j):
        return vx_ref[0, 0, pl.ds(pl.multiple_of(j * tk, tk), tk), :]

    def absorb(s_ref, v):
        s = s_ref[...]
        m_prev = m_s[...]
        m_new = jnp.maximum(m_prev, jnp.max(s, axis=-1, keepdims=True))
        p = jnp.exp2(s - m_new).astype(BF16)
        acc_s[...] = jnp.exp2(m_prev - m_new) * acc_s[...] + jnp.dot(p, v, preferred_element_type=F32)
        m_s[...] = m_new

    m_s[...] = jnp.full(m_s.shape, -jnp.inf, F32)
    acc_s[...] = jnp.zeros(acc_s.shape, F32)
    sc_s[...] = scores(kc_ref[0, 0])
    sa_s[...] = scores(kblk(0))
    absorb(sc_s, vc_ref[0, 0])

    if nblk % 2 == 0:
        def body(i, carry):
            j = 2 * i
            sb_s[...] = scores(kblk(j + 1))
            absorb(sa_s, vblk(j))
            sa_s[...] = scores(kblk(jnp.minimum(j + 2, nblk - 1)))
            absorb(sb_s, vblk(j + 1))
            return carry

        lax.fori_loop(0, nblk // 2, body, 0)
    else:
        absorb(sa_s, vblk(0))

        def body(j, carry):
            sa_s[...] = scores(kblk(j))
            absorb(sa_s, vblk(j))
            return carry

        lax.fori_loop(1, nblk, body, 0)
    acc = acc_s[...]
    o_ref[0] = (acc[:, :V_DIM] / acc[:, V_DIM:]).astype(o_ref.dtype)


def _attention(qn, qr, kc, vc, kx, vx, *, tq, tk):
    b, h, n, _ = qn.shape
    nc = kc.shape[2]
    tq = min(tq, n)
    tk = min(tk, n)
    whole = lambda rows: pl.BlockSpec((1, 1, rows, 2 * LANES), lambda bi, hi, i: (bi, hi, 0, 0))
    return pl.pallas_call(
        functools.partial(_attn_kernel, tk=tk),
        grid=(b, h, n // tq),
        in_specs=[
            pl.BlockSpec((1, 1, tq, QK_NOPE), lambda bi, hi, i: (bi, hi, i, 0)),
            pl.BlockSpec((1, 1, tq, LANES), lambda bi, hi, i: (bi, hi // 2, i, 0)),
            whole(nc), whole(nc), whole(n), whole(n),
        ],
        out_specs=pl.BlockSpec((1, tq, V_DIM), lambda bi, hi, i: (bi, i, hi)),
        out_shape=jax.ShapeDtypeStruct((b, n, h * V_DIM), BF16),
        scratch_shapes=[
            pltpu.VMEM((tq, 1), F32),
            pltpu.VMEM((tq, 2 * V_DIM), F32),
            pltpu.VMEM((tq, nc), F32),
            pltpu.VMEM((tq, tk), F32),
            pltpu.VMEM((tq, tk), F32),
        ],
        compiler_params=_cparams(("parallel", "parallel", "arbitrary")),
        name="flash_attn",
    )(qn, qr, kc, vc, kx, vx)


def _ffn_pre(x_new, mod, gffn_ref, rw_ref, h_ref, aff_ref):
    h2 = _modulate(x_new, gffn_ref[...], mod[3:4], mod[4:5])
    hi = h2.astype(BF16)
    lo = (h2 - hi.astype(F32)).astype(BF16)
    h_ref[0] = hi
    rw = rw_ref[...]
    t = jnp.dot(hi, rw, preferred_element_type=F32) + jnp.dot(lo, rw, preferred_element_type=F32)
    logits = t[:, :LANES] + t[:, LANES:]
    is_expert = lax.broadcasted_iota(jnp.int32, (1, LANES), 1) < N_EXPERTS
    logits = jnp.where(is_expert, logits, -jnp.inf)
    e = jnp.exp(logits - jnp.max(logits, axis=-1, keepdims=True))
    aff = e / jnp.sum(e, axis=-1, keepdims=True)
    aff_ref[0] = aff.T[:N_EXPERTS]


def _attn_out_kernel(o_ref, x_ref, mod_ref, wout_ref, gffn_ref, rw_ref, x1_ref, h_ref, aff_ref):
    mod = mod_ref[0]
    o = jnp.dot(o_ref[0], wout_ref[...], preferred_element_type=F32)
    x1 = x_ref[0] + mod[2:3] * o
    x1_ref[0] = x1
    _ffn_pre(x1, mod, gffn_ref, rw_ref, h_ref, aff_ref)


def _attn_out(ox, x, mod, w_out, gffn, rw, *, tm):
    b, n, d = x.shape
    tm = min(tm, n)
    const = lambda shape: pl.BlockSpec(shape, lambda bi, i: (0,) * len(shape))
    tok = lambda: pl.BlockSpec((1, tm, d), lambda bi, i: (bi, i, 0))
    return pl.pallas_call(
        _attn_out_kernel,
        grid=(b, n // tm),
        in_specs=[tok(), tok(), pl.BlockSpec((1, 6, d), lambda bi, i: (bi, 0, 0)),
                  const(w_out.shape), const((1, d)), const(rw.shape)],
        out_specs=[tok(), tok(), pl.BlockSpec((1, N_EXPERTS, tm), lambda bi, i: (bi, 0, i))],
        out_shape=[jax.ShapeDtypeStruct((b, n, d), F32),
                   jax.ShapeDtypeStruct((b, n, d), BF16),
                   jax.ShapeDtypeStruct((b, N_EXPERTS, n), F32)],
        compiler_params=_cparams(("parallel", "parallel")),
        name="attn_out",
    )(ox, x, mod, w_out, gffn, rw)


def _conv_kernel(xm_ref, xp_ref, xn_ref, mm_ref, mp_ref, mn_ref, gprev_ref, mod_ref, gmix_ref,
                 win_ref, cw_ref, wout_ref, gffn_ref, rw_ref, x3_ref, h_ref, aff_ref):
    i = pl.program_id(1)
    tm, d = xm_ref.shape[1], xm_ref.shape[2]
    halo = xp_ref.shape[1]
    gprev = gprev_ref[0]
    mod = mod_ref[0]
    xm = xm_ref[0] + gprev * mm_ref[0]
    xe = jnp.concatenate([xp_ref[0] + gprev * mp_ref[0], xm, xn_ref[0] + gprev * mn_ref[0]], axis=0)
    hx = _modulate(xe, gmix_ref[...], mod[0:1], mod[1:2]).astype(BF16)
    proj = jnp.dot(hx, win_ref[...], preferred_element_type=F32)
    u = proj[:, d:2 * d] * proj[:, 2 * d:3 * d]
    row = lax.broadcasted_iota(jnp.int32, (tm + 2 * halo, 1), 0)
    outside = jnp.logical_or(jnp.logical_and(i == 0, row < halo),
                             jnp.logical_and(i == pl.num_programs(1) - 1, row >= tm + halo))
    u = jnp.where(outside, 0.0, u)
    rows = tm + 2 * halo
    u_prev = pltpu.roll(u, 1, axis=0)[halo:halo + tm]
    u_next = pltpu.roll(u, rows - 1, axis=0)[halo:halo + tm]
    cw = cw_ref[...]
    y = cw[0:1] * u_prev + cw[1:2] * u[halo:halo + tm] + cw[2:3] * u_next
    z = (proj[halo:halo + tm, 0:d] * y).astype(BF16)
    x3 = xm + mod[2:3] * jnp.dot(z, wout_ref[...], preferred_element_type=F32)
    x3_ref[0] = x3
    _ffn_pre(x3, mod, gffn_ref, rw_ref, h_ref, aff_ref)


def _conv_mixer(x, moe, gprev, mod, gmix, w_in, cw, w_out, gffn, rw, *, tm):
    b, n, d = x.shape
    tm = min(tm, n)
    halo = 8
    nb = tm // halo
    last = n // halo - 1
    const = lambda shape: pl.BlockSpec(shape, lambda bi, i: (0,) * len(shape))
    tok = lambda: pl.BlockSpec((1, tm, d), lambda bi, i: (bi, i, 0))
    prev = lambda: pl.BlockSpec((1, halo, d), lambda bi, i: (bi, jnp.maximum(i * nb - 1, 0), 0))
    nxt = lambda: pl.BlockSpec((1, halo, d), lambda bi, i: (bi, jnp.minimum((i + 1) * nb, last), 0))
    return pl.pallas_call(
        _conv_kernel,
        grid=(b, n // tm),
        in_specs=[tok(), prev(), nxt(), tok(), prev(), nxt(),
                  pl.BlockSpec((1, 1, d), lambda bi, i: (bi, 0, 0)),
                  pl.BlockSpec((1, 6, d), lambda bi, i: (bi, 0, 0)),
                  const((1, d)), const(w_in.shape), const(cw.shape), const(w_out.shape),
                  const((1, d)), const(rw.shape)],
        out_specs=[tok(), tok(), pl.BlockSpec((1, N_EXPERTS, tm), lambda bi, i: (bi, 0, i))],
        out_shape=[jax.ShapeDtypeStruct((b, n, d), F32),
                   jax.ShapeDtypeStruct((b, n, d), BF16),
                   jax.ShapeDtypeStruct((b, N_EXPERTS, n), F32)],
        compiler_params=_cparams(("parallel", "parallel")),
        name="conv_mixer",
    )(x, x, x, moe, moe, moe, gprev, mod, gmix, w_in, cw, w_out, gffn, rw)


def _moe_kernel(xs_ref, g_ref, wg_ref, wu_ref, wd_ref, y_ref, wg_s, wu_s, wd_s, *, tr, tf):
    c = xs_ref.shape[2]
    f_total = wg_s.shape[1]

    @pl.when(pl.program_id(1) == 0)
    def _():
        for src, dst in ((wg_ref, wg_s), (wu_ref, wu_s), (wd_ref, wd_s)):
            for r in range(0, dst.shape[0], tr):
                dst[r:r + tr, :] = src[0, 0, r:r + tr, :].astype(BF16)

    def rows(r, carry):
        r0 = pl.multiple_of(r * tr, tr)
        xs = xs_ref[0, 0, pl.ds(r0, tr), :]
        acc = jnp.zeros((tr, wd_s.shape[1]), F32)
        for f in range(f_total // tf):
            a = jnp.dot(xs, wg_s[:, f * tf:(f + 1) * tf], preferred_element_type=F32)
            u = jnp.dot(xs, wu_s[:, f * tf:(f + 1) * tf], preferred_element_type=F32)
            hm = (_silu(a) * u).astype(BF16)
            acc = acc + jnp.dot(hm, wd_s[f * tf:(f + 1) * tf, :], preferred_element_type=F32)
        y_ref[0, 0, pl.ds(r0, tr), :] = acc * g_ref[0, 0, pl.ds(r0, tr), :]
        return carry

    lax.fori_loop(0, c // tr, rows, 0)


def _moe_ffn(xs, g, wg, wu, wd, layer):
    b, e, c, d = xs.shape
    f = wg.shape[3]
    tr = min(256, c)
    tf = min(512, f)
    w_spec = lambda rows_, cols: pl.BlockSpec((1, 1, rows_, cols), lambda ei, bi: (layer, ei, 0, 0))
    return pl.pallas_call(
        functools.partial(_moe_kernel, tr=tr, tf=tf),
        grid=(e, b),
        in_specs=[
            pl.BlockSpec((1, 1, c, d), lambda ei, bi: (bi, ei, 0, 0)),
            pl.BlockSpec((1, 1, c, 1), lambda ei, bi: (bi, ei, 0, 0)),
            w_spec(d, f), w_spec(d, f), w_spec(f, d),
        ],
        out_specs=pl.BlockSpec((1, 1, c, d), lambda ei, bi: (bi, ei, 0, 0)),
        out_shape=jax.ShapeDtypeStruct((b, e, c, d), F32),
        scratch_shapes=[pltpu.VMEM((d, f), BF16), pltpu.VMEM((d, f), BF16), pltpu.VMEM((f, d), BF16)],
        compiler_params=_cparams(("arbitrary", "arbitrary")),
        name="moe_ffn",
    )(xs, g.reshape(b, e, c, 1), wg, wu, wd)


def _ec_moe(h, aff, wg, wu, wd, layer):
    b, n, d = h.shape
    cap = EC_CAPACITY * n // N_EXPERTS
    g, idx = lax.top_k(aff, cap)
    xs = jax.vmap(lambda hb, ib: hb[ib])(h, idx)
    y = _moe_ffn(xs, g, wg, wu, wd, layer)
    return jax.vmap(lambda yb, ib: jnp.zeros((n, d), yb.dtype).at[ib.reshape(-1)].add(yb.reshape(-1, d)))(y, idx)


def _rope_tables(n):
    rows = n // GRID_W
    row = jnp.broadcast_to(jnp.arange(rows, dtype=F32)[:, None], (rows, GRID_W)).reshape(-1)
    col = jnp.broadcast_to(jnp.arange(GRID_W, dtype=F32)[None, :], (rows, GRID_W)).reshape(-1)
    inv = ROPE_THETA ** (-jnp.arange(ROPE_FREQS, dtype=F32) / ROPE_FREQS)
    ar, ac = row[:, None] * inv, col[:, None] * inv
    cos = jnp.concatenate([jnp.cos(ar), jnp.cos(ar), jnp.cos(ac), jnp.cos(ac)], axis=-1)
    sin = jnp.concatenate([-jnp.sin(ar), jnp.sin(ar), -jnp.sin(ac), jnp.sin(ac)], axis=-1)
    return jnp.tile(cos, (1, 2)), jnp.tile(sin, (1, 2))


def _swap_perm():
    f = ROPE_FREQS
    base = jnp.arange(QK_ROPE)
    return jnp.where((base // f) % 2 == 0, base + f, base - f)


def _mla_weights(w_in, q_norm, w_qb, kv_norm, w_kvb, q_gain, k_gain):
    perm = _swap_perm()
    pe = w_in[:, Q_LORA + KV_LORA:]
    pe_sw = pe[:, perm]
    w_in_x = jnp.concatenate([w_in[:, :Q_LORA + KV_LORA], pe, pe, pe_sw, pe_sw], axis=1)
    wq = w_qb.reshape(Q_LORA, MLA_HEADS, QK_DIM)
    wq_rope = wq[:, :, QK_NOPE:]
    w_qb_x = jnp.concatenate([
        wq[:, :, :QK_NOPE].reshape(Q_LORA, -1),
        wq_rope.reshape(Q_LORA, -1),
        wq_rope[:, :, perm].reshape(Q_LORA, -1)], axis=1)
    wkv = w_kvb.reshape(KV_LORA, MLA_HEADS, QK_NOPE + V_DIM)
    w_kvb_x = jnp.concatenate([wkv[:, :, :QK_NOPE].reshape(KV_LORA, -1),
                               wkv[:, :, QK_NOPE:].reshape(KV_LORA, -1)], axis=1)

    def gains(g):
        gr = g[QK_NOPE:]
        return jnp.stack([g[:QK_NOPE], jnp.tile(gr, 2), jnp.tile(gr[perm], 2)])

    return {
        "w_in": w_in_x.astype(BF16), "q_norm": q_norm[None, :], "w_qb": w_qb_x.astype(BF16),
        "kv_norm": kv_norm[None, :], "w_kvb": w_kvb_x.astype(BF16),
        "gq": gains(q_gain), "gk": gains(k_gain),
    }


def _pad_router(rw):
    hi = rw.astype(BF16)
    lo = (rw - hi.astype(F32)).astype(BF16)
    pad = ((0, 0), (0, LANES - rw.shape[1]))
    return jnp.concatenate([jnp.pad(hi, pad), jnp.pad(lo, pad)], axis=1)


def kernel(x, c, ctx, c_ctx, norm_mix, norm_ffn, ada_w, ada_b, mla_w_in, mla_q_norm, mla_w_qb, mla_kv_norm, mla_w_kvb, mla_q_gain, mla_k_gain, mla_w_out, conv_w_in, conv_w, conv_w_out, router_w, exp_w_gate, exp_w_up, exp_w_down):
    b, n, d = x.shape
    nc = ctx.shape[1]
    depth = ada_w.shape[0]
    assert depth == 2 and b < 8

    cond = jnp.concatenate([c, c_ctx[None, :], jnp.zeros((8 - b - 1, d), F32)], axis=0)
    mod_all = _adaln(cond, ada_w, ada_b).reshape(depth, 8, 6, d)
    mod0, mod1 = mod_all[0, :b], mod_all[1, :b]
    mod0_ctx = mod_all[0, b:b + 1]

    w = _mla_weights(mla_w_in[0], mla_q_norm[0], mla_w_qb[0], mla_kv_norm[0], mla_w_kvb[0],
                     mla_q_gain[0], mla_k_gain[0])
    cos, sin = _rope_tables(n)
    gmix0 = norm_mix[0][None, :]
    qn, qr, kx, vx = _mla_pre(x, mod0, False, gmix0, w, cos, sin, with_q=True, tm=256)
    kc, vc = _mla_pre(ctx, mod0_ctx, True, gmix0, w, jnp.ones((nc, LANES), F32),
                      jnp.zeros((nc, LANES), F32), with_q=False, tm=256)
    ox = _attention(qn, qr, kc, vc, kx, vx, tq=512, tk=1024)
    x1, h0, aff0 = _attn_out(ox, x, mod0, mla_w_out[0].astype(BF16), norm_ffn[0][None, :],
                             _pad_router(router_w[0]), tm=256)
    moe0 = _ec_moe(h0, aff0, exp_w_gate, exp_w_up, exp_w_down, 0)

    x3, h1, aff1 = _conv_mixer(x1, moe0, mod0[:, 5:6], mod1, norm_mix[1][None, :],
                               conv_w_in[0].astype(BF16), conv_w[0], conv_w_out[0].astype(BF16),
                               norm_ffn[1][None, :], _pad_router(router_w[1]), tm=256)
    moe1 = _ec_moe(h1, aff1, exp_w_gate, exp_w_up, exp_w_down, 1)
    return x3 + mod1[:, 5:6] * moe1
```

```python
import functools

import jax
import jax.numpy as jnp
from jax import lax
from jax.experimental import pallas as pl
from jax.experimental.pallas import tpu as pltpu

F32 = jnp.float32
BF16 = jnp.bfloat16
HIGHEST = lax.Precision.HIGHEST

GRID_W = 64
N_MIXERS = 2
MLA_HEADS = 8
QK_NOPE = 128
QK_ROPE = 64
QK_DIM = QK_NOPE + QK_ROPE
V_DIM = 128
Q_LORA = 384
KV_LORA = 256
ROPE_FREQS = QK_ROPE // 4
ROPE_THETA = 10000.0
ATTN_SCALE = QK_DIM ** -0.5
LOG2_E = 1.4426950408889634
N_EXPERTS = 16
EC_CAPACITY = 2
EPS = 1e-6

LANES = 128
SUBLANES = 8
VMEM_LIMIT = 56 * 1024 * 1024
MAX_UNROLLED_KEY_BLOCKS = 16


def _cparams(sem):
    return pltpu.CompilerParams(dimension_semantics=sem, vmem_limit_bytes=VMEM_LIMIT)


def _rms(x):
    return x * lax.rsqrt(jnp.mean(x * x, axis=-1, keepdims=True) + EPS)


def _modulate(x, g, shift, scale):
    return (_rms(x) * g) * (1.0 + scale) + shift


def _silu(a):
    return a * jax.nn.sigmoid(a)


def _adaln_kernel(c_ref, w_ref, b_ref, o_ref):
    s = _silu(c_ref[...])
    o_ref[0] = jnp.dot(s, w_ref[0], precision=HIGHEST, preferred_element_type=F32) + b_ref[0]


def _adaln(cond, ada_w, ada_b):
    depth, d, d6 = ada_w.shape
    tn = 1536
    return pl.pallas_call(
        _adaln_kernel,
        grid=(depth, d6 // tn),
        in_specs=[
            pl.BlockSpec((8, d), lambda l, j: (0, 0)),
            pl.BlockSpec((1, d, tn), lambda l, j: (l, 0, j)),
            pl.BlockSpec((1, 1, tn), lambda l, j: (l, 0, j)),
        ],
        out_specs=pl.BlockSpec((1, 8, tn), lambda l, j: (l, 0, j)),
        out_shape=jax.ShapeDtypeStruct((depth, 8, d6), F32),
        compiler_params=_cparams(("parallel", "parallel")),
        name="adaln",
    )(cond, ada_w, ada_b.reshape(depth, 1, d6))


def _mla_pre_kernel(x_ref, mod_ref, gmix_ref, win_ref, qnorm_ref, wqb_ref, kvnorm_ref, wkvb_ref,
                    gq_ref, gk_ref, cos_ref, sin_ref, *out_refs, with_q):
    if with_q:
        qn_ref, qr_ref, k_ref, v_ref = out_refs
    else:
        k_ref, v_ref = out_refs
    mod = mod_ref[0]
    hx = _modulate(x_ref[0], gmix_ref[...], mod[0:1], mod[1:2])
    lat = jnp.dot(hx.astype(BF16), win_ref[...], preferred_element_type=F32)
    cos = cos_ref[...]
    sin = sin_ref[...]
    lo = lax.broadcasted_iota(jnp.int32, (1, LANES), 1) < QK_ROPE

    def half_sums(v):
        v2 = v * v
        return (jnp.sum(jnp.where(lo, v2, 0.0), axis=-1, keepdims=True),
                jnp.sum(jnp.where(lo, 0.0, v2), axis=-1, keepdims=True))

    if with_q:
        gq = gq_ref[...]
        qn_in = (_rms(lat[:, :Q_LORA]) * qnorm_ref[...]).astype(BF16)
        qf = jnp.dot(qn_in, wqb_ref[...], preferred_element_type=F32)
        nope_w = MLA_HEADS * QK_NOPE
        pair_w = (MLA_HEADS // 2) * LANES
        for p in range(MLA_HEADS // 2):
            rp = qf[:, nope_w + p * LANES: nope_w + (p + 1) * LANES]
            sw = qf[:, nope_w + pair_w + p * LANES: nope_w + pair_w + (p + 1) * LANES]
            s_pair = half_sums(rp)
            r_pair = []
            for hh in range(2):
                h = 2 * p + hh
                nope = qf[:, h * QK_NOPE:(h + 1) * QK_NOPE]
                ms = (jnp.sum(nope * nope, axis=-1, keepdims=True) + s_pair[hh]) * (1.0 / QK_DIM)
                r = lax.rsqrt(ms + EPS) * (ATTN_SCALE * LOG2_E)
                r_pair.append(r)
                qn_ref[0, h] = ((nope * r) * gq[0:1]).astype(BF16)
            roped = (rp * gq[1:2]) * cos + (sw * gq[2:3]) * sin
            qr_ref[0, p] = (roped * jnp.where(lo, r_pair[0], r_pair[1])).astype(BF16)

    gk = gk_ref[...]
    kv_lo = Q_LORA
    kvn_in = (_rms(lat[:, kv_lo:kv_lo + KV_LORA]) * kvnorm_ref[...]).astype(BF16)
    kv = jnp.dot(kvn_in, wkvb_ref[...], preferred_element_type=F32)
    pe_lo = Q_LORA + KV_LORA
    kr2 = lat[:, pe_lo:pe_lo + LANES]
    ks2 = lat[:, pe_lo + LANES:pe_lo + 2 * LANES]
    s_pe = half_sums(kr2)[0]
    k_roped = (kr2 * gk[1:2]) * cos + (ks2 * gk[2:3]) * sin
    for h in range(MLA_HEADS):
        nope = kv[:, h * QK_NOPE:(h + 1) * QK_NOPE]
        ms = (jnp.sum(nope * nope, axis=-1, keepdims=True) + s_pe) * (1.0 / QK_DIM)
        r = lax.rsqrt(ms + EPS)
        k_ref[0, h, :, 0:QK_NOPE] = ((nope * r) * gk[0:1]).astype(BF16)
        keep = lo if h % 2 == 0 else jnp.logical_not(lo)
        k_ref[0, h, :, QK_NOPE:QK_NOPE + LANES] = jnp.where(keep, k_roped * r, 0.0).astype(BF16)
        v_off = MLA_HEADS * QK_NOPE + h * V_DIM
        v_ref[0, h, :, 0:V_DIM] = kv[:, v_off:v_off + V_DIM].astype(BF16)
        v_ref[0, h, :, V_DIM:2 * V_DIM] = jnp.ones((kv.shape[0], V_DIM), BF16)


def _mla_pre(x, mod, shared_mod, gmix, w, cos, sin, *, with_q, tm):
    b, n, d = x.shape
    tm = min(tm, n)
    const = lambda shape: pl.BlockSpec(shape, lambda bi, i: (0,) * len(shape))
    mod_map = (lambda bi, i: (0, 0, 0)) if shared_mod else (lambda bi, i: (bi, 0, 0))
    in_specs = [
        pl.BlockSpec((1, tm, d), lambda bi, i: (bi, i, 0)),
        pl.BlockSpec((1, 6, d), mod_map),
        const((1, d)),
        const(w["w_in"].shape),
        const((1, Q_LORA)),
        const(w["w_qb"].shape),
        const((1, KV_LORA)),
        const(w["w_kvb"].shape),
        const((3, LANES)),
        const((3, LANES)),
        pl.BlockSpec((tm, LANES), lambda bi, i: (i, 0)),
        pl.BlockSpec((tm, LANES), lambda bi, i: (i, 0)),
    ]
    head_spec = lambda nh, w_: pl.BlockSpec((1, nh, tm, w_), lambda bi, i: (bi, 0, i, 0))
    out_specs = [head_spec(MLA_HEADS, 2 * LANES), head_spec(MLA_HEADS, 2 * V_DIM)]
    out_shape = [jax.ShapeDtypeStruct((b, MLA_HEADS, n, 2 * LANES), BF16),
                 jax.ShapeDtypeStruct((b, MLA_HEADS, n, 2 * V_DIM), BF16)]
    if with_q:
        out_specs = [head_spec(MLA_HEADS, QK_NOPE), head_spec(MLA_HEADS // 2, LANES)] + out_specs
        out_shape = [jax.ShapeDtypeStruct((b, MLA_HEADS, n, QK_NOPE), BF16),
                     jax.ShapeDtypeStruct((b, MLA_HEADS // 2, n, LANES), BF16)] + out_shape
    return pl.pallas_call(
        functools.partial(_mla_pre_kernel, with_q=with_q),
        grid=(b, n // tm),
        in_specs=in_specs,
        out_specs=out_specs,
        out_shape=out_shape,
        compiler_params=_cparams(("parallel", "parallel")),
        name="mla_pre_q" if with_q else "mla_pre_ctx",
    )(x, mod, gmix, w["w_in"], w["q_norm"], w["w_qb"], w["kv_norm"], w["w_kvb"],
      w["gq"], w["gk"], cos, sin)


def _attn_kernel(qn_ref, qr_ref, kc_ref, vc_ref, kx_ref, vx_ref, o_ref, m_s, acc_s, sc_s, sa_s, sb_s, *, tk):
    q = jnp.concatenate([qn_ref[0, 0], qr_ref[0, 0]], axis=-1)
    nblk = kx_ref.shape[2] // tk

    def scores(k):
        return lax.dot_general(q, k, (((1,), (1,)), ((), ())), preferred_element_type=F32)

    def kblk(j):
        return kx_ref[0, 0, pl.ds(pl.multiple_of(j * tk, tk), tk), :]

    def vblk(j):
        return vx_ref[0, 0, pl.ds(pl.multiple_of(j * tk, tk), tk), :]

    def absorb(s_ref, v):
        s = s_ref[...]
        m_prev = m_s[...]
        m_new = jnp.maximum(m_prev, jnp.max(s, axis=-1, keepdims=True))
        p = jnp.exp2(s - m_new).astype(BF16)
        acc_s[...] = jnp.exp2(m_prev - m_new) * acc_s[...] + jnp.dot(p, v, preferred_element_type=F32)
        m_s[...] = m_new

    m_s[...] = jnp.full(m_s.shape, -jnp.inf, F32)
    acc_s[...] = jnp.zeros(acc_s.shape, F32)
    sc_s[...] = scores(kc_ref[0, 0])
    sa_s[...] = scores(kblk(0))
    absorb(sc_s, vc_ref[0, 0])

    if nblk <= MAX_UNROLLED_KEY_BLOCKS:
        bufs = (sa_s, sb_s)
        for j in range(nblk):
            if j + 1 < nblk:
                bufs[(j + 1) % 2][...] = scores(kx_ref[0, 0, (j + 1) * tk:(j + 2) * tk, :])
            absorb(bufs[j % 2], vx_ref[0, 0, j * tk:(j + 1) * tk, :])
    elif nblk % 2 == 0:
        def body(i, carry):
            j = 2 * i
            sb_s[...] = scores(kblk(j + 1))
            absorb(sa_s, vblk(j))
            sa_s[...] = scores(kblk(jnp.minimum(j + 2, nblk - 1)))
            absorb(sb_s, vblk(j + 1))
            return carry

        lax.fori_loop(0, nblk // 2, body, 0)
    else:
        absorb(sa_s, vblk(0))

        def body(j, carry):
            sa_s[...] = scores(kblk(j))
            absorb(sa_s, vblk(j))
            return carry

        lax.fori_loop(1, nblk, body, 0)
    acc = acc_s[...]
    o_ref[0] = (acc[:, :V_DIM] / acc[:, V_DIM:]).astype(o_ref.dtype)


def _attention(qn, qr, kc, vc, kx, vx, *, tq, tk):
    b, h, n, _ = qn.shape
    nc = kc.shape[2]
    tq = min(tq, n)
    tk = min(tk, n)
    whole = lambda rows: pl.BlockSpec((1, 1, rows, 2 * LANES), lambda bi, hi, i: (bi, hi, 0, 0))
    return pl.pallas_call(
        functools.partial(_attn_kernel, tk=tk),
        grid=(b, h, n // tq),
        in_specs=[
            pl.BlockSpec((1, 1, tq, QK_NOPE), lambda bi, hi, i: (bi, hi, i, 0)),
            pl.BlockSpec((1, 1, tq, LANES), lambda bi, hi, i: (bi, hi // 2, i, 0)),
            whole(nc), whole(nc), whole(n), whole(n),
        ],
        out_specs=pl.BlockSpec((1, tq, V_DIM), lambda bi, hi, i: (bi, i, hi)),
        out_shape=jax.ShapeDtypeStruct((b, n, h * V_DIM), BF16),
        scratch_shapes=[
            pltpu.VMEM((tq, 1), F32),
            pltpu.VMEM((tq, 2 * V_DIM), F32),
            pltpu.VMEM((tq, nc), F32),
            pltpu.VMEM((tq, tk), F32),
            pltpu.VMEM((tq, tk), F32),
        ],
        compiler_params=_cparams(("parallel", "parallel", "arbitrary")),
        name="flash_attn",
    )(qn, qr, kc, vc, kx, vx)


def _ffn_pre(x_new, mod, gffn_ref, rw_ref, h_ref, aff_ref):
    h2 = _modulate(x_new, gffn_ref[...], mod[3:4], mod[4:5])
    hi = h2.astype(BF16)
    lo = (h2 - hi.astype(F32)).astype(BF16)
    h_ref[0] = hi
    rw = rw_ref[...]
    t = jnp.dot(hi, rw, preferred_element_type=F32) + jnp.dot(lo, rw, preferred_element_type=F32)
    logits = t[:, :LANES] + t[:, LANES:]
    is_expert = lax.broadcasted_iota(jnp.int32, (1, LANES), 1) < N_EXPERTS
    logits = jnp.where(is_expert, logits, -jnp.inf)
    e = jnp.exp(logits - jnp.max(logits, axis=-1, keepdims=True))
    aff = e / jnp.sum(e, axis=-1, keepdims=True)
    aff_ref[0] = aff.T[:N_EXPERTS]


def _attn_out_kernel(o_ref, x_ref, mod_ref, wout_ref, gffn_ref, rw_ref, x1_ref, h_ref, aff_ref):
    mod = mod_ref[0]
    o = jnp.dot(o_ref[0], wout_ref[...], preferred_element_type=F32)
    x1 = x_ref[0] + mod[2:3] * o
    x1_ref[0] = x1
    _ffn_pre(x1, mod, gffn_ref, rw_ref, h_ref, aff_ref)


def _attn_out(ox, x, mod, w_out, gffn, rw, *, tm):
    b, n, d = x.shape
    tm = min(tm, n)
    const = lambda shape: pl.BlockSpec(shape, lambda bi, i: (0,) * len(shape))
    tok = lambda: pl.BlockSpec((1, tm, d), lambda bi, i: (bi, i, 0))
    return pl.pallas_call(
        _attn_out_kernel,
        grid=(b, n // tm),
        in_specs=[tok(), tok(), pl.BlockSpec((1, 6, d), lambda bi, i: (bi, 0, 0)),
                  const(w_out.shape), const((1, d)), const(rw.shape)],
        out_specs=[tok(), tok(), pl.BlockSpec((1, N_EXPERTS, tm), lambda bi, i: (bi, 0, i))],
        out_shape=[jax.ShapeDtypeStruct((b, n, d), F32),
                   jax.ShapeDtypeStruct((b, n, d), BF16),
                   jax.ShapeDtypeStruct((b, N_EXPERTS, n), F32)],
        compiler_params=_cparams(("parallel", "parallel")),
        name="attn_out",
    )(ox, x, mod, w_out, gffn, rw)


def _conv_kernel(xm_ref, xp_ref, xn_ref, mm_ref, mp_ref, mn_ref, gprev_ref, mod_ref, gmix_ref,
                 win_ref, cw_ref, wout_ref, gffn_ref, rw_ref, x3_ref, h_ref, aff_ref):
    i = pl.program_id(1)
    tm, d = xm_ref.shape[1], xm_ref.shape[2]
    halo = xp_ref.shape[1]
    gprev = gprev_ref[0]
    mod = mod_ref[0]
    xm = xm_ref[0] + gprev * _from_row_tiles(mm_ref)
    xe = jnp.concatenate([xp_ref[0] + gprev * _from_row_tiles(mp_ref), xm,
                          xn_ref[0] + gprev * _from_row_tiles(mn_ref)], axis=0)
    hx = _modulate(xe, gmix_ref[...], mod[0:1], mod[1:2]).astype(BF16)
    proj = jnp.dot(hx, win_ref[...], preferred_element_type=F32)
    u = proj[:, d:2 * d] * proj[:, 2 * d:3 * d]
    row = lax.broadcasted_iota(jnp.int32, (tm + 2 * halo, 1), 0)
    outside = jnp.logical_or(jnp.logical_and(i == 0, row < halo),
                             jnp.logical_and(i == pl.num_programs(1) - 1, row >= tm + halo))
    u = jnp.where(outside, 0.0, u)
    rows = tm + 2 * halo
    u_prev = pltpu.roll(u, 1, axis=0)[halo:halo + tm]
    u_next = pltpu.roll(u, rows - 1, axis=0)[halo:halo + tm]
    cw = cw_ref[...]
    y = cw[0:1] * u_prev + cw[1:2] * u[halo:halo + tm] + cw[2:3] * u_next
    z = (proj[halo:halo + tm, 0:d] * y).astype(BF16)
    x3 = xm + mod[2:3] * jnp.dot(z, wout_ref[...], preferred_element_type=F32)
    x3_ref[0] = x3
    _ffn_pre(x3, mod, gffn_ref, rw_ref, h_ref, aff_ref)


def _conv_mixer(x, moe, gprev, mod, gmix, w_in, cw, w_out, gffn, rw, *, tm):
    b, n, d = x.shape
    tm = min(tm, n)
    halo = 8
    nb = tm // halo
    last = n // halo - 1
    const = lambda shape: pl.BlockSpec(shape, lambda bi, i: (0,) * len(shape))
    tok = lambda: pl.BlockSpec((1, tm, d), lambda bi, i: (bi, i, 0))
    prev = lambda: pl.BlockSpec((1, halo, d), lambda bi, i: (bi, jnp.maximum(i * nb - 1, 0), 0))
    nxt = lambda: pl.BlockSpec((1, halo, d), lambda bi, i: (bi, jnp.minimum((i + 1) * nb, last), 0))
    rt = (SUBLANES, LANES)
    tok_rt = pl.BlockSpec((1, tm) + rt, lambda bi, i: (bi, i, 0, 0))
    prev_rt = pl.BlockSpec((1, halo) + rt, lambda bi, i: (bi, jnp.maximum(i * nb - 1, 0), 0, 0))
    nxt_rt = pl.BlockSpec((1, halo) + rt, lambda bi, i: (bi, jnp.minimum((i + 1) * nb, last), 0, 0))
    return pl.pallas_call(
        _conv_kernel,
        grid=(b, n // tm),
        in_specs=[tok(), prev(), nxt(), tok_rt, prev_rt, nxt_rt,
                  pl.BlockSpec((1, 1, d), lambda bi, i: (bi, 0, 0)),
                  pl.BlockSpec((1, 6, d), lambda bi, i: (bi, 0, 0)),
                  const((1, d)), const(w_in.shape), const(cw.shape), const(w_out.shape),
                  const((1, d)), const(rw.shape)],
        out_specs=[tok(), tok(), pl.BlockSpec((1, N_EXPERTS, tm), lambda bi, i: (bi, 0, i))],
        out_shape=[jax.ShapeDtypeStruct((b, n, d), F32),
                   jax.ShapeDtypeStruct((b, n, d), BF16),
                   jax.ShapeDtypeStruct((b, N_EXPERTS, n), F32)],
        compiler_params=_cparams(("parallel", "parallel")),
        name="conv_mixer",
    )(x, x, x, moe, moe, moe, gprev, mod, gmix, w_in, cw, w_out, gffn, rw)


def _moe_kernel(xs_ref, g_ref, wg_ref, wu_ref, wd_ref, y_ref, wg_s, wu_s, wd_s, *, tr, tf):
    c = xs_ref.shape[2]
    f_total = wg_s.shape[1]

    @pl.when(pl.program_id(1) == 0)
    def _():
        for src, dst in ((wg_ref, wg_s), (wu_ref, wu_s), (wd_ref, wd_s)):
            for r in range(0, dst.shape[0], tr):
                dst[r:r + tr, :] = src[0, 0, r:r + tr, :].astype(BF16)

    def rows(r, carry):
        r0 = pl.multiple_of(r * tr, tr)
        xs = xs_ref[0, 0, pl.ds(r0, tr), :]
        acc = jnp.zeros((tr, wd_s.shape[1]), F32)
        for f in range(f_total // tf):
            a = jnp.dot(xs, wg_s[:, f * tf:(f + 1) * tf], preferred_element_type=F32)
            u = jnp.dot(xs, wu_s[:, f * tf:(f + 1) * tf], preferred_element_type=F32)
            hm = (_silu(a) * u).astype(BF16)
            acc = acc + jnp.dot(hm, wd_s[f * tf:(f + 1) * tf, :], preferred_element_type=F32)
        y = acc * g_ref[0, 0, pl.ds(r0, tr), :]
        for k in range(SUBLANES):
            y_ref[0, 0, pl.ds(r0, tr), k, :] = y[:, k * LANES:(k + 1) * LANES]
        return carry

    lax.fori_loop(0, c // tr, rows, 0)


def _moe_ffn(xs, g, wg, wu, wd, layer):
    b, e, c, d = xs.shape
    f = wg.shape[3]
    tr = min(256, c)
    tf = min(512, f)
    w_spec = lambda rows_, cols: pl.BlockSpec((1, 1, rows_, cols), lambda ei, bi: (layer, ei, 0, 0))
    return pl.pallas_call(
        functools.partial(_moe_kernel, tr=tr, tf=tf),
        grid=(e, b),
        in_specs=[
            pl.BlockSpec((1, 1, c, d), lambda ei, bi: (bi, ei, 0, 0)),
            pl.BlockSpec((1, 1, c, 1), lambda ei, bi: (bi, ei, 0, 0)),
            w_spec(d, f), w_spec(d, f), w_spec(f, d),
        ],
        out_specs=pl.BlockSpec((1, 1, c, SUBLANES, LANES), lambda ei, bi: (bi, ei, 0, 0, 0)),
        out_shape=jax.ShapeDtypeStruct((b, e, c, SUBLANES, LANES), F32),
        scratch_shapes=[pltpu.VMEM((d, f), BF16), pltpu.VMEM((d, f), BF16), pltpu.VMEM((f, d), BF16)],
        compiler_params=_cparams(("arbitrary", "arbitrary")),
        name="moe_ffn",
    )(xs, g.reshape(b, e, c, 1), wg, wu, wd)


def _from_row_tiles(ref):
    return jnp.concatenate([ref[0, :, k, :] for k in range(SUBLANES)], axis=-1)


def _combine_kernel(idx_ref, y_ref, out_hbm, acc_s, sem, *, unroll, zero_rows):
    bi = pl.program_id(0)
    ei = pl.program_id(1)
    n = acc_s.shape[0]
    c = y_ref.shape[2]

    @pl.when(ei == 0)
    def _():
        def zero(i, carry):
            acc_s[pl.ds(pl.multiple_of(i * zero_rows, zero_rows), zero_rows)] = jnp.zeros(
                (zero_rows,) + acc_s.shape[1:], F32)
            return carry

        lax.fori_loop(0, n // zero_rows, zero, 0)

    def rows(i, carry):
        base = i * unroll
        tok = [idx_ref[0, 0, base + u] for u in range(unroll)]
        new = [acc_s[tok[u]] + y_ref[0, 0, base + u] for u in range(unroll)]
        for u in range(unroll):
            acc_s[tok[u]] = new[u]
        return carry

    lax.fori_loop(0, c // unroll, rows, 0)

    @pl.when(ei == pl.num_programs(1) - 1)
    def _():
        cp = pltpu.make_async_copy(acc_s, out_hbm.at[bi], sem)
        cp.start()
        cp.wait()


def _moe_combine(idx, y, n):
    b, e, c = idx.shape
    unroll = 8
    zero_rows = min(256, n)
    return pl.pallas_call(
        functools.partial(_combine_kernel, unroll=unroll, zero_rows=zero_rows),
        grid=(b, e),
        in_specs=[
            pl.BlockSpec((1, 1, c), lambda bi, ei: (bi * e + ei, 0, 0), memory_space=pltpu.SMEM),
            pl.BlockSpec((1, 1, c, SUBLANES, LANES), lambda bi, ei: (bi, ei, 0, 0, 0)),
        ],
        out_specs=pl.BlockSpec(memory_space=pl.ANY),
        out_shape=jax.ShapeDtypeStruct((b, n, SUBLANES, LANES), F32),
        scratch_shapes=[pltpu.VMEM((n, SUBLANES, LANES), F32), pltpu.SemaphoreType.DMA],
        compiler_params=_cparams(("arbitrary", "arbitrary")),
        name="moe_combine",
    )(idx.reshape(b * e, 1, c), y)


def _ec_moe(h, aff, wg, wu, wd, layer):
    b, n, d = h.shape
    assert d == SUBLANES * LANES
    cap = EC_CAPACITY * n // N_EXPERTS
    g, idx = lax.top_k(aff, cap)
    xs = jax.vmap(lambda hb, ib: hb[ib])(h, idx)
    y = _moe_ffn(xs, g, wg, wu, wd, layer)
    return _moe_combine(idx, y, n)


def _residual_kernel(x_ref, m_ref, g_ref, o_ref):
    o_ref[0] = x_ref[0] + g_ref[0] * _from_row_tiles(m_ref)


def _residual(x, moe, gate, *, tm):
    b, n, d = x.shape
    tm = min(tm, n)
    tok = lambda: pl.BlockSpec((1, tm, d), lambda bi, i: (bi, i, 0))
    return pl.pallas_call(
        _residual_kernel,
        grid=(b, n // tm),
        in_specs=[tok(), pl.BlockSpec((1, tm, SUBLANES, LANES), lambda bi, i: (bi, i, 0, 0)),
                  pl.BlockSpec((1, 1, d), lambda bi, i: (bi, 0, 0))],
        out_specs=tok(),
        out_shape=jax.ShapeDtypeStruct((b, n, d), F32),
        compiler_params=_cparams(("parallel", "parallel")),
        name="moe_residual",
    )(x, moe, gate)


def _rope_tables(n):
    rows = n // GRID_W
    row = jnp.broadcast_to(jnp.arange(rows, dtype=F32)[:, None], (rows, GRID_W)).reshape(-1)
    col = jnp.broadcast_to(jnp.arange(GRID_W, dtype=F32)[None, :], (rows, GRID_W)).reshape(-1)
    inv = ROPE_THETA ** (-jnp.arange(ROPE_FREQS, dtype=F32) / ROPE_FREQS)
    ar, ac = row[:, None] * inv, col[:, None] * inv
    cos = jnp.concatenate([jnp.cos(ar), jnp.cos(ar), jnp.cos(ac), jnp.cos(ac)], axis=-1)
    sin = jnp.concatenate([-jnp.sin(ar), jnp.sin(ar), -jnp.sin(ac), jnp.sin(ac)], axis=-1)
    return jnp.tile(cos, (1, 2)), jnp.tile(sin, (1, 2))


def _swap_perm():
    f = ROPE_FREQS
    base = jnp.arange(QK_ROPE)
    return jnp.where((base // f) % 2 == 0, base + f, base - f)


def _mla_weights(w_in, q_norm, w_qb, kv_norm, w_kvb, q_gain, k_gain):
    perm = _swap_perm()
    pe = w_in[:, Q_LORA + KV_LORA:]
    pe_sw = pe[:, perm]
    w_in_x = jnp.concatenate([w_in[:, :Q_LORA + KV_LORA], pe, pe, pe_sw, pe_sw], axis=1)
    wq = w_qb.reshape(Q_LORA, MLA_HEADS, QK_DIM)
    wq_rope = wq[:, :, QK_NOPE:]
    w_qb_x = jnp.concatenate([
        wq[:, :, :QK_NOPE].reshape(Q_LORA, -1),
        wq_rope.reshape(Q_LORA, -1),
        wq_rope[:, :, perm].reshape(Q_LORA, -1)], axis=1)
    wkv = w_kvb.reshape(KV_LORA, MLA_HEADS, QK_NOPE + V_DIM)
    w_kvb_x = jnp.concatenate([wkv[:, :, :QK_NOPE].reshape(KV_LORA, -1),
                               wkv[:, :, QK_NOPE:].reshape(KV_LORA, -1)], axis=1)

    def gains(g):
        gr = g[QK_NOPE:]
        return jnp.stack([g[:QK_NOPE], jnp.tile(gr, 2), jnp.tile(gr[perm], 2)])

    return {
        "w_in": w_in_x.astype(BF16), "q_norm": q_norm[None, :], "w_qb": w_qb_x.astype(BF16),
        "kv_norm": kv_norm[None, :], "w_kvb": w_kvb_x.astype(BF16),
        "gq": gains(q_gain), "gk": gains(k_gain),
    }


def _pad_router(rw):
    hi = rw.astype(BF16)
    lo = (rw - hi.astype(F32)).astype(BF16)
    pad = ((0, 0), (0, LANES - rw.shape[1]))
    return jnp.concatenate([jnp.pad(hi, pad), jnp.pad(lo, pad)], axis=1)


def kernel(x, c, ctx, c_ctx, norm_mix, norm_ffn, ada_w, ada_b, mla_w_in, mla_q_norm, mla_w_qb, mla_kv_norm, mla_w_kvb, mla_q_gain, mla_k_gain, mla_w_out, conv_w_in, conv_w, conv_w_out, router_w, exp_w_gate, exp_w_up, exp_w_down):
    b, n, d = x.shape
    nc = ctx.shape[1]
    depth = ada_w.shape[0]
    assert depth == 2 and b < 8

    cond = jnp.concatenate([c, c_ctx[None, :], jnp.zeros((8 - b - 1, d), F32)], axis=0)
    mod_all = _adaln(cond, ada_w, ada_b).reshape(depth, 8, 6, d)
    mod0, mod1 = mod_all[0, :b], mod_all[1, :b]
    mod0_ctx = mod_all[0, b:b + 1]

    w = _mla_weights(mla_w_in[0], mla_q_norm[0], mla_w_qb[0], mla_kv_norm[0], mla_w_kvb[0],
                     mla_q_gain[0], mla_k_gain[0])
    cos, sin = _rope_tables(n)
    gmix0 = norm_mix[0][None, :]
    qn, qr, kx, vx = _mla_pre(x, mod0, False, gmix0, w, cos, sin, with_q=True, tm=256)
    kc, vc = _mla_pre(ctx, mod0_ctx, True, gmix0, w, jnp.ones((nc, LANES), F32),
                      jnp.zeros((nc, LANES), F32), with_q=False, tm=256)
    ox = _attention(qn, qr, kc, vc, kx, vx, tq=512, tk=1024)
    x1, h0, aff0 = _attn_out(ox, x, mod0, mla_w_out[0].astype(BF16), norm_ffn[0][None, :],
                             _pad_router(router_w[0]), tm=256)
    moe0 = _ec_moe(h0, aff0, exp_w_gate, exp_w_up, exp_w_down, 0)

    x3, h1, aff1 = _conv_mixer(x1, moe0, mod0[:, 5:6], mod1, norm_mix[1][None, :],
                               conv_w_in[0].astype(BF16), conv_w[0], conv_w_out[0].astype(BF16),
                               norm_ffn[1][None, :], _pad_router(router_w[1]), tm=256)
    moe1 = _ec_moe(h1, aff1, exp_w_gate, exp_w_up, exp_w_down, 1)
    return _residual(x3, moe1, mod1[:, 5:6], tm=512)
```

```python
import functools

import jax
import jax.numpy as jnp
from jax import lax
from jax.experimental import pallas as pl
from jax.experimental.pallas import tpu as pltpu

F32 = jnp.float32
BF16 = jnp.bfloat16
HIGHEST = lax.Precision.HIGHEST

GRID_W = 64
N_MIXERS = 2
MLA_HEADS = 8
QK_NOPE = 128
QK_ROPE = 64
QK_DIM = QK_NOPE + QK_ROPE
V_DIM = 128
Q_LORA = 384
KV_LORA = 256
ROPE_FREQS = QK_ROPE // 4
ROPE_THETA = 10000.0
ATTN_SCALE = QK_DIM ** -0.5
LOG2_E = 1.4426950408889634
N_EXPERTS = 16
EC_CAPACITY = 2
EPS = 1e-6

LANES = 128
SUBLANES = 8
F32_INF_BITS = 0x7F800000
F32_VALUE_BITS = 31
VMEM_LIMIT = 56 * 1024 * 1024
MAX_UNROLLED_KEY_BLOCKS = 16
TOKEN_TILE = 512
ATTN_Q_TILE = 512
ATTN_KEY_TILE = 1024
SOFTMAX_ROWS = 32


def _cparams(sem):
    return pltpu.CompilerParams(dimension_semantics=sem, vmem_limit_bytes=VMEM_LIMIT)


def _rms(x):
    return x * lax.rsqrt(jnp.mean(x * x, axis=-1, keepdims=True) + EPS)


def _modulate(x, g, shift, scale):
    return (_rms(x) * g) * (1.0 + scale) + shift


def _silu(a):
    return a * jax.nn.sigmoid(a)


def _adaln_kernel(c_ref, w_ref, b_ref, o_ref):
    s = _silu(c_ref[...])
    o_ref[0] = jnp.dot(s, w_ref[0], precision=HIGHEST, preferred_element_type=F32) + b_ref[0]


def _adaln(cond, ada_w, ada_b):
    depth, d, d6 = ada_w.shape
    tn = 1536
    return pl.pallas_call(
        _adaln_kernel,
        grid=(depth, d6 // tn),
        in_specs=[
            pl.BlockSpec((8, d), lambda l, j: (0, 0)),
            pl.BlockSpec((1, d, tn), lambda l, j: (l, 0, j)),
            pl.BlockSpec((1, 1, tn), lambda l, j: (l, 0, j)),
        ],
        out_specs=pl.BlockSpec((1, 8, tn), lambda l, j: (l, 0, j)),
        out_shape=jax.ShapeDtypeStruct((depth, 8, d6), F32),
        compiler_params=_cparams(("parallel", "parallel")),
        name="adaln",
    )(cond, ada_w, ada_b.reshape(depth, 1, d6))


def _mla_pre_kernel(x_ref, mod_ref, gmix_ref, win_ref, qnorm_ref, wqb_ref, kvnorm_ref, wkvb_ref,
                    gq_ref, gk_ref, cos_ref, sin_ref, *out_refs, with_q):
    if with_q:
        qn_ref, qr_ref, k_ref, v_ref = out_refs
    else:
        k_ref, v_ref = out_refs
    mod = mod_ref[0]
    hx = _modulate(x_ref[0], gmix_ref[...], mod[0:1], mod[1:2])
    lat = jnp.dot(hx.astype(BF16), win_ref[...], preferred_element_type=F32)
    cos = cos_ref[...]
    sin = sin_ref[...]
    lo = lax.broadcasted_iota(jnp.int32, (1, LANES), 1) < QK_ROPE

    def half_sums(v):
        v2 = v * v
        return (jnp.sum(jnp.where(lo, v2, 0.0), axis=-1, keepdims=True),
                jnp.sum(jnp.where(lo, 0.0, v2), axis=-1, keepdims=True))

    if with_q:
        gq = gq_ref[...]
        qn_in = (_rms(lat[:, :Q_LORA]) * qnorm_ref[...]).astype(BF16)
        qf = jnp.dot(qn_in, wqb_ref[...], preferred_element_type=F32)
        nope_w = MLA_HEADS * QK_NOPE
        pair_w = (MLA_HEADS // 2) * LANES
        for p in range(MLA_HEADS // 2):
            rp = qf[:, nope_w + p * LANES: nope_w + (p + 1) * LANES]
            sw = qf[:, nope_w + pair_w + p * LANES: nope_w + pair_w + (p + 1) * LANES]
            s_pair = half_sums(rp)
            r_pair = []
            for hh in range(2):
                h = 2 * p + hh
                nope = qf[:, h * QK_NOPE:(h + 1) * QK_NOPE]
                ms = (jnp.sum(nope * nope, axis=-1, keepdims=True) + s_pair[hh]) * (1.0 / QK_DIM)
                r = lax.rsqrt(ms + EPS) * (ATTN_SCALE * LOG2_E)
                r_pair.append(r)
                qn_ref[0, h] = ((nope * r) * gq[0:1]).astype(BF16)
            roped = (rp * gq[1:2]) * cos + (sw * gq[2:3]) * sin
            qr_ref[0, p] = (roped * jnp.where(lo, r_pair[0], r_pair[1])).astype(BF16)

    gk = gk_ref[...]
    kv_lo = Q_LORA
    kvn_in = (_rms(lat[:, kv_lo:kv_lo + KV_LORA]) * kvnorm_ref[...]).astype(BF16)
    kv = jnp.dot(kvn_in, wkvb_ref[...], preferred_element_type=F32)
    pe_lo = Q_LORA + KV_LORA
    kr2 = lat[:, pe_lo:pe_lo + LANES]
    ks2 = lat[:, pe_lo + LANES:pe_lo + 2 * LANES]
    s_pe = half_sums(kr2)[0]
    k_roped = (kr2 * gk[1:2]) * cos + (ks2 * gk[2:3]) * sin
    for h in range(MLA_HEADS):
        nope = kv[:, h * QK_NOPE:(h + 1) * QK_NOPE]
        ms = (jnp.sum(nope * nope, axis=-1, keepdims=True) + s_pe) * (1.0 / QK_DIM)
        r = lax.rsqrt(ms + EPS)
        k_ref[0, h, :, 0:QK_NOPE] = ((nope * r) * gk[0:1]).astype(BF16)
        keep = lo if h % 2 == 0 else jnp.logical_not(lo)
        k_ref[0, h, :, QK_NOPE:QK_NOPE + LANES] = jnp.where(keep, k_roped * r, 0.0).astype(BF16)
        v_off = MLA_HEADS * QK_NOPE + h * V_DIM
        v_ref[0, h, :, 0:V_DIM] = kv[:, v_off:v_off + V_DIM].astype(BF16)
        v_ref[0, h, :, V_DIM:2 * V_DIM] = jnp.ones((kv.shape[0], V_DIM), BF16)


def _mla_pre(x, mod, shared_mod, gmix, w, cos, sin, *, with_q, tm):
    b, n, d = x.shape
    tm = min(tm, n)
    const = lambda shape: pl.BlockSpec(shape, lambda bi, i: (0,) * len(shape))
    mod_map = (lambda bi, i: (0, 0, 0)) if shared_mod else (lambda bi, i: (bi, 0, 0))
    in_specs = [
        pl.BlockSpec((1, tm, d), lambda bi, i: (bi, i, 0)),
        pl.BlockSpec((1, 6, d), mod_map),
        const((1, d)),
        const(w["w_in"].shape),
        const((1, Q_LORA)),
        const(w["w_qb"].shape),
        const((1, KV_LORA)),
        const(w["w_kvb"].shape),
        const((3, LANES)),
        const((3, LANES)),
        pl.BlockSpec((tm, LANES), lambda bi, i: (i, 0)),
        pl.BlockSpec((tm, LANES), lambda bi, i: (i, 0)),
    ]
    head_spec = lambda nh, w_: pl.BlockSpec((1, nh, tm, w_), lambda bi, i: (bi, 0, i, 0))
    out_specs = [head_spec(MLA_HEADS, 2 * LANES), head_spec(MLA_HEADS, 2 * V_DIM)]
    out_shape = [jax.ShapeDtypeStruct((b, MLA_HEADS, n, 2 * LANES), BF16),
                 jax.ShapeDtypeStruct((b, MLA_HEADS, n, 2 * V_DIM), BF16)]
    if with_q:
        out_specs = [head_spec(MLA_HEADS, QK_NOPE), head_spec(MLA_HEADS // 2, LANES)] + out_specs
        out_shape = [jax.ShapeDtypeStruct((b, MLA_HEADS, n, QK_NOPE), BF16),
                     jax.ShapeDtypeStruct((b, MLA_HEADS // 2, n, LANES), BF16)] + out_shape
    return pl.pallas_call(
        functools.partial(_mla_pre_kernel, with_q=with_q),
        grid=(b, n // tm),
        in_specs=in_specs,
        out_specs=out_specs,
        out_shape=out_shape,
        compiler_params=_cparams(("parallel", "parallel")),
        name="mla_pre_q" if with_q else "mla_pre_ctx",
    )(x, mod, gmix, w["w_in"], w["q_norm"], w["w_qb"], w["kv_norm"], w["w_kvb"],
      w["gq"], w["gk"], cos, sin)


def _attn_kernel(qn_ref, qr_ref, kc_ref, vc_ref, kx_ref, vx_ref, o_ref,
                 m_s, acc_s, sc_s, pc_s, s0_s, s1_s, p0_s, p1_s, a0_s, a1_s, *, tk):
    q = jnp.concatenate([qn_ref[0, 0], qr_ref[0, 0]], axis=-1)
    nblk = 1 + kx_ref.shape[2] // tk
    s_buf = lambda i: sc_s if i == 0 else (s0_s, s1_s)[i % 2]
    p_buf = lambda i: pc_s if i == 0 else (p0_s, p1_s)[i % 2]
    a_buf = lambda i: (a0_s, a1_s)[i % 2]
    keys = lambda i: kc_ref[0, 0] if i == 0 else kx_ref[0, 0, (i - 1) * tk:i * tk, :]
    vals = lambda i: vc_ref[0, 0] if i == 0 else vx_ref[0, 0, (i - 1) * tk:i * tk, :]

    def scores(i):
        s_buf(i)[...] = lax.dot_general(q, keys(i), (((1,), (1,)), ((), ())), preferred_element_type=F32)

    def softmax(i):
        for r in range(0, q.shape[0], SOFTMAX_ROWS):
            rows = slice(r, r + SOFTMAX_ROWS)
            s = s_buf(i)[rows, :]
            m_prev = m_s[rows, :]
            m_new = jnp.maximum(m_prev, jnp.max(s, axis=-1, keepdims=True))
            p_buf(i)[rows, :] = jnp.exp2(s - m_new).astype(BF16)
            a_buf(i)[rows, :] = jnp.exp2(m_prev - m_new)
            m_s[rows, :] = m_new

    def accumulate(i):
        acc_s[...] = a_buf(i)[...] * acc_s[...] + jnp.dot(p_buf(i)[...], vals(i), preferred_element_type=F32)

    m_s[...] = jnp.full(m_s.shape, -jnp.inf, F32)
    acc_s[...] = jnp.zeros(acc_s.shape, F32)
    scores(0)
    scores(1)
    softmax(0)
    for i in range(nblk):
        if i + 2 < nblk:
            scores(i + 2)
        if i + 1 < nblk:
            softmax(i + 1)
        accumulate(i)
    acc = acc_s[...]
    o_ref[0] = (acc[:, :V_DIM] / acc[:, V_DIM:]).astype(o_ref.dtype)


def _attention(qn, qr, kc, vc, kx, vx, *, tq, tk):
    b, h, n, _ = qn.shape
    nc = kc.shape[2]
    tq = min(tq, n)
    tk = min(tk, n)
    assert n // tk <= MAX_UNROLLED_KEY_BLOCKS
    whole = lambda rows: pl.BlockSpec((1, 1, rows, 2 * LANES), lambda bi, hi, i: (bi, hi, 0, 0))
    return pl.pallas_call(
        functools.partial(_attn_kernel, tk=tk),
        grid=(b, h, n // tq),
        in_specs=[
            pl.BlockSpec((1, 1, tq, QK_NOPE), lambda bi, hi, i: (bi, hi, i, 0)),
            pl.BlockSpec((1, 1, tq, LANES), lambda bi, hi, i: (bi, hi // 2, i, 0)),
            whole(nc), whole(nc), whole(n), whole(n),
        ],
        out_specs=pl.BlockSpec((1, tq, V_DIM), lambda bi, hi, i: (bi, i, hi)),
        out_shape=jax.ShapeDtypeStruct((b, n, h * V_DIM), BF16),
        scratch_shapes=[
            pltpu.VMEM((tq, 1), F32), pltpu.VMEM((tq, 2 * V_DIM), F32),
            pltpu.VMEM((tq, nc), F32), pltpu.VMEM((tq, nc), BF16),
            pltpu.VMEM((tq, tk), F32), pltpu.VMEM((tq, tk), F32),
            pltpu.VMEM((tq, tk), BF16), pltpu.VMEM((tq, tk), BF16),
            pltpu.VMEM((tq, 1), F32), pltpu.VMEM((tq, 1), F32),
        ],
        compiler_params=_cparams(("parallel", "parallel", "arbitrary")),
        name="flash_attn",
    )(qn, qr, kc, vc, kx, vx)


def _ffn_pre(x_new, mod, gffn_ref, rw_ref, h_ref, aff_ref):
    h2 = _modulate(x_new, gffn_ref[...], mod[3:4], mod[4:5])
    hi = h2.astype(BF16)
    lo = (h2 - hi.astype(F32)).astype(BF16)
    h_ref[0] = hi
    rw = rw_ref[...]
    t = jnp.dot(hi, rw, preferred_element_type=F32) + jnp.dot(lo, rw, preferred_element_type=F32)
    logits = t[:, :LANES] + t[:, LANES:]
    is_expert = lax.broadcasted_iota(jnp.int32, (1, LANES), 1) < N_EXPERTS
    logits = jnp.where(is_expert, logits, -jnp.inf)
    e = jnp.exp(logits - jnp.max(logits, axis=-1, keepdims=True))
    aff = e / jnp.sum(e, axis=-1, keepdims=True)
    aff_ref[0] = aff.T[:N_EXPERTS]


def _attn_out_kernel(o_ref, x_ref, mod_ref, wout_ref, gffn_ref, rw_ref, x1_ref, h_ref, aff_ref):
    mod = mod_ref[0]
    o = jnp.dot(o_ref[0], wout_ref[...], preferred_element_type=F32)
    x1 = x_ref[0] + mod[2:3] * o
    x1_ref[0] = x1
    _ffn_pre(x1, mod, gffn_ref, rw_ref, h_ref, aff_ref)


def _attn_out(ox, x, mod, w_out, gffn, rw, *, tm):
    b, n, d = x.shape
    tm = min(tm, n)
    const = lambda shape: pl.BlockSpec(shape, lambda bi, i: (0,) * len(shape))
    tok = lambda: pl.BlockSpec((1, tm, d), lambda bi, i: (bi, i, 0))
    return pl.pallas_call(
        _attn_out_kernel,
        grid=(b, n // tm),
        in_specs=[tok(), tok(), pl.BlockSpec((1, 6, d), lambda bi, i: (bi, 0, 0)),
                  const(w_out.shape), const((1, d)), const(rw.shape)],
        out_specs=[tok(), tok(), pl.BlockSpec((1, N_EXPERTS, tm), lambda bi, i: (bi, 0, i))],
        out_shape=[jax.ShapeDtypeStruct((b, n, d), F32),
                   jax.ShapeDtypeStruct((b, n, d), BF16),
                   jax.ShapeDtypeStruct((b, N_EXPERTS, n), F32)],
        compiler_params=_cparams(("parallel", "parallel")),
        name="attn_out",
    )(ox, x, mod, w_out, gffn, rw)


def _conv_kernel(xm_ref, xp_ref, xn_ref, mm_ref, mp_ref, mn_ref, gprev_ref, mod_ref, gmix_ref,
                 win_ref, cw_ref, wout_ref, gffn_ref, rw_ref, x3_ref, h_ref, aff_ref):
    i = pl.program_id(1)
    tm, d = xm_ref.shape[1], xm_ref.shape[2]
    halo = xp_ref.shape[1]
    gprev = gprev_ref[0]
    mod = mod_ref[0]
    xm = xm_ref[0] + gprev * _from_row_tiles(mm_ref)
    xe = jnp.concatenate([xp_ref[0] + gprev * _from_row_tiles(mp_ref), xm,
                          xn_ref[0] + gprev * _from_row_tiles(mn_ref)], axis=0)
    hx = _modulate(xe, gmix_ref[...], mod[0:1], mod[1:2]).astype(BF16)
    proj = jnp.dot(hx, win_ref[...], preferred_element_type=F32)
    u = proj[:, d:2 * d] * proj[:, 2 * d:3 * d]
    row = lax.broadcasted_iota(jnp.int32, (tm + 2 * halo, 1), 0)
    outside = jnp.logical_or(jnp.logical_and(i == 0, row < halo),
                             jnp.logical_and(i == pl.num_programs(1) - 1, row >= tm + halo))
    u = jnp.where(outside, 0.0, u)
    rows = tm + 2 * halo
    u_prev = pltpu.roll(u, 1, axis=0)[halo:halo + tm]
    u_next = pltpu.roll(u, rows - 1, axis=0)[halo:halo + tm]
    cw = cw_ref[...]
    y = cw[0:1] * u_prev + cw[1:2] * u[halo:halo + tm] + cw[2:3] * u_next
    z = (proj[halo:halo + tm, 0:d] * y).astype(BF16)
    x3 = xm + mod[2:3] * jnp.dot(z, wout_ref[...], preferred_element_type=F32)
    x3_ref[0] = x3
    _ffn_pre(x3, mod, gffn_ref, rw_ref, h_ref, aff_ref)


def _conv_mixer(x, moe, gprev, mod, gmix, w_in, cw, w_out, gffn, rw, *, tm):
    b, n, d = x.shape
    tm = min(tm, n)
    halo = 8
    nb = tm // halo
    last = n // halo - 1
    const = lambda shape: pl.BlockSpec(shape, lambda bi, i: (0,) * len(shape))
    tok = lambda: pl.BlockSpec((1, tm, d), lambda bi, i: (bi, i, 0))
    prev = lambda: pl.BlockSpec((1, halo, d), lambda bi, i: (bi, jnp.maximum(i * nb - 1, 0), 0))
    nxt = lambda: pl.BlockSpec((1, halo, d), lambda bi, i: (bi, jnp.minimum((i + 1) * nb, last), 0))
    rt = (SUBLANES, LANES)
    tok_rt = pl.BlockSpec((1, tm) + rt, lambda bi, i: (bi, i, 0, 0))
    prev_rt = pl.BlockSpec((1, halo) + rt, lambda bi, i: (bi, jnp.maximum(i * nb - 1, 0), 0, 0))
    nxt_rt = pl.BlockSpec((1, halo) + rt, lambda bi, i: (bi, jnp.minimum((i + 1) * nb, last), 0, 0))
    return pl.pallas_call(
        _conv_kernel,
        grid=(b, n // tm),
        in_specs=[tok(), prev(), nxt(), tok_rt, prev_rt, nxt_rt,
                  pl.BlockSpec((1, 1, d), lambda bi, i: (bi, 0, 0)),
                  pl.BlockSpec((1, 6, d), lambda bi, i: (bi, 0, 0)),
                  const((1, d)), const(w_in.shape), const(cw.shape), const(w_out.shape),
                  const((1, d)), const(rw.shape)],
        out_specs=[tok(), tok(), pl.BlockSpec((1, N_EXPERTS, tm), lambda bi, i: (bi, 0, i))],
        out_shape=[jax.ShapeDtypeStruct((b, n, d), F32),
                   jax.ShapeDtypeStruct((b, n, d), BF16),
                   jax.ShapeDtypeStruct((b, N_EXPERTS, n), F32)],
        compiler_params=_cparams(("parallel", "parallel")),
        name="conv_mixer",
    )(x, x, x, moe, moe, moe, gprev, mod, gmix, w_in, cw, w_out, gffn, rw)


def _moe_kernel(xs_ref, g_ref, wg_ref, wu_ref, wd_ref, y_ref, wg_s, wu_s, wd_s, *, tr, tf):
    c = xs_ref.shape[2]
    f_total = wg_s.shape[1]

    @pl.when(pl.program_id(1) == 0)
    def _():
        for src, dst in ((wg_ref, wg_s), (wu_ref, wu_s), (wd_ref, wd_s)):
            for r in range(0, dst.shape[0], tr):
                dst[r:r + tr, :] = src[0, 0, r:r + tr, :].astype(BF16)

    for r0 in range(0, c, tr):
        xs = xs_ref[0, 0, r0:r0 + tr, :]
        acc = jnp.zeros((tr, wd_s.shape[1]), F32)
        for f in range(f_total // tf):
            a = jnp.dot(xs, wg_s[:, f * tf:(f + 1) * tf], preferred_element_type=F32)
            u = jnp.dot(xs, wu_s[:, f * tf:(f + 1) * tf], preferred_element_type=F32)
            hm = (_silu(a) * u).astype(BF16)
            acc = acc + jnp.dot(hm, wd_s[f * tf:(f + 1) * tf, :], preferred_element_type=F32)
        y = acc * g_ref[0, 0, r0:r0 + tr, :]
        for k in range(SUBLANES):
            y_ref[0, 0, r0:r0 + tr, k, :] = y[:, k * LANES:(k + 1) * LANES]


def _moe_ffn(xs, g, wg, wu, wd, layer):
    b, e, c, d = xs.shape
    f = wg.shape[3]
    tr = min(256, c)
    tf = min(512, f)
    w_spec = lambda rows_, cols: pl.BlockSpec((1, 1, rows_, cols), lambda ei, bi: (layer, ei, 0, 0))
    return pl.pallas_call(
        functools.partial(_moe_kernel, tr=tr, tf=tf),
        grid=(e, b),
        in_specs=[
            pl.BlockSpec((1, 1, c, d), lambda ei, bi: (bi, ei, 0, 0)),
            pl.BlockSpec((1, 1, c, 1), lambda ei, bi: (bi, ei, 0, 0)),
            w_spec(d, f), w_spec(d, f), w_spec(f, d),
        ],
        out_specs=pl.BlockSpec((1, 1, c, SUBLANES, LANES), lambda ei, bi: (bi, ei, 0, 0, 0)),
        out_shape=jax.ShapeDtypeStruct((b, e, c, SUBLANES, LANES), F32),
        scratch_shapes=[pltpu.VMEM((d, f), BF16), pltpu.VMEM((d, f), BF16), pltpu.VMEM((f, d), BF16)],
        compiler_params=_cparams(("arbitrary", "arbitrary")),
        name="moe_ffn",
    )(xs, g.reshape(b, e, c, 1), wg, wu, wd)


def _from_row_tiles(ref):
    return jnp.concatenate([ref[0, :, k, :] for k in range(SUBLANES)], axis=-1)


def _combine_kernel(idx_ref, y_ref, out_hbm, acc_s, sem, *, unroll, zero_rows):
    bi = pl.program_id(0)
    ei = pl.program_id(1)
    n = acc_s.shape[0]
    c = y_ref.shape[2]

    @pl.when(ei == 0)
    def _():
        def zero(i, carry):
            acc_s[pl.ds(pl.multiple_of(i * zero_rows, zero_rows), zero_rows)] = jnp.zeros(
                (zero_rows,) + acc_s.shape[1:], F32)
            return carry

        lax.fori_loop(0, n // zero_rows, zero, 0)

    def rows(i, carry):
        base = i * unroll
        tok = [idx_ref[0, 0, base + u] for u in range(unroll)]
        new = [acc_s[tok[u]] + y_ref[0, 0, base + u] for u in range(unroll)]
        for u in range(unroll):
            acc_s[tok[u]] = new[u]
        return carry

    lax.fori_loop(0, c // unroll, rows, 0)

    @pl.when(ei == pl.num_programs(1) - 1)
    def _():
        cp = pltpu.make_async_copy(acc_s, out_hbm.at[bi], sem)
        cp.start()
        cp.wait()


def _moe_combine(idx, y, n):
    b, e, c = idx.shape
    unroll = 8
    zero_rows = min(256, n)
    return pl.pallas_call(
        functools.partial(_combine_kernel, unroll=unroll, zero_rows=zero_rows),
        grid=(b, e),
        in_specs=[
            pl.BlockSpec((1, 1, c), lambda bi, ei: (bi * e + ei, 0, 0), memory_space=pltpu.SMEM),
            pl.BlockSpec((1, 1, c, SUBLANES, LANES), lambda bi, ei: (bi, ei, 0, 0, 0)),
        ],
        out_specs=pl.BlockSpec(memory_space=pl.ANY),
        out_shape=jax.ShapeDtypeStruct((b, n, SUBLANES, LANES), F32),
        scratch_shapes=[pltpu.VMEM((n, SUBLANES, LANES), F32), pltpu.SemaphoreType.DMA],
        compiler_params=_cparams(("arbitrary", "arbitrary")),
        name="moe_combine",
    )(idx.reshape(b * e, 1, c), y)


def _route_kernel(aff_ref, uexcl_ref, uinclt_ref, ones_ref, lstrict_ref, idx_ref, gt_s, eq_s, need_s, *, cap):
    e_n, nch, _ = aff_ref.shape[1:]
    aff = aff_ref[0]

    def count(mask):
        return jnp.sum(jnp.sum(mask.astype(F32), axis=2, keepdims=True), axis=1, keepdims=True)

    def search(_, carry):
        lo, hi = carry
        mid = lo + ((hi - lo + 1) >> 1)
        ok = count(aff >= pltpu.bitcast(mid, F32)) >= cap
        return jnp.where(ok, mid, lo), jnp.where(ok, hi, mid - 1)

    lo0 = jnp.zeros((e_n, 1, 1), jnp.int32)
    hi0 = jnp.full((e_n, 1, 1), F32_INF_BITS, jnp.int32)
    tau_bits, _ = lax.fori_loop(0, F32_VALUE_BITS, search, (lo0, hi0))
    tau = pltpu.bitcast(tau_bits, F32)
    gt = aff > tau
    eq = aff == tau
    gt_s[...] = gt.astype(BF16)
    eq_s[...] = eq.astype(BF16)
    need_s[...] = jnp.broadcast_to(cap - count(gt), need_s.shape)

    n_slot = idx_ref.shape[3]
    slot = lax.broadcasted_iota(jnp.int32, (1, n_slot), 1).astype(F32)
    chunk_id = lax.broadcasted_iota(jnp.int32, (nch, n_slot), 0).astype(F32)
    widen = lambda a: jnp.concatenate([a] * (n_slot // LANES), axis=1)

    def chunk_prefix(mask):
        cnt = jnp.dot(mask, ones_ref[...], preferred_element_type=F32)
        return cnt, jnp.dot(lstrict_ref[...], cnt.astype(BF16), preferred_element_type=F32)

    def per_expert(e, carry):
        eq_e = eq_s[e]
        _, eq_start = chunk_prefix(eq_e)
        eq_rank = eq_start + jnp.dot(eq_e, uexcl_ref[...], preferred_element_type=F32)
        sel = gt_s[e] + jnp.where(eq_rank < need_s[e], eq_e, jnp.zeros_like(eq_e))
        cnt, start = chunk_prefix(sel)
        incl_t = lax.dot_general(uinclt_ref[...], sel, (((1,), (1,)), ((), ())),
                                 preferred_element_type=F32)
        start_w, cnt_w = widen(start), widen(cnt)
        hit = jnp.logical_and(start_w <= slot, slot < start_w + cnt_w)
        hit_f = hit.astype(F32)
        slot_start = jnp.sum(hit_f * start_w, axis=0, keepdims=True)
        slot_chunk = jnp.sum(hit_f * chunk_id, axis=0, keepdims=True)
        incl_of_slot = jnp.dot(incl_t.astype(BF16), hit.astype(BF16), preferred_element_type=F32)
        local = jnp.sum((incl_of_slot <= slot - slot_start).astype(F32), axis=0, keepdims=True)
        idx_ref[0, e] = (slot_chunk * LANES + local).astype(jnp.int32)
        return carry

    lax.fori_loop(0, e_n, per_expert, 0)


def _route(aff, cap):
    b, e, n = aff.shape
    assert n % LANES == 0 and cap % LANES == 0
    nch = n // LANES
    i = jnp.arange(LANES)
    t = jnp.arange(nch)
    uexcl = (i[:, None] < i[None, :]).astype(BF16)
    uinclt = (i[None, :] <= i[:, None]).astype(BF16)
    lstrict = (t[None, :] < t[:, None]).astype(BF16)
    const = lambda shape: pl.BlockSpec(shape, lambda bi: (0,) * len(shape))
    mask_scratch = lambda dt: pltpu.VMEM((e, nch, LANES), dt)
    idx = pl.pallas_call(
        functools.partial(_route_kernel, cap=cap),
        grid=(b,),
        in_specs=[pl.BlockSpec((1, e, nch, LANES), lambda bi: (bi, 0, 0, 0)),
                  const((LANES, LANES)), const((LANES, LANES)), const((LANES, LANES)), const((nch, nch))],
        out_specs=pl.BlockSpec((1, e, 1, cap), lambda bi: (bi, 0, 0, 0)),
        out_shape=jax.ShapeDtypeStruct((b, e, 1, cap), jnp.int32),
        scratch_shapes=[mask_scratch(BF16), mask_scratch(BF16), mask_scratch(F32)],
        compiler_params=_cparams(("parallel",)),
        name="moe_route",
    )(aff.reshape(b, e, nch, LANES), uexcl, uinclt, jnp.ones((LANES, LANES), BF16), lstrict)
    return idx.reshape(b, e, cap)


def _ec_moe(h, aff, wg, wu, wd, layer):
    b, n, d = h.shape
    assert d == SUBLANES * LANES
    cap = EC_CAPACITY * n // N_EXPERTS
    idx = _route(aff, cap)
    g = jnp.take_along_axis(aff, idx, axis=-1)
    xs = jax.vmap(lambda hb, ib: hb[ib])(h, idx)
    y = _moe_ffn(xs, g, wg, wu, wd, layer)
    return _moe_combine(idx, y, n)


def _residual_kernel(x_ref, m_ref, g_ref, o_ref):
    o_ref[0] = x_ref[0] + g_ref[0] * _from_row_tiles(m_ref)


def _residual(x, moe, gate, *, tm):
    b, n, d = x.shape
    tm = min(tm, n)
    tok = lambda: pl.BlockSpec((1, tm, d), lambda bi, i: (bi, i, 0))
    return pl.pallas_call(
        _residual_kernel,
        grid=(b, n // tm),
        in_specs=[tok(), pl.BlockSpec((1, tm, SUBLANES, LANES), lambda bi, i: (bi, i, 0, 0)),
                  pl.BlockSpec((1, 1, d), lambda bi, i: (bi, 0, 0))],
        out_specs=tok(),
        out_shape=jax.ShapeDtypeStruct((b, n, d), F32),
        compiler_params=_cparams(("parallel", "parallel")),
        name="moe_residual",
    )(x, moe, gate)


def _rope_tables(n):
    rows = n // GRID_W
    row = jnp.broadcast_to(jnp.arange(rows, dtype=F32)[:, None], (rows, GRID_W)).reshape(-1)
    col = jnp.broadcast_to(jnp.arange(GRID_W, dtype=F32)[None, :], (rows, GRID_W)).reshape(-1)
    inv = ROPE_THETA ** (-jnp.arange(ROPE_FREQS, dtype=F32) / ROPE_FREQS)
    ar, ac = row[:, None] * inv, col[:, None] * inv
    cos = jnp.concatenate([jnp.cos(ar), jnp.cos(ar), jnp.cos(ac), jnp.cos(ac)], axis=-1)
    sin = jnp.concatenate([-jnp.sin(ar), jnp.sin(ar), -jnp.sin(ac), jnp.sin(ac)], axis=-1)
    return jnp.tile(cos, (1, 2)), jnp.tile(sin, (1, 2))


def _swap_perm():
    f = ROPE_FREQS
    base = jnp.arange(QK_ROPE)
    return jnp.where((base // f) % 2 == 0, base + f, base - f)


def _mla_weights(w_in, q_norm, w_qb, kv_norm, w_kvb, q_gain, k_gain):
    perm = _swap_perm()
    pe = w_in[:, Q_LORA + KV_LORA:]
    pe_sw = pe[:, perm]
    w_in_x = jnp.concatenate([w_in[:, :Q_LORA + KV_LORA], pe, pe, pe_sw, pe_sw], axis=1)
    wq = w_qb.reshape(Q_LORA, MLA_HEADS, QK_DIM)
    wq_rope = wq[:, :, QK_NOPE:]
    w_qb_x = jnp.concatenate([
        wq[:, :, :QK_NOPE].reshape(Q_LORA, -1),
        wq_rope.reshape(Q_LORA, -1),
        wq_rope[:, :, perm].reshape(Q_LORA, -1)], axis=1)
    wkv = w_kvb.reshape(KV_LORA, MLA_HEADS, QK_NOPE + V_DIM)
    w_kvb_x = jnp.concatenate([wkv[:, :, :QK_NOPE].reshape(KV_LORA, -1),
                               wkv[:, :, QK_NOPE:].reshape(KV_LORA, -1)], axis=1)

    def gains(g):
        gr = g[QK_NOPE:]
        return jnp.stack([g[:QK_NOPE], jnp.tile(gr, 2), jnp.tile(gr[perm], 2)])

    return {
        "w_in": w_in_x.astype(BF16), "q_norm": q_norm[None, :], "w_qb": w_qb_x.astype(BF16),
        "kv_norm": kv_norm[None, :], "w_kvb": w_kvb_x.astype(BF16),
        "gq": gains(q_gain), "gk": gains(k_gain),
    }


def _pad_router(rw):
    hi = rw.astype(BF16)
    lo = (rw - hi.astype(F32)).astype(BF16)
    pad = ((0, 0), (0, LANES - rw.shape[1]))
    return jnp.concatenate([jnp.pad(hi, pad), jnp.pad(lo, pad)], axis=1)


def kernel(x, c, ctx, c_ctx, norm_mix, norm_ffn, ada_w, ada_b, mla_w_in, mla_q_norm, mla_w_qb, mla_kv_norm, mla_w_kvb, mla_q_gain, mla_k_gain, mla_w_out, conv_w_in, conv_w, conv_w_out, router_w, exp_w_gate, exp_w_up, exp_w_down):
    b, n, d = x.shape
    nc = ctx.shape[1]
    depth = ada_w.shape[0]
    assert depth == 2 and b < 8

    cond = jnp.concatenate([c, c_ctx[None, :], jnp.zeros((8 - b - 1, d), F32)], axis=0)
    mod_all = _adaln(cond, ada_w, ada_b).reshape(depth, 8, 6, d)
    mod0, mod1 = mod_all[0, :b], mod_all[1, :b]
    mod0_ctx = mod_all[0, b:b + 1]

    w = _mla_weights(mla_w_in[0], mla_q_norm[0], mla_w_qb[0], mla_kv_norm[0], mla_w_kvb[0],
                     mla_q_gain[0], mla_k_gain[0])
    cos, sin = _rope_tables(n)
    gmix0 = norm_mix[0][None, :]
    qn, qr, kx, vx = _mla_pre(x, mod0, False, gmix0, w, cos, sin, with_q=True, tm=TOKEN_TILE)
    kc, vc = _mla_pre(ctx, mod0_ctx, True, gmix0, w, jnp.ones((nc, LANES), F32),
                      jnp.zeros((nc, LANES), F32), with_q=False, tm=TOKEN_TILE)
    ox = _attention(qn, qr, kc, vc, kx, vx, tq=ATTN_Q_TILE, tk=ATTN_KEY_TILE)
    x1, h0, aff0 = _attn_out(ox, x, mod0, mla_w_out[0].astype(BF16), norm_ffn[0][None, :],
                             _pad_router(router_w[0]), tm=TOKEN_TILE)
    moe0 = _ec_moe(h0, aff0, exp_w_gate, exp_w_up, exp_w_down, 0)

    x3, h1, aff1 = _conv_mixer(x1, moe0, mod0[:, 5:6], mod1, norm_mix[1][None, :],
                               conv_w_in[0].astype(BF16), conv_w[0], conv_w_out[0].astype(BF16),
                               norm_ffn[1][None, :], _pad_router(router_w[1]), tm=TOKEN_TILE)
    moe1 = _ec_moe(h1, aff1, exp_w_gate, exp_w_up, exp_w_down, 1)
    return _residual(x3, moe1, mod1[:, 5:6], tm=TOKEN_TILE)
```

```python
import functools

import jax
import jax.numpy as jnp
from jax import lax
from jax.experimental import pallas as pl
from jax.experimental.pallas import tpu as pltpu

F32 = jnp.float32
BF16 = jnp.bfloat16
HIGHEST = lax.Precision.HIGHEST

GRID_W = 64
N_MIXERS = 2
MLA_HEADS = 8
QK_NOPE = 128
QK_ROPE = 64
QK_DIM = QK_NOPE + QK_ROPE
V_DIM = 128
Q_LORA = 384
KV_LORA = 256
ROPE_FREQS = QK_ROPE // 4
ROPE_THETA = 10000.0
ATTN_SCALE = QK_DIM ** -0.5
LOG2_E = 1.4426950408889634
N_EXPERTS = 16
EC_CAPACITY = 2
EPS = 1e-6

LANES = 128
SUBLANES = 8
F32_INF_BITS = 0x7F800000
F32_VALUE_BITS = 31
VMEM_LIMIT = 56 * 1024 * 1024
MAX_UNROLLED_KEY_BLOCKS = 16
TOKEN_TILE = 512
ATTN_Q_TILE = 512
ATTN_KEY_TILE = 1024


def _cparams(sem):
    return pltpu.CompilerParams(dimension_semantics=sem, vmem_limit_bytes=VMEM_LIMIT)


def _rms(x):
    return x * lax.rsqrt(jnp.mean(x * x, axis=-1, keepdims=True) + EPS)


def _modulate(x, g, shift, scale):
    return (_rms(x) * g) * (1.0 + scale) + shift


def _silu(a):
    return a * jax.nn.sigmoid(a)


def _adaln_kernel(c_ref, w_ref, b_ref, o_ref):
    s = _silu(c_ref[...])
    o_ref[0] = jnp.dot(s, w_ref[0], precision=HIGHEST, preferred_element_type=F32) + b_ref[0]


def _adaln(cond, ada_w, ada_b):
    depth, d, d6 = ada_w.shape
    tn = 1536
    return pl.pallas_call(
        _adaln_kernel,
        grid=(depth, d6 // tn),
        in_specs=[
            pl.BlockSpec((8, d), lambda l, j: (0, 0)),
            pl.BlockSpec((1, d, tn), lambda l, j: (l, 0, j)),
            pl.BlockSpec((1, 1, tn), lambda l, j: (l, 0, j)),
        ],
        out_specs=pl.BlockSpec((1, 8, tn), lambda l, j: (l, 0, j)),
        out_shape=jax.ShapeDtypeStruct((depth, 8, d6), F32),
        compiler_params=_cparams(("parallel", "parallel")),
        name="adaln",
    )(cond, ada_w, ada_b.reshape(depth, 1, d6))


def _mla_pre_kernel(x_ref, mod_ref, gmix_ref, win_ref, qnorm_ref, wqb_ref, kvnorm_ref, wkb_ref, wvt_ref,
                    gq_ref, gk_ref, cos_ref, sin_ref, *out_refs, with_q):
    if with_q:
        qn_ref, qr_ref, k_ref, v_ref = out_refs
    else:
        k_ref, v_ref = out_refs
    mod = mod_ref[0]
    hx = _modulate(x_ref[0], gmix_ref[...], mod[0:1], mod[1:2])
    lat = jnp.dot(hx.astype(BF16), win_ref[...], preferred_element_type=F32)
    cos = cos_ref[...]
    sin = sin_ref[...]
    lo = lax.broadcasted_iota(jnp.int32, (1, LANES), 1) < QK_ROPE

    def half_sums(v):
        v2 = v * v
        return (jnp.sum(jnp.where(lo, v2, 0.0), axis=-1, keepdims=True),
                jnp.sum(jnp.where(lo, 0.0, v2), axis=-1, keepdims=True))

    if with_q:
        gq = gq_ref[...]
        qn_in = (_rms(lat[:, :Q_LORA]) * qnorm_ref[...]).astype(BF16)
        qf = jnp.dot(qn_in, wqb_ref[...], preferred_element_type=F32)
        nope_w = MLA_HEADS * QK_NOPE
        pair_w = (MLA_HEADS // 2) * LANES
        for p in range(MLA_HEADS // 2):
            rp = qf[:, nope_w + p * LANES: nope_w + (p + 1) * LANES]
            sw = qf[:, nope_w + pair_w + p * LANES: nope_w + pair_w + (p + 1) * LANES]
            s_pair = half_sums(rp)
            r_pair = []
            for hh in range(2):
                h = 2 * p + hh
                nope = qf[:, h * QK_NOPE:(h + 1) * QK_NOPE]
                ms = (jnp.sum(nope * nope, axis=-1, keepdims=True) + s_pair[hh]) * (1.0 / QK_DIM)
                r = lax.rsqrt(ms + EPS) * (ATTN_SCALE * LOG2_E)
                r_pair.append(r)
                qn_ref[0, h] = ((nope * r) * gq[0:1]).astype(BF16)
            roped = (rp * gq[1:2]) * cos + (sw * gq[2:3]) * sin
            qr_ref[0, p] = (roped * jnp.where(lo, r_pair[0], r_pair[1])).astype(BF16)

    gk = gk_ref[...]
    kv_lo = Q_LORA
    kvn_in = (_rms(lat[:, kv_lo:kv_lo + KV_LORA]) * kvnorm_ref[...]).astype(BF16)
    kv = jnp.dot(kvn_in, wkb_ref[...], preferred_element_type=F32)
    vt = lax.dot_general(wvt_ref[...], kvn_in, (((1,), (1,)), ((), ())), preferred_element_type=F32)
    pe_lo = Q_LORA + KV_LORA
    kr2 = lat[:, pe_lo:pe_lo + LANES]
    ks2 = lat[:, pe_lo + LANES:pe_lo + 2 * LANES]
    s_pe = half_sums(kr2)[0]
    k_roped = (kr2 * gk[1:2]) * cos + (ks2 * gk[2:3]) * sin
    for h in range(MLA_HEADS):
        nope = kv[:, h * QK_NOPE:(h + 1) * QK_NOPE]
        ms = (jnp.sum(nope * nope, axis=-1, keepdims=True) + s_pe) * (1.0 / QK_DIM)
        r = lax.rsqrt(ms + EPS)
        k_ref[0, h, :, 0:QK_NOPE] = ((nope * r) * gk[0:1]).astype(BF16)
        keep = lo if h % 2 == 0 else jnp.logical_not(lo)
        k_ref[0, h, :, QK_NOPE:QK_NOPE + LANES] = jnp.where(keep, k_roped * r, 0.0).astype(BF16)
        v_ref[0, h, 0:V_DIM, :] = vt[h * V_DIM:(h + 1) * V_DIM, :].astype(BF16)
        v_ref[0, h, V_DIM:2 * V_DIM, :] = jnp.ones((V_DIM, vt.shape[1]), BF16)


def _mla_pre(x, mod, shared_mod, gmix, w, cos, sin, *, with_q, tm):
    b, n, d = x.shape
    tm = min(tm, n)
    const = lambda shape: pl.BlockSpec(shape, lambda bi, i: (0,) * len(shape))
    mod_map = (lambda bi, i: (0, 0, 0)) if shared_mod else (lambda bi, i: (bi, 0, 0))
    in_specs = [
        pl.BlockSpec((1, tm, d), lambda bi, i: (bi, i, 0)),
        pl.BlockSpec((1, 6, d), mod_map),
        const((1, d)),
        const(w["w_in"].shape),
        const((1, Q_LORA)),
        const(w["w_qb"].shape),
        const((1, KV_LORA)),
        const(w["w_kb"].shape),
        const(w["w_vt"].shape),
        const((3, LANES)),
        const((3, LANES)),
        pl.BlockSpec((tm, LANES), lambda bi, i: (i, 0)),
        pl.BlockSpec((tm, LANES), lambda bi, i: (i, 0)),
    ]
    head_spec = lambda nh, w_: pl.BlockSpec((1, nh, tm, w_), lambda bi, i: (bi, 0, i, 0))
    out_specs = [head_spec(MLA_HEADS, 2 * LANES),
                 pl.BlockSpec((1, MLA_HEADS, 2 * V_DIM, tm), lambda bi, i: (bi, 0, 0, i))]
    out_shape = [jax.ShapeDtypeStruct((b, MLA_HEADS, n, 2 * LANES), BF16),
                 jax.ShapeDtypeStruct((b, MLA_HEADS, 2 * V_DIM, n), BF16)]
    if with_q:
        out_specs = [head_spec(MLA_HEADS, QK_NOPE), head_spec(MLA_HEADS // 2, LANES)] + out_specs
        out_shape = [jax.ShapeDtypeStruct((b, MLA_HEADS, n, QK_NOPE), BF16),
                     jax.ShapeDtypeStruct((b, MLA_HEADS // 2, n, LANES), BF16)] + out_shape
    return pl.pallas_call(
        functools.partial(_mla_pre_kernel, with_q=with_q),
        grid=(b, n // tm),
        in_specs=in_specs,
        out_specs=out_specs,
        out_shape=out_shape,
        compiler_params=_cparams(("parallel", "parallel")),
        name="mla_pre_q" if with_q else "mla_pre_ctx",
    )(x, mod, gmix, w["w_in"], w["q_norm"], w["w_qb"], w["kv_norm"], w["w_kb"], w["w_vt"],
      w["gq"], w["gk"], cos, sin)


def _attn_kernel(qn_ref, qr_ref, kc_ref, vc_ref, kx_ref, vx_ref, o_ref,
                 m_s, acc_s, sc_s, pc_s, s0_s, s1_s, p0_s, p1_s, a0_s, a1_s, *, tk):
    q = jnp.concatenate([qn_ref[0, 0], qr_ref[0, 0]], axis=-1)
    nblk = 1 + kx_ref.shape[2] // tk
    s_buf = lambda i: sc_s if i == 0 else (s0_s, s1_s)[i % 2]
    p_buf = lambda i: pc_s if i == 0 else (p0_s, p1_s)[i % 2]
    a_buf = lambda i: (a0_s, a1_s)[i % 2]
    keys = lambda i: kc_ref[0, 0] if i == 0 else kx_ref[0, 0, (i - 1) * tk:i * tk, :]
    vals_t = lambda i: vc_ref[0, 0] if i == 0 else vx_ref[0, 0, :, (i - 1) * tk:i * tk]

    def scores(i):
        s_buf(i)[...] = lax.dot_general(keys(i), q, (((1,), (1,)), ((), ())), preferred_element_type=F32)

    def softmax(i):
        s = s_buf(i)[...]
        m_prev = m_s[...]
        m_new = jnp.maximum(m_prev, jnp.max(s, axis=0, keepdims=True))
        p_buf(i)[...] = jnp.exp2(s - m_new).astype(BF16)
        a_buf(i)[...] = jnp.exp2(m_prev - m_new)
        m_s[...] = m_new

    def accumulate(i):
        acc_s[...] = a_buf(i)[...] * acc_s[...] + jnp.dot(vals_t(i), p_buf(i)[...], preferred_element_type=F32)

    m_s[...] = jnp.full(m_s.shape, -jnp.inf, F32)
    acc_s[...] = jnp.zeros(acc_s.shape, F32)
    scores(0)
    scores(1)
    softmax(0)
    for i in range(nblk):
        if i + 2 < nblk:
            scores(i + 2)
        if i + 1 < nblk:
            softmax(i + 1)
        accumulate(i)
    acc = acc_s[...]
    o_ref[0] = (acc[:V_DIM, :] / acc[V_DIM:V_DIM + 1, :]).T.astype(o_ref.dtype)


def _attention(qn, qr, kc, vc, kx, vx, *, tq, tk):
    b, h, n, _ = qn.shape
    nc = kc.shape[2]
    tq = min(tq, n)
    tk = min(tk, n)
    assert n // tk <= MAX_UNROLLED_KEY_BLOCKS
    k_spec = lambda rows: pl.BlockSpec((1, 1, rows, 2 * LANES), lambda bi, hi, i: (bi, hi, 0, 0))
    vt_spec = lambda cols: pl.BlockSpec((1, 1, 2 * V_DIM, cols), lambda bi, hi, i: (bi, hi, 0, 0))
    return pl.pallas_call(
        functools.partial(_attn_kernel, tk=tk),
        grid=(b, h, n // tq),
        in_specs=[
            pl.BlockSpec((1, 1, tq, QK_NOPE), lambda bi, hi, i: (bi, hi, i, 0)),
            pl.BlockSpec((1, 1, tq, LANES), lambda bi, hi, i: (bi, hi // 2, i, 0)),
            k_spec(nc), vt_spec(nc), k_spec(n), vt_spec(n),
        ],
        out_specs=pl.BlockSpec((1, tq, V_DIM), lambda bi, hi, i: (bi, i, hi)),
        out_shape=jax.ShapeDtypeStruct((b, n, h * V_DIM), BF16),
        scratch_shapes=[
            pltpu.VMEM((1, tq), F32), pltpu.VMEM((2 * V_DIM, tq), F32),
            pltpu.VMEM((nc, tq), F32), pltpu.VMEM((nc, tq), BF16),
            pltpu.VMEM((tk, tq), F32), pltpu.VMEM((tk, tq), F32),
            pltpu.VMEM((tk, tq), BF16), pltpu.VMEM((tk, tq), BF16),
            pltpu.VMEM((1, tq), F32), pltpu.VMEM((1, tq), F32),
        ],
        compiler_params=_cparams(("parallel", "parallel", "arbitrary")),
        name="flash_attn",
    )(qn, qr, kc, vc, kx, vx)


def _ffn_pre(x_new, mod, gffn_ref, rw_ref, h_ref, aff_ref):
    h2 = _modulate(x_new, gffn_ref[...], mod[3:4], mod[4:5])
    hi = h2.astype(BF16)
    lo = (h2 - hi.astype(F32)).astype(BF16)
    h_ref[0] = hi
    rw = rw_ref[...]
    t = jnp.dot(hi, rw, preferred_element_type=F32) + jnp.dot(lo, rw, preferred_element_type=F32)
    logits = t[:, :LANES] + t[:, LANES:]
    is_expert = lax.broadcasted_iota(jnp.int32, (1, LANES), 1) < N_EXPERTS
    logits = jnp.where(is_expert, logits, -jnp.inf)
    e = jnp.exp(logits - jnp.max(logits, axis=-1, keepdims=True))
    aff = e / jnp.sum(e, axis=-1, keepdims=True)
    aff_ref[0] = aff.T[:N_EXPERTS]


def _attn_out_kernel(o_ref, x_ref, mod_ref, wout_ref, gffn_ref, rw_ref, x1_ref, h_ref, aff_ref):
    mod = mod_ref[0]
    o = jnp.dot(o_ref[0], wout_ref[...], preferred_element_type=F32)
    x1 = x_ref[0] + mod[2:3] * o
    x1_ref[0] = x1
    _ffn_pre(x1, mod, gffn_ref, rw_ref, h_ref, aff_ref)


def _attn_out(ox, x, mod, w_out, gffn, rw, *, tm):
    b, n, d = x.shape
    tm = min(tm, n)
    const = lambda shape: pl.BlockSpec(shape, lambda bi, i: (0,) * len(shape))
    tok = lambda: pl.BlockSpec((1, tm, d), lambda bi, i: (bi, i, 0))
    return pl.pallas_call(
        _attn_out_kernel,
        grid=(b, n // tm),
        in_specs=[tok(), tok(), pl.BlockSpec((1, 6, d), lambda bi, i: (bi, 0, 0)),
                  const(w_out.shape), const((1, d)), const(rw.shape)],
        out_specs=[tok(), tok(), pl.BlockSpec((1, N_EXPERTS, tm), lambda bi, i: (bi, 0, i))],
        out_shape=[jax.ShapeDtypeStruct((b, n, d), F32),
                   jax.ShapeDtypeStruct((b, n, d), BF16),
                   jax.ShapeDtypeStruct((b, N_EXPERTS, n), F32)],
        compiler_params=_cparams(("parallel", "parallel")),
        name="attn_out",
    )(ox, x, mod, w_out, gffn, rw)


def _conv_kernel(xm_ref, xp_ref, xn_ref, mm_ref, mp_ref, mn_ref, gprev_ref, mod_ref, gmix_ref,
                 win_ref, cw_ref, wout_ref, gffn_ref, rw_ref, x3_ref, h_ref, aff_ref):
    i = pl.program_id(1)
    tm, d = xm_ref.shape[1], xm_ref.shape[2]
    halo = xp_ref.shape[1]
    gprev = gprev_ref[0]
    mod = mod_ref[0]
    xm = xm_ref[0] + gprev * _from_row_tiles(mm_ref)
    xe = jnp.concatenate([xp_ref[0] + gprev * _from_row_tiles(mp_ref), xm,
                          xn_ref[0] + gprev * _from_row_tiles(mn_ref)], axis=0)
    hx = _modulate(xe, gmix_ref[...], mod[0:1], mod[1:2]).astype(BF16)
    proj = jnp.dot(hx, win_ref[...], preferred_element_type=F32)
    u = proj[:, d:2 * d] * proj[:, 2 * d:3 * d]
    row = lax.broadcasted_iota(jnp.int32, (tm + 2 * halo, 1), 0)
    outside = jnp.logical_or(jnp.logical_and(i == 0, row < halo),
                             jnp.logical_and(i == pl.num_programs(1) - 1, row >= tm + halo))
    u = jnp.where(outside, 0.0, u)
    rows = tm + 2 * halo
    u_prev = pltpu.roll(u, 1, axis=0)[halo:halo + tm]
    u_next = pltpu.roll(u, rows - 1, axis=0)[halo:halo + tm]
    cw = cw_ref[...]
    y = cw[0:1] * u_prev + cw[1:2] * u[halo:halo + tm] + cw[2:3] * u_next
    z = (proj[halo:halo + tm, 0:d] * y).astype(BF16)
    x3 = xm + mod[2:3] * jnp.dot(z, wout_ref[...], preferred_element_type=F32)
    x3_ref[0] = x3
    _ffn_pre(x3, mod, gffn_ref, rw_ref, h_ref, aff_ref)


def _conv_mixer(x, moe, gprev, mod, gmix, w_in, cw, w_out, gffn, rw, *, tm):
    b, n, d = x.shape
    tm = min(tm, n)
    halo = 8
    nb = tm // halo
    last = n // halo - 1
    const = lambda shape: pl.BlockSpec(shape, lambda bi, i: (0,) * len(shape))
    tok = lambda: pl.BlockSpec((1, tm, d), lambda bi, i: (bi, i, 0))
    prev = lambda: pl.BlockSpec((1, halo, d), lambda bi, i: (bi, jnp.maximum(i * nb - 1, 0), 0))
    nxt = lambda: pl.BlockSpec((1, halo, d), lambda bi, i: (bi, jnp.minimum((i + 1) * nb, last), 0))
    rt = (SUBLANES, LANES)
    tok_rt = pl.BlockSpec((1, tm) + rt, lambda bi, i: (bi, i, 0, 0))
    prev_rt = pl.BlockSpec((1, halo) + rt, lambda bi, i: (bi, jnp.maximum(i * nb - 1, 0), 0, 0))
    nxt_rt = pl.BlockSpec((1, halo) + rt, lambda bi, i: (bi, jnp.minimum((i + 1) * nb, last), 0, 0))
    return pl.pallas_call(
        _conv_kernel,
        grid=(b, n // tm),
        in_specs=[tok(), prev(), nxt(), tok_rt, prev_rt, nxt_rt,
                  pl.BlockSpec((1, 1, d), lambda bi, i: (bi, 0, 0)),
                  pl.BlockSpec((1, 6, d), lambda bi, i: (bi, 0, 0)),
                  const((1, d)), const(w_in.shape), const(cw.shape), const(w_out.shape),
                  const((1, d)), const(rw.shape)],
        out_specs=[tok(), tok(), pl.BlockSpec((1, N_EXPERTS, tm), lambda bi, i: (bi, 0, i))],
        out_shape=[jax.ShapeDtypeStruct((b, n, d), F32),
                   jax.ShapeDtypeStruct((b, n, d), BF16),
                   jax.ShapeDtypeStruct((b, N_EXPERTS, n), F32)],
        compiler_params=_cparams(("parallel", "parallel")),
        name="conv_mixer",
    )(x, x, x, moe, moe, moe, gprev, mod, gmix, w_in, cw, w_out, gffn, rw)


def _moe_kernel(xs_ref, g_ref, wg_ref, wu_ref, wd_ref, y_ref, wg_s, wu_s, wd_s, *, tr, tf):
    c = xs_ref.shape[2]
    f_total = wg_s.shape[1]

    @pl.when(pl.program_id(1) == 0)
    def _():
        for src, dst in ((wg_ref, wg_s), (wu_ref, wu_s), (wd_ref, wd_s)):
            for r in range(0, dst.shape[0], tr):
                dst[r:r + tr, :] = src[0, 0, r:r + tr, :].astype(BF16)

    for r0 in range(0, c, tr):
        xs = xs_ref[0, 0, r0:r0 + tr, :]
        acc = jnp.zeros((tr, wd_s.shape[1]), F32)
        for f in range(f_total // tf):
            a = jnp.dot(xs, wg_s[:, f * tf:(f + 1) * tf], preferred_element_type=F32)
            u = jnp.dot(xs, wu_s[:, f * tf:(f + 1) * tf], preferred_element_type=F32)
            hm = (_silu(a) * u).astype(BF16)
            acc = acc + jnp.dot(hm, wd_s[f * tf:(f + 1) * tf, :], preferred_element_type=F32)
        y = acc * g_ref[0, 0, r0:r0 + tr, :]
        for k in range(SUBLANES):
            y_ref[0, 0, r0:r0 + tr, k, :] = y[:, k * LANES:(k + 1) * LANES]


def _moe_ffn(xs, g, wg, wu, wd, layer):
    b, e, c, d = xs.shape
    f = wg.shape[3]
    tr = min(256, c)
    tf = min(512, f)
    w_spec = lambda rows_, cols: pl.BlockSpec((1, 1, rows_, cols), lambda ei, bi: (layer, ei, 0, 0))
    return pl.pallas_call(
        functools.partial(_moe_kernel, tr=tr, tf=tf),
        grid=(e, b),
        in_specs=[
            pl.BlockSpec((1, 1, c, d), lambda ei, bi: (bi, ei, 0, 0)),
            pl.BlockSpec((1, 1, c, 1), lambda ei, bi: (bi, ei, 0, 0)),
            w_spec(d, f), w_spec(d, f), w_spec(f, d),
        ],
        out_specs=pl.BlockSpec((1, 1, c, SUBLANES, LANES), lambda ei, bi: (bi, ei, 0, 0, 0)),
        out_shape=jax.ShapeDtypeStruct((b, e, c, SUBLANES, LANES), F32),
        scratch_shapes=[pltpu.VMEM((d, f), BF16), pltpu.VMEM((d, f), BF16), pltpu.VMEM((f, d), BF16)],
        compiler_params=_cparams(("arbitrary", "arbitrary")),
        name="moe_ffn",
    )(xs, g.reshape(b, e, c, 1), wg, wu, wd)


def _from_row_tiles(ref):
    return jnp.concatenate([ref[0, :, k, :] for k in range(SUBLANES)], axis=-1)


def _combine_kernel(idx_ref, y_ref, out_hbm, acc_s, sem, *, unroll, zero_rows):
    bi = pl.program_id(0)
    ei = pl.program_id(1)
    n = acc_s.shape[0]
    c = y_ref.shape[2]

    @pl.when(ei == 0)
    def _():
        def zero(i, carry):
            acc_s[pl.ds(pl.multiple_of(i * zero_rows, zero_rows), zero_rows)] = jnp.zeros(
                (zero_rows,) + acc_s.shape[1:], F32)
            return carry

        lax.fori_loop(0, n // zero_rows, zero, 0)

    def rows(i, carry):
        base = i * unroll
        tok = [idx_ref[0, 0, base + u] for u in range(unroll)]
        new = [acc_s[tok[u]] + y_ref[0, 0, base + u] for u in range(unroll)]
        for u in range(unroll):
            acc_s[tok[u]] = new[u]
        return carry

    lax.fori_loop(0, c // unroll, rows, 0)

    @pl.when(ei == pl.num_programs(1) - 1)
    def _():
        cp = pltpu.make_async_copy(acc_s, out_hbm.at[bi], sem)
        cp.start()
        cp.wait()


def _moe_combine(idx, y, n):
    b, e, c = idx.shape
    unroll = 8
    zero_rows = min(256, n)
    return pl.pallas_call(
        functools.partial(_combine_kernel, unroll=unroll, zero_rows=zero_rows),
        grid=(b, e),
        in_specs=[
            pl.BlockSpec((1, 1, c), lambda bi, ei: (bi * e + ei, 0, 0), memory_space=pltpu.SMEM),
            pl.BlockSpec((1, 1, c, SUBLANES, LANES), lambda bi, ei: (bi, ei, 0, 0, 0)),
        ],
        out_specs=pl.BlockSpec(memory_space=pl.ANY),
        out_shape=jax.ShapeDtypeStruct((b, n, SUBLANES, LANES), F32),
        scratch_shapes=[pltpu.VMEM((n, SUBLANES, LANES), F32), pltpu.SemaphoreType.DMA],
        compiler_params=_cparams(("arbitrary", "arbitrary")),
        name="moe_combine",
    )(idx.reshape(b * e, 1, c), y)


def _route_kernel(aff_ref, uexcl_ref, uinclt_ref, ones_ref, lstrict_ref, idx_ref, gt_s, eq_s, need_s, *, cap):
    e_n, nch, _ = aff_ref.shape[1:]
    aff = aff_ref[0]

    def count(mask):
        return jnp.sum(jnp.sum(mask.astype(F32), axis=2, keepdims=True), axis=1, keepdims=True)

    def search(_, carry):
        lo, hi = carry
        mid = lo + ((hi - lo + 1) >> 1)
        ok = count(aff >= pltpu.bitcast(mid, F32)) >= cap
        return jnp.where(ok, mid, lo), jnp.where(ok, hi, mid - 1)

    lo0 = jnp.zeros((e_n, 1, 1), jnp.int32)
    hi0 = jnp.full((e_n, 1, 1), F32_INF_BITS, jnp.int32)
    tau_bits, _ = lax.fori_loop(0, F32_VALUE_BITS, search, (lo0, hi0))
    tau = pltpu.bitcast(tau_bits, F32)
    gt = aff > tau
    eq = aff == tau
    gt_s[...] = gt.astype(BF16)
    eq_s[...] = eq.astype(BF16)
    need_s[...] = jnp.broadcast_to(cap - count(gt), need_s.shape)

    n_slot = idx_ref.shape[3]
    slot = lax.broadcasted_iota(jnp.int32, (1, n_slot), 1).astype(F32)
    chunk_id = lax.broadcasted_iota(jnp.int32, (nch, n_slot), 0).astype(F32)
    widen = lambda a: jnp.concatenate([a] * (n_slot // LANES), axis=1)

    def chunk_prefix(mask):
        cnt = jnp.dot(mask, ones_ref[...], preferred_element_type=F32)
        return cnt, jnp.dot(lstrict_ref[...], cnt.astype(BF16), preferred_element_type=F32)

    def per_expert(e, carry):
        eq_e = eq_s[e]
        _, eq_start = chunk_prefix(eq_e)
        eq_rank = eq_start + jnp.dot(eq_e, uexcl_ref[...], preferred_element_type=F32)
        sel = gt_s[e] + jnp.where(eq_rank < need_s[e], eq_e, jnp.zeros_like(eq_e))
        cnt, start = chunk_prefix(sel)
        incl_t = lax.dot_general(uinclt_ref[...], sel, (((1,), (1,)), ((), ())),
                                 preferred_element_type=F32)
        start_w, cnt_w = widen(start), widen(cnt)
        hit = jnp.logical_and(start_w <= slot, slot < start_w + cnt_w)
        hit_f = hit.astype(F32)
        slot_start = jnp.sum(hit_f * start_w, axis=0, keepdims=True)
        slot_chunk = jnp.sum(hit_f * chunk_id, axis=0, keepdims=True)
        incl_of_slot = jnp.dot(incl_t.astype(BF16), hit.astype(BF16), preferred_element_type=F32)
        local = jnp.sum((incl_of_slot <= slot - slot_start).astype(F32), axis=0, keepdims=True)
        idx_ref[0, e] = (slot_chunk * LANES + local).astype(jnp.int32)
        return carry

    lax.fori_loop(0, e_n, per_expert, 0)


def _route(aff, cap):
    b, e, n = aff.shape
    assert n % LANES == 0 and cap % LANES == 0
    nch = n // LANES
    i = jnp.arange(LANES)
    t = jnp.arange(nch)
    uexcl = (i[:, None] < i[None, :]).astype(BF16)
    uinclt = (i[None, :] <= i[:, None]).astype(BF16)
    lstrict = (t[None, :] < t[:, None]).astype(BF16)
    const = lambda shape: pl.BlockSpec(shape, lambda bi: (0,) * len(shape))
    mask_scratch = lambda dt: pltpu.VMEM((e, nch, LANES), dt)
    idx = pl.pallas_call(
        functools.partial(_route_kernel, cap=cap),
        grid=(b,),
        in_specs=[pl.BlockSpec((1, e, nch, LANES), lambda bi: (bi, 0, 0, 0)),
                  const((LANES, LANES)), const((LANES, LANES)), const((LANES, LANES)), const((nch, nch))],
        out_specs=pl.BlockSpec((1, e, 1, cap), lambda bi: (bi, 0, 0, 0)),
        out_shape=jax.ShapeDtypeStruct((b, e, 1, cap), jnp.int32),
        scratch_shapes=[mask_scratch(BF16), mask_scratch(BF16), mask_scratch(F32)],
        compiler_params=_cparams(("parallel",)),
        name="moe_route",
    )(aff.reshape(b, e, nch, LANES), uexcl, uinclt, jnp.ones((LANES, LANES), BF16), lstrict)
    return idx.reshape(b, e, cap)


def _ec_moe(h, aff, wg, wu, wd, layer):
    b, n, d = h.shape
    assert d == SUBLANES * LANES
    cap = EC_CAPACITY * n // N_EXPERTS
    idx = _route(aff, cap)
    g = jnp.take_along_axis(aff, idx, axis=-1)
    xs = jax.vmap(lambda hb, ib: hb[ib])(h, idx)
    y = _moe_ffn(xs, g, wg, wu, wd, layer)
    return _moe_combine(idx, y, n)


def _residual_kernel(x_ref, m_ref, g_ref, o_ref):
    o_ref[0] = x_ref[0] + g_ref[0] * _from_row_tiles(m_ref)


def _residual(x, moe, gate, *, tm):
    b, n, d = x.shape
    tm = min(tm, n)
    tok = lambda: pl.BlockSpec((1, tm, d), lambda bi, i: (bi, i, 0))
    return pl.pallas_call(
        _residual_kernel,
        grid=(b, n // tm),
        in_specs=[tok(), pl.BlockSpec((1, tm, SUBLANES, LANES), lambda bi, i: (bi, i, 0, 0)),
                  pl.BlockSpec((1, 1, d), lambda bi, i: (bi, 0, 0))],
        out_specs=tok(),
        out_shape=jax.ShapeDtypeStruct((b, n, d), F32),
        compiler_params=_cparams(("parallel", "parallel")),
        name="moe_residual",
    )(x, moe, gate)


def _rope_tables(n):
    rows = n // GRID_W
    row = jnp.broadcast_to(jnp.arange(rows, dtype=F32)[:, None], (rows, GRID_W)).reshape(-1)
    col = jnp.broadcast_to(jnp.arange(GRID_W, dtype=F32)[None, :], (rows, GRID_W)).reshape(-1)
    inv = ROPE_THETA ** (-jnp.arange(ROPE_FREQS, dtype=F32) / ROPE_FREQS)
    ar, ac = row[:, None] * inv, col[:, None] * inv
    cos = jnp.concatenate([jnp.cos(ar), jnp.cos(ar), jnp.cos(ac), jnp.cos(ac)], axis=-1)
    sin = jnp.concatenate([-jnp.sin(ar), jnp.sin(ar), -jnp.sin(ac), jnp.sin(ac)], axis=-1)
    return jnp.tile(cos, (1, 2)), jnp.tile(sin, (1, 2))


def _swap_perm():
    f = ROPE_FREQS
    base = jnp.arange(QK_ROPE)
    return jnp.where((base // f) % 2 == 0, base + f, base - f)


def _mla_weights(w_in, q_norm, w_qb, kv_norm, w_kvb, q_gain, k_gain):
    perm = _swap_perm()
    pe = w_in[:, Q_LORA + KV_LORA:]
    pe_sw = pe[:, perm]
    w_in_x = jnp.concatenate([w_in[:, :Q_LORA + KV_LORA], pe, pe, pe_sw, pe_sw], axis=1)
    wq = w_qb.reshape(Q_LORA, MLA_HEADS, QK_DIM)
    wq_rope = wq[:, :, QK_NOPE:]
    w_qb_x = jnp.concatenate([
        wq[:, :, :QK_NOPE].reshape(Q_LORA, -1),
        wq_rope.reshape(Q_LORA, -1),
        wq_rope[:, :, perm].reshape(Q_LORA, -1)], axis=1)
    wkv = w_kvb.reshape(KV_LORA, MLA_HEADS, QK_NOPE + V_DIM)
    w_kb = wkv[:, :, :QK_NOPE].reshape(KV_LORA, -1)
    w_vt = wkv[:, :, QK_NOPE:].reshape(KV_LORA, -1).T

    def gains(g):
        gr = g[QK_NOPE:]
        return jnp.stack([g[:QK_NOPE], jnp.tile(gr, 2), jnp.tile(gr[perm], 2)])

    return {
        "w_in": w_in_x.astype(BF16), "q_norm": q_norm[None, :], "w_qb": w_qb_x.astype(BF16),
        "kv_norm": kv_norm[None, :], "w_kb": w_kb.astype(BF16), "w_vt": w_vt.astype(BF16),
        "gq": gains(q_gain), "gk": gains(k_gain),
    }


def _pad_router(rw):
    hi = rw.astype(BF16)
    lo = (rw - hi.astype(F32)).astype(BF16)
    pad = ((0, 0), (0, LANES - rw.shape[1]))
    return jnp.concatenate([jnp.pad(hi, pad), jnp.pad(lo, pad)], axis=1)


def kernel(x, c, ctx, c_ctx, norm_mix, norm_ffn, ada_w, ada_b, mla_w_in, mla_q_norm, mla_w_qb, mla_kv_norm, mla_w_kvb, mla_q_gain, mla_k_gain, mla_w_out, conv_w_in, conv_w, conv_w_out, router_w, exp_w_gate, exp_w_up, exp_w_down):
    b, n, d = x.shape
    nc = ctx.shape[1]
    depth = ada_w.shape[0]
    assert depth == 2 and b < 8

    cond = jnp.concatenate([c, c_ctx[None, :], jnp.zeros((8 - b - 1, d), F32)], axis=0)
    mod_all = _adaln(cond, ada_w, ada_b).reshape(depth, 8, 6, d)
    mod0, mod1 = mod_all[0, :b], mod_all[1, :b]
    mod0_ctx = mod_all[0, b:b + 1]

    w = _mla_weights(mla_w_in[0], mla_q_norm[0], mla_w_qb[0], mla_kv_norm[0], mla_w_kvb[0],
                     mla_q_gain[0], mla_k_gain[0])
    cos, sin = _rope_tables(n)
    gmix0 = norm_mix[0][None, :]
    qn, qr, kx, vx = _mla_pre(x, mod0, False, gmix0, w, cos, sin, with_q=True, tm=TOKEN_TILE)
    kc, vc = _mla_pre(ctx, mod0_ctx, True, gmix0, w, jnp.ones((nc, LANES), F32),
                      jnp.zeros((nc, LANES), F32), with_q=False, tm=TOKEN_TILE)
    ox = _attention(qn, qr, kc, vc, kx, vx, tq=ATTN_Q_TILE, tk=ATTN_KEY_TILE)
    x1, h0, aff0 = _attn_out(ox, x, mod0, mla_w_out[0].astype(BF16), norm_ffn[0][None, :],
                             _pad_router(router_w[0]), tm=TOKEN_TILE)
    moe0 = _ec_moe(h0, aff0, exp_w_gate, exp_w_up, exp_w_down, 0)

    x3, h1, aff1 = _conv_mixer(x1, moe0, mod0[:, 5:6], mod1, norm_mix[1][None, :],
                               conv_w_in[0].astype(BF16), conv_w[0], conv_w_out[0].astype(BF16),
                               norm_ffn[1][None, :], _pad_router(router_w[1]), tm=TOKEN_TILE)
    moe1 = _ec_moe(h1, aff1, exp_w_gate, exp_w_up, exp_w_down, 1)
    return _residual(x3, moe1, mod1[:, 5:6], tm=TOKEN_TILE)
```

```python
import functools

import jax
import jax.numpy as jnp
from jax import lax
from jax.experimental import pallas as pl
from jax.experimental.pallas import tpu as pltpu

F32 = jnp.float32
BF16 = jnp.bfloat16
HIGHEST = lax.Precision.HIGHEST

GRID_W = 64
N_MIXERS = 2
MLA_HEADS = 8
QK_NOPE = 128
QK_ROPE = 64
QK_DIM = QK_NOPE + QK_ROPE
V_DIM = 128
Q_LORA = 384
KV_LORA = 256
ROPE_FREQS = QK_ROPE // 4
ROPE_THETA = 10000.0
ATTN_SCALE = QK_DIM ** -0.5
LOG2_E = 1.4426950408889634
N_EXPERTS = 16
EC_CAPACITY = 2
EPS = 1e-6

LANES = 128
SUBLANES = 8
F32_INF_BITS = 0x7F800000
F32_VALUE_BITS = 31
VMEM_LIMIT = 56 * 1024 * 1024
MAX_UNROLLED_KEY_BLOCKS = 16
TOKEN_TILE = 512
MOE_ROW_TILE = 512
ATTN_Q_TILE = 1024
ATTN_KEY_TILE = 1024
SOFTMAX_ROWS = 32


def _cparams(sem):
    return pltpu.CompilerParams(dimension_semantics=sem, vmem_limit_bytes=VMEM_LIMIT)


def _rms(x):
    return x * lax.rsqrt(jnp.mean(x * x, axis=-1, keepdims=True) + EPS)


def _modulate(x, g, shift, scale):
    return (_rms(x) * g) * (1.0 + scale) + shift


def _silu(a):
    return a * jax.nn.sigmoid(a)


def _adaln_kernel(c_ref, w_ref, b_ref, o_ref):
    s = _silu(c_ref[...])
    o_ref[0] = jnp.dot(s, w_ref[0], precision=HIGHEST, preferred_element_type=F32) + b_ref[0]


def _adaln(cond, ada_w, ada_b):
    depth, d, d6 = ada_w.shape
    tn = 1536
    return pl.pallas_call(
        _adaln_kernel,
        grid=(depth, d6 // tn),
        in_specs=[
            pl.BlockSpec((8, d), lambda l, j: (0, 0)),
            pl.BlockSpec((1, d, tn), lambda l, j: (l, 0, j)),
            pl.BlockSpec((1, 1, tn), lambda l, j: (l, 0, j)),
        ],
        out_specs=pl.BlockSpec((1, 8, tn), lambda l, j: (l, 0, j)),
        out_shape=jax.ShapeDtypeStruct((depth, 8, d6), F32),
        compiler_params=_cparams(("parallel", "parallel")),
        name="adaln",
    )(cond, ada_w, ada_b.reshape(depth, 1, d6))


def _mla_pre_kernel(x_ref, mod_ref, gmix_ref, win_ref, qnorm_ref, wqb_ref, kvnorm_ref, wkvb_ref,
                    gq_ref, gk_ref, cos_ref, sin_ref, *out_refs, with_q):
    if with_q:
        qn_ref, qr_ref, k_ref, v_ref = out_refs
    else:
        k_ref, v_ref = out_refs
    mod = mod_ref[0]
    hx = _modulate(x_ref[0], gmix_ref[...], mod[0:1], mod[1:2])
    lat = jnp.dot(hx.astype(BF16), win_ref[...], preferred_element_type=F32)
    cos = cos_ref[...]
    sin = sin_ref[...]
    lo = lax.broadcasted_iota(jnp.int32, (1, LANES), 1) < QK_ROPE

    def half_sums(v):
        v2 = v * v
        return (jnp.sum(jnp.where(lo, v2, 0.0), axis=-1, keepdims=True),
                jnp.sum(jnp.where(lo, 0.0, v2), axis=-1, keepdims=True))

    if with_q:
        gq = gq_ref[...]
        qn_in = (_rms(lat[:, :Q_LORA]) * qnorm_ref[...]).astype(BF16)
        qf = jnp.dot(qn_in, wqb_ref[...], preferred_element_type=F32)
        nope_w = MLA_HEADS * QK_NOPE
        pair_w = (MLA_HEADS // 2) * LANES
        for p in range(MLA_HEADS // 2):
            rp = qf[:, nope_w + p * LANES: nope_w + (p + 1) * LANES]
            sw = qf[:, nope_w + pair_w + p * LANES: nope_w + pair_w + (p + 1) * LANES]
            s_pair = half_sums(rp)
            r_pair = []
            for hh in range(2):
                h = 2 * p + hh
                nope = qf[:, h * QK_NOPE:(h + 1) * QK_NOPE]
                ms = (jnp.sum(nope * nope, axis=-1, keepdims=True) + s_pair[hh]) * (1.0 / QK_DIM)
                r = lax.rsqrt(ms + EPS) * (ATTN_SCALE * LOG2_E)
                r_pair.append(r)
                qn_ref[0, h] = ((nope * r) * gq[0:1]).astype(BF16)
            roped = (rp * gq[1:2]) * cos + (sw * gq[2:3]) * sin
            qr_ref[0, p] = (roped * jnp.where(lo, r_pair[0], r_pair[1])).astype(BF16)

    gk = gk_ref[...]
    kv_lo = Q_LORA
    kvn_in = (_rms(lat[:, kv_lo:kv_lo + KV_LORA]) * kvnorm_ref[...]).astype(BF16)
    kv = jnp.dot(kvn_in, wkvb_ref[...], preferred_element_type=F32)
    pe_lo = Q_LORA + KV_LORA
    kr2 = lat[:, pe_lo:pe_lo + LANES]
    ks2 = lat[:, pe_lo + LANES:pe_lo + 2 * LANES]
    s_pe = half_sums(kr2)[0]
    k_roped = (kr2 * gk[1:2]) * cos + (ks2 * gk[2:3]) * sin
    for h in range(MLA_HEADS):
        nope = kv[:, h * QK_NOPE:(h + 1) * QK_NOPE]
        ms = (jnp.sum(nope * nope, axis=-1, keepdims=True) + s_pe) * (1.0 / QK_DIM)
        r = lax.rsqrt(ms + EPS)
        k_ref[0, h, :, 0:QK_NOPE] = ((nope * r) * gk[0:1]).astype(BF16)
        keep = lo if h % 2 == 0 else jnp.logical_not(lo)
        k_ref[0, h, :, QK_NOPE:QK_NOPE + LANES] = jnp.where(keep, k_roped * r, 0.0).astype(BF16)
        v_off = MLA_HEADS * QK_NOPE + h * V_DIM
        v_ref[0, h, :, 0:V_DIM] = kv[:, v_off:v_off + V_DIM].astype(BF16)
        v_ref[0, h, :, V_DIM:2 * V_DIM] = jnp.ones((kv.shape[0], V_DIM), BF16)


def _mla_pre(x, mod, shared_mod, gmix, w, cos, sin, *, with_q, tm):
    b, n, d = x.shape
    tm = min(tm, n)
    const = lambda shape: pl.BlockSpec(shape, lambda bi, i: (0,) * len(shape))
    mod_map = (lambda bi, i: (0, 0, 0)) if shared_mod else (lambda bi, i: (bi, 0, 0))
    in_specs = [
        pl.BlockSpec((1, tm, d), lambda bi, i: (bi, i, 0)),
        pl.BlockSpec((1, 6, d), mod_map),
        const((1, d)),
        const(w["w_in"].shape),
        const((1, Q_LORA)),
        const(w["w_qb"].shape),
        const((1, KV_LORA)),
        const(w["w_kvb"].shape),
        const((3, LANES)),
        const((3, LANES)),
        pl.BlockSpec((tm, LANES), lambda bi, i: (i, 0)),
        pl.BlockSpec((tm, LANES), lambda bi, i: (i, 0)),
    ]
    head_spec = lambda nh, w_: pl.BlockSpec((1, nh, tm, w_), lambda bi, i: (bi, 0, i, 0))
    out_specs = [head_spec(MLA_HEADS, 2 * LANES), head_spec(MLA_HEADS, 2 * V_DIM)]
    out_shape = [jax.ShapeDtypeStruct((b, MLA_HEADS, n, 2 * LANES), BF16),
                 jax.ShapeDtypeStruct((b, MLA_HEADS, n, 2 * V_DIM), BF16)]
    if with_q:
        out_specs = [head_spec(MLA_HEADS, QK_NOPE), head_spec(MLA_HEADS // 2, LANES)] + out_specs
        out_shape = [jax.ShapeDtypeStruct((b, MLA_HEADS, n, QK_NOPE), BF16),
                     jax.ShapeDtypeStruct((b, MLA_HEADS // 2, n, LANES), BF16)] + out_shape
    return pl.pallas_call(
        functools.partial(_mla_pre_kernel, with_q=with_q),
        grid=(b, n // tm),
        in_specs=in_specs,
        out_specs=out_specs,
        out_shape=out_shape,
        compiler_params=_cparams(("parallel", "parallel")),
        name="mla_pre_q" if with_q else "mla_pre_ctx",
    )(x, mod, gmix, w["w_in"], w["q_norm"], w["w_qb"], w["kv_norm"], w["w_kvb"],
      w["gq"], w["gk"], cos, sin)


def _attn_kernel(qn_ref, qr_ref, kc_ref, vc_ref, kx_ref, vx_ref, o_ref,
                 m_s, acc_s, sc_s, pc_s, s0_s, s1_s, p0_s, p1_s, a0_s, a1_s, *, tk):
    q = jnp.concatenate([qn_ref[0, 0], qr_ref[0, 0]], axis=-1)
    nblk = 1 + kx_ref.shape[2] // tk
    s_buf = lambda i: sc_s if i == 0 else (s0_s, s1_s)[i % 2]
    p_buf = lambda i: pc_s if i == 0 else (p0_s, p1_s)[i % 2]
    a_buf = lambda i: (a0_s, a1_s)[i % 2]
    keys = lambda i: kc_ref[0, 0] if i == 0 else kx_ref[0, 0, (i - 1) * tk:i * tk, :]
    vals = lambda i: vc_ref[0, 0] if i == 0 else vx_ref[0, 0, (i - 1) * tk:i * tk, :]

    def scores(i):
        s_buf(i)[...] = lax.dot_general(q, keys(i), (((1,), (1,)), ((), ())), preferred_element_type=F32)

    def softmax(i):
        for r in range(0, q.shape[0], SOFTMAX_ROWS):
            rows = slice(r, r + SOFTMAX_ROWS)
            s = s_buf(i)[rows, :]
            m_prev = m_s[rows, :]
            m_new = jnp.maximum(m_prev, jnp.max(s, axis=-1, keepdims=True))
            p_buf(i)[rows, :] = jnp.exp2(s - m_new).astype(BF16)
            a_buf(i)[rows, :] = jnp.exp2(m_prev - m_new)
            m_s[rows, :] = m_new

    def accumulate(i):
        acc_s[...] = a_buf(i)[...] * acc_s[...] + jnp.dot(p_buf(i)[...], vals(i), preferred_element_type=F32)

    m_s[...] = jnp.full(m_s.shape, -jnp.inf, F32)
    acc_s[...] = jnp.zeros(acc_s.shape, F32)
    scores(0)
    scores(1)
    softmax(0)
    for i in range(nblk):
        if i + 2 < nblk:
            scores(i + 2)
        if i + 1 < nblk:
            softmax(i + 1)
        accumulate(i)
    acc = acc_s[...]
    o_ref[0] = (acc[:, :V_DIM] / acc[:, V_DIM:]).astype(o_ref.dtype)


def _attention(qn, qr, kc, vc, kx, vx, *, tq, tk):
    b, h, n, _ = qn.shape
    nc = kc.shape[2]
    tq = min(tq, n)
    tk = min(tk, n)
    assert n // tk <= MAX_UNROLLED_KEY_BLOCKS
    whole = lambda rows: pl.BlockSpec((1, 1, rows, 2 * LANES), lambda bi, hi, i: (bi, hi, 0, 0))
    return pl.pallas_call(
        functools.partial(_attn_kernel, tk=tk),
        grid=(b, h, n // tq),
        in_specs=[
            pl.BlockSpec((1, 1, tq, QK_NOPE), lambda bi, hi, i: (bi, hi, i, 0)),
            pl.BlockSpec((1, 1, tq, LANES), lambda bi, hi, i: (bi, hi // 2, i, 0)),
            whole(nc), whole(nc), whole(n), whole(n),
        ],
        out_specs=pl.BlockSpec((1, tq, V_DIM), lambda bi, hi, i: (bi, i, hi)),
        out_shape=jax.ShapeDtypeStruct((b, n, h * V_DIM), BF16),
        scratch_shapes=[
            pltpu.VMEM((tq, 1), F32), pltpu.VMEM((tq, 2 * V_DIM), F32),
            pltpu.VMEM((tq, nc), F32), pltpu.VMEM((tq, nc), BF16),
            pltpu.VMEM((tq, tk), F32), pltpu.VMEM((tq, tk), F32),
            pltpu.VMEM((tq, tk), BF16), pltpu.VMEM((tq, tk), BF16),
            pltpu.VMEM((tq, 1), F32), pltpu.VMEM((tq, 1), F32),
        ],
        compiler_params=_cparams(("parallel", "parallel", "arbitrary")),
        name="flash_attn",
    )(qn, qr, kc, vc, kx, vx)


def _ffn_pre(x_new, mod, gffn_ref, rw_ref, h_ref, aff_ref):
    h2 = _modulate(x_new, gffn_ref[...], mod[3:4], mod[4:5])
    hi = h2.astype(BF16)
    lo = (h2 - hi.astype(F32)).astype(BF16)
    h_ref[0] = hi
    rw = rw_ref[...]
    t = jnp.dot(hi, rw, preferred_element_type=F32) + jnp.dot(lo, rw, preferred_element_type=F32)
    logits = t[:, :LANES] + t[:, LANES:]
    is_expert = lax.broadcasted_iota(jnp.int32, (1, LANES), 1) < N_EXPERTS
    logits = jnp.where(is_expert, logits, -jnp.inf)
    e = jnp.exp(logits - jnp.max(logits, axis=-1, keepdims=True))
    aff = e / jnp.sum(e, axis=-1, keepdims=True)
    aff_ref[0] = aff.T[:N_EXPERTS]


def _attn_out_kernel(o_ref, x_ref, mod_ref, wout_ref, gffn_ref, rw_ref, x1_ref, h_ref, aff_ref):
    mod = mod_ref[0]
    o = jnp.dot(o_ref[0], wout_ref[...], preferred_element_type=F32)
    x1 = x_ref[0] + mod[2:3] * o
    x1_ref[0] = x1
    _ffn_pre(x1, mod, gffn_ref, rw_ref, h_ref, aff_ref)


def _attn_out(ox, x, mod, w_out, gffn, rw, *, tm):
    b, n, d = x.shape
    tm = min(tm, n)
    const = lambda shape: pl.BlockSpec(shape, lambda bi, i: (0,) * len(shape))
    tok = lambda: pl.BlockSpec((1, tm, d), lambda bi, i: (bi, i, 0))
    return pl.pallas_call(
        _attn_out_kernel,
        grid=(b, n // tm),
        in_specs=[tok(), tok(), pl.BlockSpec((1, 6, d), lambda bi, i: (bi, 0, 0)),
                  const(w_out.shape), const((1, d)), const(rw.shape)],
        out_specs=[tok(), tok(), pl.BlockSpec((1, N_EXPERTS, tm), lambda bi, i: (bi, 0, i))],
        out_shape=[jax.ShapeDtypeStruct((b, n, d), F32),
                   jax.ShapeDtypeStruct((b, n, d), BF16),
                   jax.ShapeDtypeStruct((b, N_EXPERTS, n), F32)],
        compiler_params=_cparams(("parallel", "parallel")),
        name="attn_out",
    )(ox, x, mod, w_out, gffn, rw)


def _conv_kernel(xm_ref, xp_ref, xn_ref, mm_ref, mp_ref, mn_ref, gprev_ref, mod_ref, gmix_ref,
                 win_ref, cw_ref, wout_ref, gffn_ref, rw_ref, x3_ref, h_ref, aff_ref):
    i = pl.program_id(1)
    tm, d = xm_ref.shape[1], xm_ref.shape[2]
    halo = xp_ref.shape[1]
    gprev = gprev_ref[0]
    mod = mod_ref[0]
    xm = xm_ref[0] + gprev * _from_row_tiles(mm_ref)
    xe = jnp.concatenate([xp_ref[0] + gprev * _from_row_tiles(mp_ref), xm,
                          xn_ref[0] + gprev * _from_row_tiles(mn_ref)], axis=0)
    hx = _modulate(xe, gmix_ref[...], mod[0:1], mod[1:2]).astype(BF16)
    proj = jnp.dot(hx, win_ref[...], preferred_element_type=F32)
    u = proj[:, d:2 * d] * proj[:, 2 * d:3 * d]
    row = lax.broadcasted_iota(jnp.int32, (tm + 2 * halo, 1), 0)
    outside = jnp.logical_or(jnp.logical_and(i == 0, row < halo),
                             jnp.logical_and(i == pl.num_programs(1) - 1, row >= tm + halo))
    u = jnp.where(outside, 0.0, u)
    rows = tm + 2 * halo
    u_prev = pltpu.roll(u, 1, axis=0)[halo:halo + tm]
    u_next = pltpu.roll(u, rows - 1, axis=0)[halo:halo + tm]
    cw = cw_ref[...]
    y = cw[0:1] * u_prev + cw[1:2] * u[halo:halo + tm] + cw[2:3] * u_next
    z = (proj[halo:halo + tm, 0:d] * y).astype(BF16)
    x3 = xm + mod[2:3] * jnp.dot(z, wout_ref[...], preferred_element_type=F32)
    x3_ref[0] = x3
    _ffn_pre(x3, mod, gffn_ref, rw_ref, h_ref, aff_ref)


def _conv_mixer(x, moe, gprev, mod, gmix, w_in, cw, w_out, gffn, rw, *, tm):
    b, n, d = x.shape
    tm = min(tm, n)
    halo = 8
    nb = tm // halo
    last = n // halo - 1
    const = lambda shape: pl.BlockSpec(shape, lambda bi, i: (0,) * len(shape))
    tok = lambda: pl.BlockSpec((1, tm, d), lambda bi, i: (bi, i, 0))
    prev = lambda: pl.BlockSpec((1, halo, d), lambda bi, i: (bi, jnp.maximum(i * nb - 1, 0), 0))
    nxt = lambda: pl.BlockSpec((1, halo, d), lambda bi, i: (bi, jnp.minimum((i + 1) * nb, last), 0))
    rt = (SUBLANES, LANES)
    tok_rt = pl.BlockSpec((1, tm) + rt, lambda bi, i: (bi, i, 0, 0))
    prev_rt = pl.BlockSpec((1, halo) + rt, lambda bi, i: (bi, jnp.maximum(i * nb - 1, 0), 0, 0))
    nxt_rt = pl.BlockSpec((1, halo) + rt, lambda bi, i: (bi, jnp.minimum((i + 1) * nb, last), 0, 0))
    return pl.pallas_call(
        _conv_kernel,
        grid=(b, n // tm),
        in_specs=[tok(), prev(), nxt(), tok_rt, prev_rt, nxt_rt,
                  pl.BlockSpec((1, 1, d), lambda bi, i: (bi, 0, 0)),
                  pl.BlockSpec((1, 6, d), lambda bi, i: (bi, 0, 0)),
                  const((1, d)), const(w_in.shape), const(cw.shape), const(w_out.shape),
                  const((1, d)), const(rw.shape)],
        out_specs=[tok(), tok(), pl.BlockSpec((1, N_EXPERTS, tm), lambda bi, i: (bi, 0, i))],
        out_shape=[jax.ShapeDtypeStruct((b, n, d), F32),
                   jax.ShapeDtypeStruct((b, n, d), BF16),
                   jax.ShapeDtypeStruct((b, N_EXPERTS, n), F32)],
        compiler_params=_cparams(("parallel", "parallel")),
        name="conv_mixer",
    )(x, x, x, moe, moe, moe, gprev, mod, gmix, w_in, cw, w_out, gffn, rw)


def _moe_kernel(xs_ref, g_ref, wg_ref, wu_ref, wd_ref, y_ref, wg_s, wu_s, wd_s, *, tr, tf):
    c = xs_ref.shape[2]
    f_total = wg_s.shape[1]

    @pl.when(pl.program_id(1) == 0)
    def _():
        for src, dst in ((wg_ref, wg_s), (wu_ref, wu_s), (wd_ref, wd_s)):
            for r in range(0, dst.shape[0], tr):
                dst[r:r + tr, :] = src[0, 0, r:r + tr, :].astype(BF16)

    for r0 in range(0, c, tr):
        xs = xs_ref[0, 0, r0:r0 + tr, :]
        acc = jnp.zeros((tr, wd_s.shape[1]), F32)
        for f in range(f_total // tf):
            a = jnp.dot(xs, wg_s[:, f * tf:(f + 1) * tf], preferred_element_type=F32)
            u = jnp.dot(xs, wu_s[:, f * tf:(f + 1) * tf], preferred_element_type=F32)
            hm = (_silu(a) * u).astype(BF16)
            acc = acc + jnp.dot(hm, wd_s[f * tf:(f + 1) * tf, :], preferred_element_type=F32)
        y = acc * g_ref[0, 0, r0:r0 + tr, :]
        for k in range(SUBLANES):
            y_ref[0, 0, r0:r0 + tr, k, :] = y[:, k * LANES:(k + 1) * LANES]


def _moe_ffn(xs, g, wg, wu, wd, layer):
    b, e, c, d = xs.shape
    f = wg.shape[3]
    tr = min(MOE_ROW_TILE, c)
    tf = min(512, f)
    w_spec = lambda rows_, cols: pl.BlockSpec((1, 1, rows_, cols), lambda ei, bi: (layer, ei, 0, 0))
    return pl.pallas_call(
        functools.partial(_moe_kernel, tr=tr, tf=tf),
        grid=(e, b),
        in_specs=[
            pl.BlockSpec((1, 1, c, d), lambda ei, bi: (bi, ei, 0, 0)),
            pl.BlockSpec((1, 1, c, 1), lambda ei, bi: (bi, ei, 0, 0)),
            w_spec(d, f), w_spec(d, f), w_spec(f, d),
        ],
        out_specs=pl.BlockSpec((1, 1, c, SUBLANES, LANES), lambda ei, bi: (bi, ei, 0, 0, 0)),
        out_shape=jax.ShapeDtypeStruct((b, e, c, SUBLANES, LANES), F32),
        scratch_shapes=[pltpu.VMEM((d, f), BF16), pltpu.VMEM((d, f), BF16), pltpu.VMEM((f, d), BF16)],
        compiler_params=_cparams(("arbitrary", "arbitrary")),
        name="moe_ffn",
    )(xs, g.reshape(b, e, c, 1), wg, wu, wd)


def _from_row_tiles(ref):
    return jnp.concatenate([ref[0, :, k, :] for k in range(SUBLANES)], axis=-1)


def _combine_kernel(idx_ref, y_ref, out_hbm, acc_s, sem, *, unroll, zero_rows):
    bi = pl.program_id(0)
    ei = pl.program_id(1)
    n = acc_s.shape[0]
    c = y_ref.shape[2]

    @pl.when(ei == 0)
    def _():
        def zero(i, carry):
            acc_s[pl.ds(pl.multiple_of(i * zero_rows, zero_rows), zero_rows)] = jnp.zeros(
                (zero_rows,) + acc_s.shape[1:], F32)
            return carry

        lax.fori_loop(0, n // zero_rows, zero, 0)

    def rows(i, carry):
        base = i * unroll
        tok = [idx_ref[0, 0, base + u] for u in range(unroll)]
        new = [acc_s[tok[u]] + y_ref[0, 0, base + u] for u in range(unroll)]
        for u in range(unroll):
            acc_s[tok[u]] = new[u]
        return carry

    lax.fori_loop(0, c // unroll, rows, 0)

    @pl.when(ei == pl.num_programs(1) - 1)
    def _():
        cp = pltpu.make_async_copy(acc_s, out_hbm.at[bi], sem)
        cp.start()
        cp.wait()


def _moe_combine(idx, y, n):
    b, e, c = idx.shape
    unroll = 8
    zero_rows = min(256, n)
    return pl.pallas_call(
        functools.partial(_combine_kernel, unroll=unroll, zero_rows=zero_rows),
        grid=(b, e),
        in_specs=[
            pl.BlockSpec((1, 1, c), lambda bi, ei: (bi * e + ei, 0, 0), memory_space=pltpu.SMEM),
            pl.BlockSpec((1, 1, c, SUBLANES, LANES), lambda bi, ei: (bi, ei, 0, 0, 0)),
        ],
        out_specs=pl.BlockSpec(memory_space=pl.ANY),
        out_shape=jax.ShapeDtypeStruct((b, n, SUBLANES, LANES), F32),
        scratch_shapes=[pltpu.VMEM((n, SUBLANES, LANES), F32), pltpu.SemaphoreType.DMA],
        compiler_params=_cparams(("arbitrary", "arbitrary")),
        name="moe_combine",
    )(idx.reshape(b * e, 1, c), y)


def _route_kernel(aff_ref, uexcl_ref, uinclt_ref, ones_ref, lstrict_ref, idx_ref, gt_s, eq_s, need_s, *, cap):
    e_n, nch, _ = aff_ref.shape[1:]
    aff = aff_ref[0]

    def count(mask):
        return jnp.sum(jnp.sum(mask.astype(F32), axis=2, keepdims=True), axis=1, keepdims=True)

    def search(_, carry):
        lo, hi = carry
        mid = lo + ((hi - lo + 1) >> 1)
        ok = count(aff >= pltpu.bitcast(mid, F32)) >= cap
        return jnp.where(ok, mid, lo), jnp.where(ok, hi, mid - 1)

    lo0 = jnp.zeros((e_n, 1, 1), jnp.int32)
    hi0 = jnp.full((e_n, 1, 1), F32_INF_BITS, jnp.int32)
    tau_bits, _ = lax.fori_loop(0, F32_VALUE_BITS, search, (lo0, hi0))
    tau = pltpu.bitcast(tau_bits, F32)
    gt = aff > tau
    eq = aff == tau
    gt_s[...] = gt.astype(BF16)
    eq_s[...] = eq.astype(BF16)
    need_s[...] = jnp.broadcast_to(cap - count(gt), need_s.shape)

    n_slot = idx_ref.shape[3]
    slot = lax.broadcasted_iota(jnp.int32, (1, n_slot), 1).astype(F32)
    chunk_id = lax.broadcasted_iota(jnp.int32, (nch, n_slot), 0).astype(F32)
    widen = lambda a: jnp.concatenate([a] * (n_slot // LANES), axis=1)

    def chunk_prefix(mask):
        cnt = jnp.dot(mask, ones_ref[...], preferred_element_type=F32)
        return cnt, jnp.dot(lstrict_ref[...], cnt.astype(BF16), preferred_element_type=F32)

    def per_expert(e, carry):
        eq_e = eq_s[e]
        _, eq_start = chunk_prefix(eq_e)
        eq_rank = eq_start + jnp.dot(eq_e, uexcl_ref[...], preferred_element_type=F32)
        sel = gt_s[e] + jnp.where(eq_rank < need_s[e], eq_e, jnp.zeros_like(eq_e))
        cnt, start = chunk_prefix(sel)
        incl_t = lax.dot_general(uinclt_ref[...], sel, (((1,), (1,)), ((), ())),
                                 preferred_element_type=F32)
        start_w, cnt_w = widen(start), widen(cnt)
        hit = jnp.logical_and(start_w <= slot, slot < start_w + cnt_w)
        hit_f = hit.astype(F32)
        slot_start = jnp.sum(hit_f * start_w, axis=0, keepdims=True)
        slot_chunk = jnp.sum(hit_f * chunk_id, axis=0, keepdims=True)
        incl_of_slot = jnp.dot(incl_t.astype(BF16), hit.astype(BF16), preferred_element_type=F32)
        local = jnp.sum((incl_of_slot <= slot - slot_start).astype(F32), axis=0, keepdims=True)
        idx_ref[0, e] = (slot_chunk * LANES + local).astype(jnp.int32)
        return carry

    lax.fori_loop(0, e_n, per_expert, 0)


def _route(aff, cap):
    b, e, n = aff.shape
    assert n % LANES == 0 and cap % LANES == 0
    nch = n // LANES
    i = jnp.arange(LANES)
    t = jnp.arange(nch)
    uexcl = (i[:, None] < i[None, :]).astype(BF16)
    uinclt = (i[None, :] <= i[:, None]).astype(BF16)
    lstrict = (t[None, :] < t[:, None]).astype(BF16)
    const = lambda shape: pl.BlockSpec(shape, lambda bi: (0,) * len(shape))
    mask_scratch = lambda dt: pltpu.VMEM((e, nch, LANES), dt)
    idx = pl.pallas_call(
        functools.partial(_route_kernel, cap=cap),
        grid=(b,),
        in_specs=[pl.BlockSpec((1, e, nch, LANES), lambda bi: (bi, 0, 0, 0)),
                  const((LANES, LANES)), const((LANES, LANES)), const((LANES, LANES)), const((nch, nch))],
        out_specs=pl.BlockSpec((1, e, 1, cap), lambda bi: (bi, 0, 0, 0)),
        out_shape=jax.ShapeDtypeStruct((b, e, 1, cap), jnp.int32),
        scratch_shapes=[mask_scratch(BF16), mask_scratch(BF16), mask_scratch(F32)],
        compiler_params=_cparams(("parallel",)),
        name="moe_route",
    )(aff.reshape(b, e, nch, LANES), uexcl, uinclt, jnp.ones((LANES, LANES), BF16), lstrict)
    return idx.reshape(b, e, cap)


def _ec_moe(h, aff, wg, wu, wd, layer):
    b, n, d = h.shape
    assert d == SUBLANES * LANES
    cap = EC_CAPACITY * n // N_EXPERTS
    idx = _route(aff, cap)
    g = jnp.take_along_axis(aff, idx, axis=-1)
    xs = jax.vmap(lambda hb, ib: hb[ib])(h, idx)
    y = _moe_ffn(xs, g, wg, wu, wd, layer)
    return _moe_combine(idx, y, n)


def _residual_kernel(x_ref, m_ref, g_ref, o_ref):
    o_ref[0] = x_ref[0] + g_ref[0] * _from_row_tiles(m_ref)


def _residual(x, moe, gate, *, tm):
    b, n, d = x.shape
    tm = min(tm, n)
    tok = lambda: pl.BlockSpec((1, tm, d), lambda bi, i: (bi, i, 0))
    return pl.pallas_call(
        _residual_kernel,
        grid=(b, n // tm),
        in_specs=[tok(), pl.BlockSpec((1, tm, SUBLANES, LANES), lambda bi, i: (bi, i, 0, 0)),
                  pl.BlockSpec((1, 1, d), lambda bi, i: (bi, 0, 0))],
        out_specs=tok(),
        out_shape=jax.ShapeDtypeStruct((b, n, d), F32),
        compiler_params=_cparams(("parallel", "parallel")),
        name="moe_residual",
    )(x, moe, gate)


def _rope_tables(n):
    rows = n // GRID_W
    row = jnp.broadcast_to(jnp.arange(rows, dtype=F32)[:, None], (rows, GRID_W)).reshape(-1)
    col = jnp.broadcast_to(jnp.arange(GRID_W, dtype=F32)[None, :], (rows, GRID_W)).reshape(-1)
    inv = ROPE_THETA ** (-jnp.arange(ROPE_FREQS, dtype=F32) / ROPE_FREQS)
    ar, ac = row[:, None] * inv, col[:, None] * inv
    cos = jnp.concatenate([jnp.cos(ar), jnp.cos(ar), jnp.cos(ac), jnp.cos(ac)], axis=-1)
    sin = jnp.concatenate([-jnp.sin(ar), jnp.sin(ar), -jnp.sin(ac), jnp.sin(ac)], axis=-1)
    return jnp.tile(cos, (1, 2)), jnp.tile(sin, (1, 2))


def _swap_perm():
    f = ROPE_FREQS
    base = jnp.arange(QK_ROPE)
    return jnp.where((base // f) % 2 == 0, base + f, base - f)


def _mla_weights(w_in, q_norm, w_qb, kv_norm, w_kvb, q_gain, k_gain):
    perm = _swap_perm()
    pe = w_in[:, Q_LORA + KV_LORA:]
    pe_sw = pe[:, perm]
    w_in_x = jnp.concatenate([w_in[:, :Q_LORA + KV_LORA], pe, pe, pe_sw, pe_sw], axis=1)
    wq = w_qb.reshape(Q_LORA, MLA_HEADS, QK_DIM)
    wq_rope = wq[:, :, QK_NOPE:]
    w_qb_x = jnp.concatenate([
        wq[:, :, :QK_NOPE].reshape(Q_LORA, -1),
        wq_rope.reshape(Q_LORA, -1),
        wq_rope[:, :, perm].reshape(Q_LORA, -1)], axis=1)
    wkv = w_kvb.reshape(KV_LORA, MLA_HEADS, QK_NOPE + V_DIM)
    w_kvb_x = jnp.concatenate([wkv[:, :, :QK_NOPE].reshape(KV_LORA, -1),
                               wkv[:, :, QK_NOPE:].reshape(KV_LORA, -1)], axis=1)

    def gains(g):
        gr = g[QK_NOPE:]
        return jnp.stack([g[:QK_NOPE], jnp.tile(gr, 2), jnp.tile(gr[perm], 2)])

    return {
        "w_in": w_in_x.astype(BF16), "q_norm": q_norm[None, :], "w_qb": w_qb_x.astype(BF16),
        "kv_norm": kv_norm[None, :], "w_kvb": w_kvb_x.astype(BF16),
        "gq": gains(q_gain), "gk": gains(k_gain),
    }


def _pad_router(rw):
    hi = rw.astype(BF16)
    lo = (rw - hi.astype(F32)).astype(BF16)
    pad = ((0, 0), (0, LANES - rw.shape[1]))
    return jnp.concatenate([jnp.pad(hi, pad), jnp.pad(lo, pad)], axis=1)


def kernel(x, c, ctx, c_ctx, norm_mix, norm_ffn, ada_w, ada_b, mla_w_in, mla_q_norm, mla_w_qb, mla_kv_norm, mla_w_kvb, mla_q_gain, mla_k_gain, mla_w_out, conv_w_in, conv_w, conv_w_out, router_w, exp_w_gate, exp_w_up, exp_w_down):
    b, n, d = x.shape
    nc = ctx.shape[1]
    depth = ada_w.shape[0]
    assert depth == 2 and b < 8

    cond = jnp.concatenate([c, c_ctx[None, :], jnp.zeros((8 - b - 1, d), F32)], axis=0)
    mod_all = _adaln(cond, ada_w, ada_b).reshape(depth, 8, 6, d)
    mod0, mod1 = mod_all[0, :b], mod_all[1, :b]
    mod0_ctx = mod_all[0, b:b + 1]

    w = _mla_weights(mla_w_in[0], mla_q_norm[0], mla_w_qb[0], mla_kv_norm[0], mla_w_kvb[0],
                     mla_q_gain[0], mla_k_gain[0])
    cos, sin = _rope_tables(n)
    gmix0 = norm_mix[0][None, :]
    qn, qr, kx, vx = _mla_pre(x, mod0, False, gmix0, w, cos, sin, with_q=True, tm=TOKEN_TILE)
    kc, vc = _mla_pre(ctx, mod0_ctx, True, gmix0, w, jnp.ones((nc, LANES), F32),
                      jnp.zeros((nc, LANES), F32), with_q=False, tm=TOKEN_TILE)
    ox = _attention(qn, qr, kc, vc, kx, vx, tq=ATTN_Q_TILE, tk=ATTN_KEY_TILE)
    x1, h0, aff0 = _attn_out(ox, x, mod0, mla_w_out[0].astype(BF16), norm_ffn[0][None, :],
                             _pad_router(router_w[0]), tm=TOKEN_TILE)
    moe0 = _ec_moe(h0, aff0, exp_w_gate, exp_w_up, exp_w_down, 0)

    x3, h1, aff1 = _conv_mixer(x1, moe0, mod0[:, 5:6], mod1, norm_mix[1][None, :],
                               conv_w_in[0].astype(BF16), conv_w[0], conv_w_out[0].astype(BF16),
                               norm_ffn[1][None, :], _pad_router(router_w[1]), tm=TOKEN_TILE)
    moe1 = _ec_moe(h1, aff1, exp_w_gate, exp_w_up, exp_w_down, 1)
    return _residual(x3, moe1, mod1[:, 5:6], tm=TOKEN_TILE)
```

```python
import functools

import jax
import jax.numpy as jnp
from jax import lax
from jax.experimental import pallas as pl
from jax.experimental.pallas import tpu as pltpu

F32 = jnp.float32
BF16 = jnp.bfloat16
HIGHEST = lax.Precision.HIGHEST

GRID_W = 64
N_MIXERS = 2
MLA_HEADS = 8
QK_NOPE = 128
QK_ROPE = 64
QK_DIM = QK_NOPE + QK_ROPE
V_DIM = 128
Q_LORA = 384
KV_LORA = 256
ROPE_FREQS = QK_ROPE // 4
ROPE_THETA = 10000.0
ATTN_SCALE = QK_DIM ** -0.5
LOG2_E = 1.4426950408889634
N_EXPERTS = 16
EC_CAPACITY = 2
EPS = 1e-6

LANES = 128
SUBLANES = 8
F32_INF_BITS = 0x7F800000
F32_VALUE_BITS = 31
VMEM_LIMIT = 56 * 1024 * 1024
MAX_UNROLLED_KEY_BLOCKS = 16
TOKEN_TILE = 1024
PROJ_TOKEN_TILE = 1024
MOE_ROW_TILE = 512
ATTN_Q_TILE = 1024
ATTN_KEY_TILE = 1024
SOFTMAX_ROWS = 32


def _cparams(sem):
    return pltpu.CompilerParams(dimension_semantics=sem, vmem_limit_bytes=VMEM_LIMIT)


def _rms(x):
    return x * lax.rsqrt(jnp.mean(x * x, axis=-1, keepdims=True) + EPS)


def _modulate(x, g, shift, scale):
    return (_rms(x) * g) * (1.0 + scale) + shift


def _silu(a):
    return a * jax.nn.sigmoid(a)


def _adaln_kernel(c_ref, w_ref, b_ref, o_ref):
    s = _silu(c_ref[...])
    o_ref[0] = jnp.dot(s, w_ref[0], precision=HIGHEST, preferred_element_type=F32) + b_ref[0]


def _adaln(cond, ada_w, ada_b):
    depth, d, d6 = ada_w.shape
    tn = 1536
    return pl.pallas_call(
        _adaln_kernel,
        grid=(depth, d6 // tn),
        in_specs=[
            pl.BlockSpec((8, d), lambda l, j: (0, 0)),
            pl.BlockSpec((1, d, tn), lambda l, j: (l, 0, j)),
            pl.BlockSpec((1, 1, tn), lambda l, j: (l, 0, j)),
        ],
        out_specs=pl.BlockSpec((1, 8, tn), lambda l, j: (l, 0, j)),
        out_shape=jax.ShapeDtypeStruct((depth, 8, d6), F32),
        compiler_params=_cparams(("parallel", "parallel")),
        name="adaln",
    )(cond, ada_w, ada_b.reshape(depth, 1, d6))


def _mla_pre_kernel(x_ref, mod_ref, gmix_ref, win_ref, qnorm_ref, wqb_ref, kvnorm_ref, wkvb_ref,
                    gq_ref, gk_ref, cos_ref, sin_ref, *out_refs, with_q):
    if with_q:
        qn_ref, qr_ref, k_ref, v_ref = out_refs
    else:
        k_ref, v_ref = out_refs
    mod = mod_ref[0]
    hx = _modulate(x_ref[0], gmix_ref[...], mod[0:1], mod[1:2])
    lat = jnp.dot(hx.astype(BF16), win_ref[...], preferred_element_type=F32)
    cos = cos_ref[...]
    sin = sin_ref[...]
    lo = lax.broadcasted_iota(jnp.int32, (1, LANES), 1) < QK_ROPE

    def half_sums(v):
        v2 = v * v
        return (jnp.sum(jnp.where(lo, v2, 0.0), axis=-1, keepdims=True),
                jnp.sum(jnp.where(lo, 0.0, v2), axis=-1, keepdims=True))

    if with_q:
        gq = gq_ref[...]
        qn_in = (_rms(lat[:, :Q_LORA]) * qnorm_ref[...]).astype(BF16)
        qf = jnp.dot(qn_in, wqb_ref[...], preferred_element_type=F32)
        nope_w = MLA_HEADS * QK_NOPE
        pair_w = (MLA_HEADS // 2) * LANES
        for p in range(MLA_HEADS // 2):
            rp = qf[:, nope_w + p * LANES: nope_w + (p + 1) * LANES]
            sw = qf[:, nope_w + pair_w + p * LANES: nope_w + pair_w + (p + 1) * LANES]
            s_pair = half_sums(rp)
            r_pair = []
            for hh in range(2):
                h = 2 * p + hh
                nope = qf[:, h * QK_NOPE:(h + 1) * QK_NOPE]
                ms = (jnp.sum(nope * nope, axis=-1, keepdims=True) + s_pair[hh]) * (1.0 / QK_DIM)
                r = lax.rsqrt(ms + EPS) * (ATTN_SCALE * LOG2_E)
                r_pair.append(r)
                qn_ref[0, h] = ((nope * r) * gq[0:1]).astype(BF16)
            roped = (rp * gq[1:2]) * cos + (sw * gq[2:3]) * sin
            qr_ref[0, p] = (roped * jnp.where(lo, r_pair[0], r_pair[1])).astype(BF16)

    gk = gk_ref[...]
    kv_lo = Q_LORA
    kvn_in = (_rms(lat[:, kv_lo:kv_lo + KV_LORA]) * kvnorm_ref[...]).astype(BF16)
    kv = jnp.dot(kvn_in, wkvb_ref[...], preferred_element_type=F32)
    pe_lo = Q_LORA + KV_LORA
    kr2 = lat[:, pe_lo:pe_lo + LANES]
    ks2 = lat[:, pe_lo + LANES:pe_lo + 2 * LANES]
    s_pe = half_sums(kr2)[0]
    k_roped = (kr2 * gk[1:2]) * cos + (ks2 * gk[2:3]) * sin
    for h in range(MLA_HEADS):
        nope = kv[:, h * QK_NOPE:(h + 1) * QK_NOPE]
        ms = (jnp.sum(nope * nope, axis=-1, keepdims=True) + s_pe) * (1.0 / QK_DIM)
        r = lax.rsqrt(ms + EPS)
        k_ref[0, h, :, 0:QK_NOPE] = ((nope * r) * gk[0:1]).astype(BF16)
        keep = lo if h % 2 == 0 else jnp.logical_not(lo)
        k_ref[0, h, :, QK_NOPE:QK_NOPE + LANES] = jnp.where(keep, k_roped * r, 0.0).astype(BF16)
        v_off = MLA_HEADS * QK_NOPE + h * V_DIM
        v_ref[0, h, :, 0:V_DIM] = kv[:, v_off:v_off + V_DIM].astype(BF16)
        v_ref[0, h, :, V_DIM:2 * V_DIM] = jnp.ones((kv.shape[0], V_DIM), BF16)


def _mla_pre(x, mod, shared_mod, gmix, w, cos, sin, *, with_q, tm):
    b, n, d = x.shape
    tm = min(tm, n)
    const = lambda shape: pl.BlockSpec(shape, lambda bi, i: (0,) * len(shape))
    mod_map = (lambda bi, i: (0, 0, 0)) if shared_mod else (lambda bi, i: (bi, 0, 0))
    in_specs = [
        pl.BlockSpec((1, tm, d), lambda bi, i: (bi, i, 0)),
        pl.BlockSpec((1, 6, d), mod_map),
        const((1, d)),
        const(w["w_in"].shape),
        const((1, Q_LORA)),
        const(w["w_qb"].shape),
        const((1, KV_LORA)),
        const(w["w_kvb"].shape),
        const((3, LANES)),
        const((3, LANES)),
        pl.BlockSpec((tm, LANES), lambda bi, i: (i, 0)),
        pl.BlockSpec((tm, LANES), lambda bi, i: (i, 0)),
    ]
    head_spec = lambda nh, w_: pl.BlockSpec((1, nh, tm, w_), lambda bi, i: (bi, 0, i, 0))
    out_specs = [head_spec(MLA_HEADS, 2 * LANES), head_spec(MLA_HEADS, 2 * V_DIM)]
    out_shape = [jax.ShapeDtypeStruct((b, MLA_HEADS, n, 2 * LANES), BF16),
                 jax.ShapeDtypeStruct((b, MLA_HEADS, n, 2 * V_DIM), BF16)]
    if with_q:
        out_specs = [head_spec(MLA_HEADS, QK_NOPE), head_spec(MLA_HEADS // 2, LANES)] + out_specs
        out_shape = [jax.ShapeDtypeStruct((b, MLA_HEADS, n, QK_NOPE), BF16),
                     jax.ShapeDtypeStruct((b, MLA_HEADS // 2, n, LANES), BF16)] + out_shape
    return pl.pallas_call(
        functools.partial(_mla_pre_kernel, with_q=with_q),
        grid=(b, n // tm),
        in_specs=in_specs,
        out_specs=out_specs,
        out_shape=out_shape,
        compiler_params=_cparams(("parallel", "parallel")),
        name="mla_pre_q" if with_q else "mla_pre_ctx",
    )(x, mod, gmix, w["w_in"], w["q_norm"], w["w_qb"], w["kv_norm"], w["w_kvb"],
      w["gq"], w["gk"], cos, sin)


def _attn_kernel(qn_ref, qr_ref, kc_ref, vc_ref, kx_ref, vx_ref, o_ref,
                 m_s, acc_s, sc_s, pc_s, s0_s, s1_s, p0_s, p1_s, a0_s, a1_s, *, tk):
    q = jnp.concatenate([qn_ref[0, 0], qr_ref[0, 0]], axis=-1)
    nblk = 1 + kx_ref.shape[2] // tk
    s_buf = lambda i: sc_s if i == 0 else (s0_s, s1_s)[i % 2]
    p_buf = lambda i: pc_s if i == 0 else (p0_s, p1_s)[i % 2]
    a_buf = lambda i: (a0_s, a1_s)[i % 2]
    keys = lambda i: kc_ref[0, 0] if i == 0 else kx_ref[0, 0, (i - 1) * tk:i * tk, :]
    vals = lambda i: vc_ref[0, 0] if i == 0 else vx_ref[0, 0, (i - 1) * tk:i * tk, :]

    def scores(i):
        s_buf(i)[...] = lax.dot_general(q, keys(i), (((1,), (1,)), ((), ())), preferred_element_type=F32)

    def softmax(i):
        for r in range(0, q.shape[0], SOFTMAX_ROWS):
            rows = slice(r, r + SOFTMAX_ROWS)
            s = s_buf(i)[rows, :]
            m_prev = m_s[rows, :]
            m_new = jnp.maximum(m_prev, jnp.max(s, axis=-1, keepdims=True))
            p_buf(i)[rows, :] = jnp.exp2(s - m_new).astype(BF16)
            a_buf(i)[rows, :] = jnp.exp2(m_prev - m_new)
            m_s[rows, :] = m_new

    def accumulate(i):
        acc_s[...] = a_buf(i)[...] * acc_s[...] + jnp.dot(p_buf(i)[...], vals(i), preferred_element_type=F32)

    m_s[...] = jnp.full(m_s.shape, -jnp.inf, F32)
    acc_s[...] = jnp.zeros(acc_s.shape, F32)
    scores(0)
    scores(1)
    softmax(0)
    for i in range(nblk):
        if i + 2 < nblk:
            scores(i + 2)
        if i + 1 < nblk:
            softmax(i + 1)
        accumulate(i)
    acc = acc_s[...]
    o_ref[0] = (acc[:, :V_DIM] / acc[:, V_DIM:]).astype(o_ref.dtype)


def _attention(qn, qr, kc, vc, kx, vx, *, tq, tk):
    b, h, n, _ = qn.shape
    nc = kc.shape[2]
    tq = min(tq, n)
    tk = min(tk, n)
    assert n // tk <= MAX_UNROLLED_KEY_BLOCKS
    whole = lambda rows: pl.BlockSpec((1, 1, rows, 2 * LANES), lambda bi, hi, i: (bi, hi, 0, 0))
    return pl.pallas_call(
        functools.partial(_attn_kernel, tk=tk),
        grid=(b, h, n // tq),
        in_specs=[
            pl.BlockSpec((1, 1, tq, QK_NOPE), lambda bi, hi, i: (bi, hi, i, 0)),
            pl.BlockSpec((1, 1, tq, LANES), lambda bi, hi, i: (bi, hi // 2, i, 0)),
            whole(nc), whole(nc), whole(n), whole(n),
        ],
        out_specs=pl.BlockSpec((1, tq, V_DIM), lambda bi, hi, i: (bi, i, hi)),
        out_shape=jax.ShapeDtypeStruct((b, n, h * V_DIM), BF16),
        scratch_shapes=[
            pltpu.VMEM((tq, 1), F32), pltpu.VMEM((tq, 2 * V_DIM), F32),
            pltpu.VMEM((tq, nc), F32), pltpu.VMEM((tq, nc), BF16),
            pltpu.VMEM((tq, tk), F32), pltpu.VMEM((tq, tk), F32),
            pltpu.VMEM((tq, tk), BF16), pltpu.VMEM((tq, tk), BF16),
            pltpu.VMEM((tq, 1), F32), pltpu.VMEM((tq, 1), F32),
        ],
        compiler_params=_cparams(("parallel", "parallel", "arbitrary")),
        name="flash_attn",
    )(qn, qr, kc, vc, kx, vx)


def _ffn_pre(x_new, mod, gffn_ref, rw_ref, h_ref, aff_ref):
    h2 = _modulate(x_new, gffn_ref[...], mod[3:4], mod[4:5])
    hi = h2.astype(BF16)
    lo = (h2 - hi.astype(F32)).astype(BF16)
    h_ref[0] = hi
    rw = rw_ref[...]
    t = jnp.dot(hi, rw, preferred_element_type=F32) + jnp.dot(lo, rw, preferred_element_type=F32)
    logits = t[:, :LANES] + t[:, LANES:]
    is_expert = lax.broadcasted_iota(jnp.int32, (1, LANES), 1) < N_EXPERTS
    logits = jnp.where(is_expert, logits, -jnp.inf)
    e = jnp.exp(logits - jnp.max(logits, axis=-1, keepdims=True))
    aff = e / jnp.sum(e, axis=-1, keepdims=True)
    aff_ref[0] = aff.T[:N_EXPERTS]


def _attn_out_kernel(o_ref, x_ref, mod_ref, wout_ref, gffn_ref, rw_ref, x1_ref, h_ref, aff_ref):
    mod = mod_ref[0]
    o = jnp.dot(o_ref[0], wout_ref[...], preferred_element_type=F32)
    x1 = x_ref[0] + mod[2:3] * o
    x1_ref[0] = x1
    _ffn_pre(x1, mod, gffn_ref, rw_ref, h_ref, aff_ref)


def _attn_out(ox, x, mod, w_out, gffn, rw, *, tm):
    b, n, d = x.shape
    tm = min(tm, n)
    const = lambda shape: pl.BlockSpec(shape, lambda bi, i: (0,) * len(shape))
    tok = lambda: pl.BlockSpec((1, tm, d), lambda bi, i: (bi, i, 0))
    return pl.pallas_call(
        _attn_out_kernel,
        grid=(b, n // tm),
        in_specs=[tok(), tok(), pl.BlockSpec((1, 6, d), lambda bi, i: (bi, 0, 0)),
                  const(w_out.shape), const((1, d)), const(rw.shape)],
        out_specs=[tok(), tok(), pl.BlockSpec((1, N_EXPERTS, tm), lambda bi, i: (bi, 0, i))],
        out_shape=[jax.ShapeDtypeStruct((b, n, d), F32),
                   jax.ShapeDtypeStruct((b, n, d), BF16),
                   jax.ShapeDtypeStruct((b, N_EXPERTS, n), F32)],
        compiler_params=_cparams(("parallel", "parallel")),
        name="attn_out",
    )(ox, x, mod, w_out, gffn, rw)


def _conv_kernel(xm_ref, xp_ref, xn_ref, mm_ref, mp_ref, mn_ref, gprev_ref, mod_ref, gmix_ref,
                 win_ref, cw_ref, wout_ref, gffn_ref, rw_ref, x3_ref, h_ref, aff_ref):
    i = pl.program_id(1)
    tm, d = xm_ref.shape[1], xm_ref.shape[2]
    halo = xp_ref.shape[1]
    gprev = gprev_ref[0]
    mod = mod_ref[0]
    xm = xm_ref[0] + gprev * _from_row_tiles(mm_ref)
    xe = jnp.concatenate([xp_ref[0] + gprev * _from_row_tiles(mp_ref), xm,
                          xn_ref[0] + gprev * _from_row_tiles(mn_ref)], axis=0)
    hx = _modulate(xe, gmix_ref[...], mod[0:1], mod[1:2]).astype(BF16)
    proj = jnp.dot(hx, win_ref[...], preferred_element_type=F32)
    u = proj[:, d:2 * d] * proj[:, 2 * d:3 * d]
    row = lax.broadcasted_iota(jnp.int32, (tm + 2 * halo, 1), 0)
    outside = jnp.logical_or(jnp.logical_and(i == 0, row < halo),
                             jnp.logical_and(i == pl.num_programs(1) - 1, row >= tm + halo))
    u = jnp.where(outside, 0.0, u)
    rows = tm + 2 * halo
    u_prev = pltpu.roll(u, 1, axis=0)[halo:halo + tm]
    u_next = pltpu.roll(u, rows - 1, axis=0)[halo:halo + tm]
    cw = cw_ref[...]
    y = cw[0:1] * u_prev + cw[1:2] * u[halo:halo + tm] + cw[2:3] * u_next
    z = (proj[halo:halo + tm, 0:d] * y).astype(BF16)
    x3 = xm + mod[2:3] * jnp.dot(z, wout_ref[...], preferred_element_type=F32)
    x3_ref[0] = x3
    _ffn_pre(x3, mod, gffn_ref, rw_ref, h_ref, aff_ref)


def _conv_mixer(x, moe, gprev, mod, gmix, w_in, cw, w_out, gffn, rw, *, tm):
    b, n, d = x.shape
    tm = min(tm, n)
    halo = 8
    nb = tm // halo
    last = n // halo - 1
    const = lambda shape: pl.BlockSpec(shape, lambda bi, i: (0,) * len(shape))
    tok = lambda: pl.BlockSpec((1, tm, d), lambda bi, i: (bi, i, 0))
    prev = lambda: pl.BlockSpec((1, halo, d), lambda bi, i: (bi, jnp.maximum(i * nb - 1, 0), 0))
    nxt = lambda: pl.BlockSpec((1, halo, d), lambda bi, i: (bi, jnp.minimum((i + 1) * nb, last), 0))
    rt = (SUBLANES, LANES)
    tok_rt = pl.BlockSpec((1, tm) + rt, lambda bi, i: (bi, i, 0, 0))
    prev_rt = pl.BlockSpec((1, halo) + rt, lambda bi, i: (bi, jnp.maximum(i * nb - 1, 0), 0, 0))
    nxt_rt = pl.BlockSpec((1, halo) + rt, lambda bi, i: (bi, jnp.minimum((i + 1) * nb, last), 0, 0))
    return pl.pallas_call(
        _conv_kernel,
        grid=(b, n // tm),
        in_specs=[tok(), prev(), nxt(), tok_rt, prev_rt, nxt_rt,
                  pl.BlockSpec((1, 1, d), lambda bi, i: (bi, 0, 0)),
                  pl.BlockSpec((1, 6, d), lambda bi, i: (bi, 0, 0)),
                  const((1, d)), const(w_in.shape), const(cw.shape), const(w_out.shape),
                  const((1, d)), const(rw.shape)],
        out_specs=[tok(), tok(), pl.BlockSpec((1, N_EXPERTS, tm), lambda bi, i: (bi, 0, i))],
        out_shape=[jax.ShapeDtypeStruct((b, n, d), F32),
                   jax.ShapeDtypeStruct((b, n, d), BF16),
                   jax.ShapeDtypeStruct((b, N_EXPERTS, n), F32)],
        compiler_params=_cparams(("parallel", "parallel")),
        name="conv_mixer",
    )(x, x, x, moe, moe, moe, gprev, mod, gmix, w_in, cw, w_out, gffn, rw)


def _moe_kernel(xs_ref, g_ref, wg_ref, wu_ref, wd_ref, y_ref, wg_s, wu_s, wd_s, *, tr, tf):
    c = xs_ref.shape[2]
    f_total = wg_s.shape[1]

    @pl.when(pl.program_id(1) == 0)
    def _():
        for src, dst in ((wg_ref, wg_s), (wu_ref, wu_s), (wd_ref, wd_s)):
            for r in range(0, dst.shape[0], tr):
                dst[r:r + tr, :] = src[0, 0, r:r + tr, :].astype(BF16)

    for r0 in range(0, c, tr):
        xs = xs_ref[0, 0, r0:r0 + tr, :]
        acc = jnp.zeros((tr, wd_s.shape[1]), F32)
        for f in range(f_total // tf):
            a = jnp.dot(xs, wg_s[:, f * tf:(f + 1) * tf], preferred_element_type=F32)
            u = jnp.dot(xs, wu_s[:, f * tf:(f + 1) * tf], preferred_element_type=F32)
            hm = (_silu(a) * u).astype(BF16)
            acc = acc + jnp.dot(hm, wd_s[f * tf:(f + 1) * tf, :], preferred_element_type=F32)
        y = acc * g_ref[0, 0, r0:r0 + tr, :]
        for k in range(SUBLANES):
            y_ref[0, 0, r0:r0 + tr, k, :] = y[:, k * LANES:(k + 1) * LANES]


def _moe_ffn(xs, g, wg, wu, wd, layer):
    b, e, c, d = xs.shape
    f = wg.shape[3]
    tr = min(MOE_ROW_TILE, c)
    tf = min(512, f)
    w_spec = lambda rows_, cols: pl.BlockSpec((1, 1, rows_, cols), lambda ei, bi: (layer, ei, 0, 0))
    return pl.pallas_call(
        functools.partial(_moe_kernel, tr=tr, tf=tf),
        grid=(e, b),
        in_specs=[
            pl.BlockSpec((1, 1, c, d), lambda ei, bi: (bi, ei, 0, 0)),
            pl.BlockSpec((1, 1, c, 1), lambda ei, bi: (bi, ei, 0, 0)),
            w_spec(d, f), w_spec(d, f), w_spec(f, d),
        ],
        out_specs=pl.BlockSpec((1, 1, c, SUBLANES, LANES), lambda ei, bi: (bi, ei, 0, 0, 0)),
        out_shape=jax.ShapeDtypeStruct((b, e, c, SUBLANES, LANES), F32),
        scratch_shapes=[pltpu.VMEM((d, f), BF16), pltpu.VMEM((d, f), BF16), pltpu.VMEM((f, d), BF16)],
        compiler_params=_cparams(("arbitrary", "arbitrary")),
        name="moe_ffn",
    )(xs, g.reshape(b, e, c, 1), wg, wu, wd)


def _from_row_tiles(ref):
    return jnp.concatenate([ref[0, :, k, :] for k in range(SUBLANES)], axis=-1)


def _combine_kernel(idx_ref, y_ref, out_hbm, acc_s, sem, *, unroll, zero_rows):
    bi = pl.program_id(0)
    ei = pl.program_id(1)
    n = acc_s.shape[0]
    c = y_ref.shape[2]

    @pl.when(ei == 0)
    def _():
        def zero(i, carry):
            acc_s[pl.ds(pl.multiple_of(i * zero_rows, zero_rows), zero_rows)] = jnp.zeros(
                (zero_rows,) + acc_s.shape[1:], F32)
            return carry

        lax.fori_loop(0, n // zero_rows, zero, 0)

    def rows(i, carry):
        base = i * unroll
        tok = [idx_ref[0, 0, base + u] for u in range(unroll)]
        new = [acc_s[tok[u]] + y_ref[0, 0, base + u] for u in range(unroll)]
        for u in range(unroll):
            acc_s[tok[u]] = new[u]
        return carry

    lax.fori_loop(0, c // unroll, rows, 0)

    @pl.when(ei == pl.num_programs(1) - 1)
    def _():
        cp = pltpu.make_async_copy(acc_s, out_hbm.at[bi], sem)
        cp.start()
        cp.wait()


def _moe_combine(idx, y, n):
    b, e, c = idx.shape
    unroll = 8
    zero_rows = min(256, n)
    return pl.pallas_call(
        functools.partial(_combine_kernel, unroll=unroll, zero_rows=zero_rows),
        grid=(b, e),
        in_specs=[
            pl.BlockSpec((1, 1, c), lambda bi, ei: (bi * e + ei, 0, 0), memory_space=pltpu.SMEM),
            pl.BlockSpec((1, 1, c, SUBLANES, LANES), lambda bi, ei: (bi, ei, 0, 0, 0)),
        ],
        out_specs=pl.BlockSpec(memory_space=pl.ANY),
        out_shape=jax.ShapeDtypeStruct((b, n, SUBLANES, LANES), F32),
        scratch_shapes=[pltpu.VMEM((n, SUBLANES, LANES), F32), pltpu.SemaphoreType.DMA],
        compiler_params=_cparams(("arbitrary", "arbitrary")),
        name="moe_combine",
    )(idx.reshape(b * e, 1, c), y)


def _route_kernel(aff_ref, uexcl_ref, uinclt_ref, ones_ref, lstrict_ref, idx_ref, gt_s, eq_s, need_s, *, cap):
    e_n, nch, _ = aff_ref.shape[1:]
    aff = aff_ref[0]

    def count(mask):
        return jnp.sum(jnp.sum(mask.astype(F32), axis=2, keepdims=True), axis=1, keepdims=True)

    def search(_, carry):
        lo, hi = carry
        mid = lo + ((hi - lo + 1) >> 1)
        ok = count(aff >= pltpu.bitcast(mid, F32)) >= cap
        return jnp.where(ok, mid, lo), jnp.where(ok, hi, mid - 1)

    lo0 = jnp.zeros((e_n, 1, 1), jnp.int32)
    hi0 = jnp.full((e_n, 1, 1), F32_INF_BITS, jnp.int32)
    tau_bits, _ = lax.fori_loop(0, F32_VALUE_BITS, search, (lo0, hi0))
    tau = pltpu.bitcast(tau_bits, F32)
    gt = aff > tau
    eq = aff == tau
    gt_s[...] = gt.astype(BF16)
    eq_s[...] = eq.astype(BF16)
    need_s[...] = jnp.broadcast_to(cap - count(gt), need_s.shape)

    n_slot = idx_ref.shape[3]
    slot = lax.broadcasted_iota(jnp.int32, (1, n_slot), 1).astype(F32)
    chunk_id = lax.broadcasted_iota(jnp.int32, (nch, n_slot), 0).astype(F32)
    widen = lambda a: jnp.concatenate([a] * (n_slot // LANES), axis=1)

    def chunk_prefix(mask):
        cnt = jnp.dot(mask, ones_ref[...], preferred_element_type=F32)
        return cnt, jnp.dot(lstrict_ref[...], cnt.astype(BF16), preferred_element_type=F32)

    def per_expert(e, carry):
        eq_e = eq_s[e]
        _, eq_start = chunk_prefix(eq_e)
        eq_rank = eq_start + jnp.dot(eq_e, uexcl_ref[...], preferred_element_type=F32)
        sel = gt_s[e] + jnp.where(eq_rank < need_s[e], eq_e, jnp.zeros_like(eq_e))
        cnt, start = chunk_prefix(sel)
        incl_t = lax.dot_general(uinclt_ref[...], sel, (((1,), (1,)), ((), ())),
                                 preferred_element_type=F32)
        start_w, cnt_w = widen(start), widen(cnt)
        hit = jnp.logical_and(start_w <= slot, slot < start_w + cnt_w)
        hit_f = hit.astype(F32)
        slot_start = jnp.sum(hit_f * start_w, axis=0, keepdims=True)
        slot_chunk = jnp.sum(hit_f * chunk_id, axis=0, keepdims=True)
        incl_of_slot = jnp.dot(incl_t.astype(BF16), hit.astype(BF16), preferred_element_type=F32)
        local = jnp.sum((incl_of_slot <= slot - slot_start).astype(F32), axis=0, keepdims=True)
        idx_ref[0, e] = (slot_chunk * LANES + local).astype(jnp.int32)
        return carry

    lax.fori_loop(0, e_n, per_expert, 0)


def _route(aff, cap):
    b, e, n = aff.shape
    assert n % LANES == 0 and cap % LANES == 0
    nch = n // LANES
    i = jnp.arange(LANES)
    t = jnp.arange(nch)
    uexcl = (i[:, None] < i[None, :]).astype(BF16)
    uinclt = (i[None, :] <= i[:, None]).astype(BF16)
    lstrict = (t[None, :] < t[:, None]).astype(BF16)
    const = lambda shape: pl.BlockSpec(shape, lambda bi: (0,) * len(shape))
    mask_scratch = lambda dt: pltpu.VMEM((e, nch, LANES), dt)
    idx = pl.pallas_call(
        functools.partial(_route_kernel, cap=cap),
        grid=(b,),
        in_specs=[pl.BlockSpec((1, e, nch, LANES), lambda bi: (bi, 0, 0, 0)),
                  const((LANES, LANES)), const((LANES, LANES)), const((LANES, LANES)), const((nch, nch))],
        out_specs=pl.BlockSpec((1, e, 1, cap), lambda bi: (bi, 0, 0, 0)),
        out_shape=jax.ShapeDtypeStruct((b, e, 1, cap), jnp.int32),
        scratch_shapes=[mask_scratch(BF16), mask_scratch(BF16), mask_scratch(F32)],
        compiler_params=_cparams(("parallel",)),
        name="moe_route",
    )(aff.reshape(b, e, nch, LANES), uexcl, uinclt, jnp.ones((LANES, LANES), BF16), lstrict)
    return idx.reshape(b, e, cap)


def _ec_moe(h, aff, wg, wu, wd, layer):
    b, n, d = h.shape
    assert d == SUBLANES * LANES
    cap = EC_CAPACITY * n // N_EXPERTS
    idx = _route(aff, cap)
    g = jnp.take_along_axis(aff, idx, axis=-1)
    xs = jax.vmap(lambda hb, ib: hb[ib])(h, idx)
    y = _moe_ffn(xs, g, wg, wu, wd, layer)
    return _moe_combine(idx, y, n)


def _residual_kernel(x_ref, m_ref, g_ref, o_ref):
    o_ref[0] = x_ref[0] + g_ref[0] * _from_row_tiles(m_ref)


def _residual(x, moe, gate, *, tm):
    b, n, d = x.shape
    tm = min(tm, n)
    tok = lambda: pl.BlockSpec((1, tm, d), lambda bi, i: (bi, i, 0))
    return pl.pallas_call(
        _residual_kernel,
        grid=(b, n // tm),
        in_specs=[tok(), pl.BlockSpec((1, tm, SUBLANES, LANES), lambda bi, i: (bi, i, 0, 0)),
                  pl.BlockSpec((1, 1, d), lambda bi, i: (bi, 0, 0))],
        out_specs=tok(),
        out_shape=jax.ShapeDtypeStruct((b, n, d), F32),
        compiler_params=_cparams(("parallel", "parallel")),
        name="moe_residual",
    )(x, moe, gate)


def _rope_tables(n):
    rows = n // GRID_W
    row = jnp.broadcast_to(jnp.arange(rows, dtype=F32)[:, None], (rows, GRID_W)).reshape(-1)
    col = jnp.broadcast_to(jnp.arange(GRID_W, dtype=F32)[None, :], (rows, GRID_W)).reshape(-1)
    inv = ROPE_THETA ** (-jnp.arange(ROPE_FREQS, dtype=F32) / ROPE_FREQS)
    ar, ac = row[:, None] * inv, col[:, None] * inv
    cos = jnp.concatenate([jnp.cos(ar), jnp.cos(ar), jnp.cos(ac), jnp.cos(ac)], axis=-1)
    sin = jnp.concatenate([-jnp.sin(ar), jnp.sin(ar), -jnp.sin(ac), jnp.sin(ac)], axis=-1)
    return jnp.tile(cos, (1, 2)), jnp.tile(sin, (1, 2))


def _swap_perm():
    f = ROPE_FREQS
    base = jnp.arange(QK_ROPE)
    return jnp.where((base // f) % 2 == 0, base + f, base - f)


def _mla_weights(w_in, q_norm, w_qb, kv_norm, w_kvb, q_gain, k_gain):
    perm = _swap_perm()
    pe = w_in[:, Q_LORA + KV_LORA:]
    pe_sw = pe[:, perm]
    w_in_x = jnp.concatenate([w_in[:, :Q_LORA + KV_LORA], pe, pe, pe_sw, pe_sw], axis=1)
    wq = w_qb.reshape(Q_LORA, MLA_HEADS, QK_DIM)
    wq_rope = wq[:, :, QK_NOPE:]
    w_qb_x = jnp.concatenate([
        wq[:, :, :QK_NOPE].reshape(Q_LORA, -1),
        wq_rope.reshape(Q_LORA, -1),
        wq_rope[:, :, perm].reshape(Q_LORA, -1)], axis=1)
    wkv = w_kvb.reshape(KV_LORA, MLA_HEADS, QK_NOPE + V_DIM)
    w_kvb_x = jnp.concatenate([wkv[:, :, :QK_NOPE].reshape(KV_LORA, -1),
                               wkv[:, :, QK_NOPE:].reshape(KV_LORA, -1)], axis=1)

    def gains(g):
        gr = g[QK_NOPE:]
        return jnp.stack([g[:QK_NOPE], jnp.tile(gr, 2), jnp.tile(gr[perm], 2)])

    return {
        "w_in": w_in_x.astype(BF16), "q_norm": q_norm[None, :], "w_qb": w_qb_x.astype(BF16),
        "kv_norm": kv_norm[None, :], "w_kvb": w_kvb_x.astype(BF16),
        "gq": gains(q_gain), "gk": gains(k_gain),
    }


def _pad_router(rw):
    hi = rw.astype(BF16)
    lo = (rw - hi.astype(F32)).astype(BF16)
    pad = ((0, 0), (0, LANES - rw.shape[1]))
    return jnp.concatenate([jnp.pad(hi, pad), jnp.pad(lo, pad)], axis=1)


def kernel(x, c, ctx, c_ctx, norm_mix, norm_ffn, ada_w, ada_b, mla_w_in, mla_q_norm, mla_w_qb, mla_kv_norm, mla_w_kvb, mla_q_gain, mla_k_gain, mla_w_out, conv_w_in, conv_w, conv_w_out, router_w, exp_w_gate, exp_w_up, exp_w_down):
    b, n, d = x.shape
    nc = ctx.shape[1]
    depth = ada_w.shape[0]
    assert depth == 2 and b < 8

    cond = jnp.concatenate([c, c_ctx[None, :], jnp.zeros((8 - b - 1, d), F32)], axis=0)
    mod_all = _adaln(cond, ada_w, ada_b).reshape(depth, 8, 6, d)
    mod0, mod1 = mod_all[0, :b], mod_all[1, :b]
    mod0_ctx = mod_all[0, b:b + 1]

    w = _mla_weights(mla_w_in[0], mla_q_norm[0], mla_w_qb[0], mla_kv_norm[0], mla_w_kvb[0],
                     mla_q_gain[0], mla_k_gain[0])
    cos, sin = _rope_tables(n)
    gmix0 = norm_mix[0][None, :]
    qn, qr, kx, vx = _mla_pre(x, mod0, False, gmix0, w, cos, sin, with_q=True, tm=PROJ_TOKEN_TILE)
    kc, vc = _mla_pre(ctx, mod0_ctx, True, gmix0, w, jnp.ones((nc, LANES), F32),
                      jnp.zeros((nc, LANES), F32), with_q=False, tm=PROJ_TOKEN_TILE)
    ox = _attention(qn, qr, kc, vc, kx, vx, tq=ATTN_Q_TILE, tk=ATTN_KEY_TILE)
    x1, h0, aff0 = _attn_out(ox, x, mod0, mla_w_out[0].astype(BF16), norm_ffn[0][None, :],
                             _pad_router(router_w[0]), tm=PROJ_TOKEN_TILE)
    moe0 = _ec_moe(h0, aff0, exp_w_gate, exp_w_up, exp_w_down, 0)

    x3, h1, aff1 = _conv_mixer(x1, moe0, mod0[:, 5:6], mod1, norm_mix[1][None, :],
                               conv_w_in[0].astype(BF16), conv_w[0], conv_w_out[0].astype(BF16),
                               norm_ffn[1][None, :], _pad_router(router_w[1]), tm=TOKEN_TILE)
    moe1 = _ec_moe(h1, aff1, exp_w_gate, exp_w_up, exp_w_down, 1)
    return _residual(x3, moe1, mod1[:, 5:6], tm=TOKEN_TILE)
```

```python
import functools

import jax
import jax.numpy as jnp
from jax import lax
from jax.experimental import pallas as pl
from jax.experimental.pallas import tpu as pltpu

F32 = jnp.float32
BF16 = jnp.bfloat16
HIGHEST = lax.Precision.HIGHEST

GRID_W = 64
N_MIXERS = 2
MLA_HEADS = 8
QK_NOPE = 128
QK_ROPE = 64
QK_DIM = QK_NOPE + QK_ROPE
V_DIM = 128
Q_LORA = 384
KV_LORA = 256
ROPE_FREQS = QK_ROPE // 4
ROPE_THETA = 10000.0
ATTN_SCALE = QK_DIM ** -0.5
LOG2_E = 1.4426950408889634
N_EXPERTS = 16
EC_CAPACITY = 2
EPS = 1e-6

LANES = 128
SUBLANES = 8
F32_INF_BITS = 0x7F800000
F32_VALUE_BITS = 31
VMEM_LIMIT = 56 * 1024 * 1024
MAX_UNROLLED_KEY_BLOCKS = 16
TOKEN_TILE = 1024
PROJ_TOKEN_TILE = 1024
MOE_ROW_TILE = 512
MOE_EXPERT_GROUPS = 2
ATTN_Q_TILE = 1024
ATTN_KEY_TILE = 1024
SOFTMAX_ROWS = 32


def _cparams(sem):
    return pltpu.CompilerParams(dimension_semantics=sem, vmem_limit_bytes=VMEM_LIMIT)


def _rms(x):
    return x * lax.rsqrt(jnp.mean(x * x, axis=-1, keepdims=True) + EPS)


def _modulate(x, g, shift, scale):
    return (_rms(x) * g) * (1.0 + scale) + shift


def _silu(a):
    return a * jax.nn.sigmoid(a)


def _adaln_kernel(c_ref, w_ref, b_ref, o_ref):
    s = _silu(c_ref[...])
    o_ref[0] = jnp.dot(s, w_ref[0], precision=HIGHEST, preferred_element_type=F32) + b_ref[0]


def _adaln(cond, ada_w, ada_b):
    depth, d, d6 = ada_w.shape
    tn = 1536
    return pl.pallas_call(
        _adaln_kernel,
        grid=(depth, d6 // tn),
        in_specs=[
            pl.BlockSpec((8, d), lambda l, j: (0, 0)),
            pl.BlockSpec((1, d, tn), lambda l, j: (l, 0, j)),
            pl.BlockSpec((1, 1, tn), lambda l, j: (l, 0, j)),
        ],
        out_specs=pl.BlockSpec((1, 8, tn), lambda l, j: (l, 0, j)),
        out_shape=jax.ShapeDtypeStruct((depth, 8, d6), F32),
        compiler_params=_cparams(("parallel", "parallel")),
        name="adaln",
    )(cond, ada_w, ada_b.reshape(depth, 1, d6))


def _mla_pre_kernel(x_ref, mod_ref, gmix_ref, win_ref, qnorm_ref, wqb_ref, kvnorm_ref, wkvb_ref,
                    gq_ref, gk_ref, cos_ref, sin_ref, *out_refs, with_q):
    if with_q:
        qn_ref, qr_ref, k_ref, v_ref = out_refs
    else:
        k_ref, v_ref = out_refs
    mod = mod_ref[0]
    hx = _modulate(x_ref[0], gmix_ref[...], mod[0:1], mod[1:2])
    lat = jnp.dot(hx.astype(BF16), win_ref[...], preferred_element_type=F32)
    cos = cos_ref[...]
    sin = sin_ref[...]
    lo = lax.broadcasted_iota(jnp.int32, (1, LANES), 1) < QK_ROPE

    def half_sums(v):
        v2 = v * v
        return (jnp.sum(jnp.where(lo, v2, 0.0), axis=-1, keepdims=True),
                jnp.sum(jnp.where(lo, 0.0, v2), axis=-1, keepdims=True))

    if with_q:
        gq = gq_ref[...]
        qn_in = (_rms(lat[:, :Q_LORA]) * qnorm_ref[...]).astype(BF16)
        qf = jnp.dot(qn_in, wqb_ref[...], preferred_element_type=F32)
        nope_w = MLA_HEADS * QK_NOPE
        pair_w = (MLA_HEADS // 2) * LANES
        for p in range(MLA_HEADS // 2):
            rp = qf[:, nope_w + p * LANES: nope_w + (p + 1) * LANES]
            sw = qf[:, nope_w + pair_w + p * LANES: nope_w + pair_w + (p + 1) * LANES]
            s_pair = half_sums(rp)
            r_pair = []
            for hh in range(2):
                h = 2 * p + hh
                nope = qf[:, h * QK_NOPE:(h + 1) * QK_NOPE]
                ms = (jnp.sum(nope * nope, axis=-1, keepdims=True) + s_pair[hh]) * (1.0 / QK_DIM)
                r = lax.rsqrt(ms + EPS) * (ATTN_SCALE * LOG2_E)
                r_pair.append(r)
                qn_ref[0, h] = ((nope * r) * gq[0:1]).astype(BF16)
            roped = (rp * gq[1:2]) * cos + (sw * gq[2:3]) * sin
            qr_ref[0, p] = (roped * jnp.where(lo, r_pair[0], r_pair[1])).astype(BF16)

    gk = gk_ref[...]
    kv_lo = Q_LORA
    kvn_in = (_rms(lat[:, kv_lo:kv_lo + KV_LORA]) * kvnorm_ref[...]).astype(BF16)
    kv = jnp.dot(kvn_in, wkvb_ref[...], preferred_element_type=F32)
    pe_lo = Q_LORA + KV_LORA
    kr2 = lat[:, pe_lo:pe_lo + LANES]
    ks2 = lat[:, pe_lo + LANES:pe_lo + 2 * LANES]
    s_pe = half_sums(kr2)[0]
    k_roped = (kr2 * gk[1:2]) * cos + (ks2 * gk[2:3]) * sin
    for h in range(MLA_HEADS):
        nope = kv[:, h * QK_NOPE:(h + 1) * QK_NOPE]
        ms = (jnp.sum(nope * nope, axis=-1, keepdims=True) + s_pe) * (1.0 / QK_DIM)
        r = lax.rsqrt(ms + EPS)
        k_ref[0, h, :, 0:QK_NOPE] = ((nope * r) * gk[0:1]).astype(BF16)
        keep = lo if h % 2 == 0 else jnp.logical_not(lo)
        k_ref[0, h, :, QK_NOPE:QK_NOPE + LANES] = jnp.where(keep, k_roped * r, 0.0).astype(BF16)
        v_off = MLA_HEADS * QK_NOPE + h * V_DIM
        v_ref[0, h, :, 0:V_DIM] = kv[:, v_off:v_off + V_DIM].astype(BF16)
        v_ref[0, h, :, V_DIM:2 * V_DIM] = jnp.ones((kv.shape[0], V_DIM), BF16)


def _mla_pre(x, mod, shared_mod, gmix, w, cos, sin, *, with_q, tm):
    b, n, d = x.shape
    tm = min(tm, n)
    const = lambda shape: pl.BlockSpec(shape, lambda bi, i: (0,) * len(shape))
    mod_map = (lambda bi, i: (0, 0, 0)) if shared_mod else (lambda bi, i: (bi, 0, 0))
    in_specs = [
        pl.BlockSpec((1, tm, d), lambda bi, i: (bi, i, 0)),
        pl.BlockSpec((1, 6, d), mod_map),
        const((1, d)),
        const(w["w_in"].shape),
        const((1, Q_LORA)),
        const(w["w_qb"].shape),
        const((1, KV_LORA)),
        const(w["w_kvb"].shape),
        const((3, LANES)),
        const((3, LANES)),
        pl.BlockSpec((tm, LANES), lambda bi, i: (i, 0)),
        pl.BlockSpec((tm, LANES), lambda bi, i: (i, 0)),
    ]
    head_spec = lambda nh, w_: pl.BlockSpec((1, nh, tm, w_), lambda bi, i: (bi, 0, i, 0))
    out_specs = [head_spec(MLA_HEADS, 2 * LANES), head_spec(MLA_HEADS, 2 * V_DIM)]
    out_shape = [jax.ShapeDtypeStruct((b, MLA_HEADS, n, 2 * LANES), BF16),
                 jax.ShapeDtypeStruct((b, MLA_HEADS, n, 2 * V_DIM), BF16)]
    if with_q:
        out_specs = [head_spec(MLA_HEADS, QK_NOPE), head_spec(MLA_HEADS // 2, LANES)] + out_specs
        out_shape = [jax.ShapeDtypeStruct((b, MLA_HEADS, n, QK_NOPE), BF16),
                     jax.ShapeDtypeStruct((b, MLA_HEADS // 2, n, LANES), BF16)] + out_shape
    return pl.pallas_call(
        functools.partial(_mla_pre_kernel, with_q=with_q),
        grid=(b, n // tm),
        in_specs=in_specs,
        out_specs=out_specs,
        out_shape=out_shape,
        compiler_params=_cparams(("parallel", "parallel")),
        name="mla_pre_q" if with_q else "mla_pre_ctx",
    )(x, mod, gmix, w["w_in"], w["q_norm"], w["w_qb"], w["kv_norm"], w["w_kvb"],
      w["gq"], w["gk"], cos, sin)


def _attn_kernel(qn_ref, qr_ref, kc_ref, vc_ref, kx_ref, vx_ref, o_ref,
                 m_s, acc_s, sc_s, pc_s, s0_s, s1_s, p0_s, p1_s, a0_s, a1_s, *, tk):
    q = jnp.concatenate([qn_ref[0, 0], qr_ref[0, 0]], axis=-1)
    nblk = 1 + kx_ref.shape[2] // tk
    s_buf = lambda i: sc_s if i == 0 else (s0_s, s1_s)[i % 2]
    p_buf = lambda i: pc_s if i == 0 else (p0_s, p1_s)[i % 2]
    a_buf = lambda i: (a0_s, a1_s)[i % 2]
    keys = lambda i: kc_ref[0, 0] if i == 0 else kx_ref[0, 0, (i - 1) * tk:i * tk, :]
    vals = lambda i: vc_ref[0, 0] if i == 0 else vx_ref[0, 0, (i - 1) * tk:i * tk, :]

    def scores(i):
        s_buf(i)[...] = lax.dot_general(q, keys(i), (((1,), (1,)), ((), ())), preferred_element_type=F32)

    def softmax(i):
        for r in range(0, q.shape[0], SOFTMAX_ROWS):
            rows = slice(r, r + SOFTMAX_ROWS)
            s = s_buf(i)[rows, :]
            m_prev = m_s[rows, :]
            m_new = jnp.maximum(m_prev, jnp.max(s, axis=-1, keepdims=True))
            p_buf(i)[rows, :] = jnp.exp2(s - m_new).astype(BF16)
            a_buf(i)[rows, :] = jnp.exp2(m_prev - m_new)
            m_s[rows, :] = m_new

    def accumulate(i):
        acc_s[...] = a_buf(i)[...] * acc_s[...] + jnp.dot(p_buf(i)[...], vals(i), preferred_element_type=F32)

    m_s[...] = jnp.full(m_s.shape, -jnp.inf, F32)
    acc_s[...] = jnp.zeros(acc_s.shape, F32)
    scores(0)
    scores(1)
    softmax(0)
    for i in range(nblk):
        if i + 2 < nblk:
            scores(i + 2)
        if i + 1 < nblk:
            softmax(i + 1)
        accumulate(i)
    acc = acc_s[...]
    o_ref[0] = (acc[:, :V_DIM] / acc[:, V_DIM:]).astype(o_ref.dtype)


def _attention(qn, qr, kc, vc, kx, vx, *, tq, tk):
    b, h, n, _ = qn.shape
    nc = kc.shape[2]
    tq = min(tq, n)
    tk = min(tk, n)
    assert n // tk <= MAX_UNROLLED_KEY_BLOCKS
    whole = lambda rows: pl.BlockSpec((1, 1, rows, 2 * LANES), lambda bi, hi, i: (bi, hi, 0, 0))
    return pl.pallas_call(
        functools.partial(_attn_kernel, tk=tk),
        grid=(b, h, n // tq),
        in_specs=[
            pl.BlockSpec((1, 1, tq, QK_NOPE), lambda bi, hi, i: (bi, hi, i, 0)),
            pl.BlockSpec((1, 1, tq, LANES), lambda bi, hi, i: (bi, hi // 2, i, 0)),
            whole(nc), whole(nc), whole(n), whole(n),
        ],
        out_specs=pl.BlockSpec((1, tq, V_DIM), lambda bi, hi, i: (bi, i, hi)),
        out_shape=jax.ShapeDtypeStruct((b, n, h * V_DIM), BF16),
        scratch_shapes=[
            pltpu.VMEM((tq, 1), F32), pltpu.VMEM((tq, 2 * V_DIM), F32),
            pltpu.VMEM((tq, nc), F32), pltpu.VMEM((tq, nc), BF16),
            pltpu.VMEM((tq, tk), F32), pltpu.VMEM((tq, tk), F32),
            pltpu.VMEM((tq, tk), BF16), pltpu.VMEM((tq, tk), BF16),
            pltpu.VMEM((tq, 1), F32), pltpu.VMEM((tq, 1), F32),
        ],
        compiler_params=_cparams(("parallel", "parallel", "arbitrary")),
        name="flash_attn",
    )(qn, qr, kc, vc, kx, vx)


def _ffn_pre(x_new, mod, gffn_ref, rw_ref, h_ref, aff_ref):
    h2 = _modulate(x_new, gffn_ref[...], mod[3:4], mod[4:5])
    hi = h2.astype(BF16)
    lo = (h2 - hi.astype(F32)).astype(BF16)
    h_ref[0] = hi
    rw = rw_ref[...]
    t = jnp.dot(hi, rw, preferred_element_type=F32) + jnp.dot(lo, rw, preferred_element_type=F32)
    logits = t[:, :LANES] + t[:, LANES:]
    is_expert = lax.broadcasted_iota(jnp.int32, (1, LANES), 1) < N_EXPERTS
    logits = jnp.where(is_expert, logits, -jnp.inf)
    e = jnp.exp(logits - jnp.max(logits, axis=-1, keepdims=True))
    aff = e / jnp.sum(e, axis=-1, keepdims=True)
    aff_ref[0] = aff.T[:N_EXPERTS]


def _attn_out_kernel(o_ref, x_ref, mod_ref, wout_ref, gffn_ref, rw_ref, x1_ref, h_ref, aff_ref):
    mod = mod_ref[0]
    o = jnp.dot(o_ref[0], wout_ref[...], preferred_element_type=F32)
    x1 = x_ref[0] + mod[2:3] * o
    x1_ref[0] = x1
    _ffn_pre(x1, mod, gffn_ref, rw_ref, h_ref, aff_ref)


def _attn_out(ox, x, mod, w_out, gffn, rw, *, tm):
    b, n, d = x.shape
    tm = min(tm, n)
    const = lambda shape: pl.BlockSpec(shape, lambda bi, i: (0,) * len(shape))
    tok = lambda: pl.BlockSpec((1, tm, d), lambda bi, i: (bi, i, 0))
    return pl.pallas_call(
        _attn_out_kernel,
        grid=(b, n // tm),
        in_specs=[tok(), tok(), pl.BlockSpec((1, 6, d), lambda bi, i: (bi, 0, 0)),
                  const(w_out.shape), const((1, d)), const(rw.shape)],
        out_specs=[tok(), tok(), pl.BlockSpec((1, N_EXPERTS, tm), lambda bi, i: (bi, 0, i))],
        out_shape=[jax.ShapeDtypeStruct((b, n, d), F32),
                   jax.ShapeDtypeStruct((b, n, d), BF16),
                   jax.ShapeDtypeStruct((b, N_EXPERTS, n), F32)],
        compiler_params=_cparams(("parallel", "parallel")),
        name="attn_out",
    )(ox, x, mod, w_out, gffn, rw)


def _conv_kernel(xm_ref, xp_ref, xn_ref, mm_ref, mp_ref, mn_ref, gprev_ref, mod_ref, gmix_ref,
                 win_ref, cw_ref, wout_ref, gffn_ref, rw_ref, x3_ref, h_ref, aff_ref):
    i = pl.program_id(1)
    tm, d = xm_ref.shape[1], xm_ref.shape[2]
    halo = xp_ref.shape[1]
    gprev = gprev_ref[0]
    mod = mod_ref[0]
    xm = xm_ref[0] + gprev * _from_row_tiles(mm_ref)
    xe = jnp.concatenate([xp_ref[0] + gprev * _from_row_tiles(mp_ref), xm,
                          xn_ref[0] + gprev * _from_row_tiles(mn_ref)], axis=0)
    hx = _modulate(xe, gmix_ref[...], mod[0:1], mod[1:2]).astype(BF16)
    proj = jnp.dot(hx, win_ref[...], preferred_element_type=F32)
    u = proj[:, d:2 * d] * proj[:, 2 * d:3 * d]
    row = lax.broadcasted_iota(jnp.int32, (tm + 2 * halo, 1), 0)
    outside = jnp.logical_or(jnp.logical_and(i == 0, row < halo),
                             jnp.logical_and(i == pl.num_programs(1) - 1, row >= tm + halo))
    u = jnp.where(outside, 0.0, u)
    rows = tm + 2 * halo
    u_prev = pltpu.roll(u, 1, axis=0)[halo:halo + tm]
    u_next = pltpu.roll(u, rows - 1, axis=0)[halo:halo + tm]
    cw = cw_ref[...]
    y = cw[0:1] * u_prev + cw[1:2] * u[halo:halo + tm] + cw[2:3] * u_next
    z = (proj[halo:halo + tm, 0:d] * y).astype(BF16)
    x3 = xm + mod[2:3] * jnp.dot(z, wout_ref[...], preferred_element_type=F32)
    x3_ref[0] = x3
    _ffn_pre(x3, mod, gffn_ref, rw_ref, h_ref, aff_ref)


def _conv_mixer(x, moe, gprev, mod, gmix, w_in, cw, w_out, gffn, rw, *, tm):
    b, n, d = x.shape
    tm = min(tm, n)
    halo = 8
    nb = tm // halo
    last = n // halo - 1
    const = lambda shape: pl.BlockSpec(shape, lambda bi, i: (0,) * len(shape))
    tok = lambda: pl.BlockSpec((1, tm, d), lambda bi, i: (bi, i, 0))
    prev = lambda: pl.BlockSpec((1, halo, d), lambda bi, i: (bi, jnp.maximum(i * nb - 1, 0), 0))
    nxt = lambda: pl.BlockSpec((1, halo, d), lambda bi, i: (bi, jnp.minimum((i + 1) * nb, last), 0))
    rt = (SUBLANES, LANES)
    tok_rt = pl.BlockSpec((1, tm) + rt, lambda bi, i: (bi, i, 0, 0))
    prev_rt = pl.BlockSpec((1, halo) + rt, lambda bi, i: (bi, jnp.maximum(i * nb - 1, 0), 0, 0))
    nxt_rt = pl.BlockSpec((1, halo) + rt, lambda bi, i: (bi, jnp.minimum((i + 1) * nb, last), 0, 0))
    return pl.pallas_call(
        _conv_kernel,
        grid=(b, n // tm),
        in_specs=[tok(), prev(), nxt(), tok_rt, prev_rt, nxt_rt,
                  pl.BlockSpec((1, 1, d), lambda bi, i: (bi, 0, 0)),
                  pl.BlockSpec((1, 6, d), lambda bi, i: (bi, 0, 0)),
                  const((1, d)), const(w_in.shape), const(cw.shape), const(w_out.shape),
                  const((1, d)), const(rw.shape)],
        out_specs=[tok(), tok(), pl.BlockSpec((1, N_EXPERTS, tm), lambda bi, i: (bi, 0, i))],
        out_shape=[jax.ShapeDtypeStruct((b, n, d), F32),
                   jax.ShapeDtypeStruct((b, n, d), BF16),
                   jax.ShapeDtypeStruct((b, N_EXPERTS, n), F32)],
        compiler_params=_cparams(("parallel", "parallel")),
        name="conv_mixer",
    )(x, x, x, moe, moe, moe, gprev, mod, gmix, w_in, cw, w_out, gffn, rw)


def _moe_kernel(xs_ref, g_ref, wg_ref, wu_ref, wd_ref, y_ref, wg_s, wu_s, wd_s, *, tr, tf):
    c = xs_ref.shape[2]
    f_total = wg_s.shape[1]

    @pl.when(pl.program_id(1) == 0)
    def _():
        for src, dst in ((wg_ref, wg_s), (wu_ref, wu_s), (wd_ref, wd_s)):
            for r in range(0, dst.shape[0], tr):
                dst[r:r + tr, :] = src[0, 0, r:r + tr, :].astype(BF16)

    for r0 in range(0, c, tr):
        xs = xs_ref[0, 0, r0:r0 + tr, :]
        acc = jnp.zeros((tr, wd_s.shape[1]), F32)
        for f in range(f_total // tf):
            a = jnp.dot(xs, wg_s[:, f * tf:(f + 1) * tf], preferred_element_type=F32)
            u = jnp.dot(xs, wu_s[:, f * tf:(f + 1) * tf], preferred_element_type=F32)
            hm = (_silu(a) * u).astype(BF16)
            acc = acc + jnp.dot(hm, wd_s[f * tf:(f + 1) * tf, :], preferred_element_type=F32)
        y = acc * g_ref[0, 0, r0:r0 + tr, :]
        for k in range(SUBLANES):
            y_ref[0, 0, r0:r0 + tr, k, :] = y[:, k * LANES:(k + 1) * LANES]


def _moe_ffn(xs, g, wg, wu, wd, layer, e0):
    b, e, c, d = xs.shape
    f = wg.shape[3]
    tr = min(MOE_ROW_TILE, c)
    tf = min(512, f)
    w_spec = lambda rows_, cols: pl.BlockSpec((1, 1, rows_, cols), lambda ei, bi: (layer, e0 + ei, 0, 0))
    return pl.pallas_call(
        functools.partial(_moe_kernel, tr=tr, tf=tf),
        grid=(e, b),
        in_specs=[
            pl.BlockSpec((1, 1, c, d), lambda ei, bi: (bi, ei, 0, 0)),
            pl.BlockSpec((1, 1, c, 1), lambda ei, bi: (bi, ei, 0, 0)),
            w_spec(d, f), w_spec(d, f), w_spec(f, d),
        ],
        out_specs=pl.BlockSpec((1, 1, c, SUBLANES, LANES), lambda ei, bi: (bi, ei, 0, 0, 0)),
        out_shape=jax.ShapeDtypeStruct((b, e, c, SUBLANES, LANES), F32),
        scratch_shapes=[pltpu.VMEM((d, f), BF16), pltpu.VMEM((d, f), BF16), pltpu.VMEM((f, d), BF16)],
        compiler_params=_cparams(("arbitrary", "arbitrary")),
        name="moe_ffn",
    )(xs, g.reshape(b, e, c, 1), wg, wu, wd)


def _from_row_tiles(ref):
    return jnp.concatenate([ref[0, :, k, :] for k in range(SUBLANES)], axis=-1)


def _combine_kernel(idx_ref, *refs, unroll, zero_rows):
    y_refs, (out_hbm, acc_s, sem) = refs[:-3], refs[-3:]
    bi = pl.program_id(0)
    ei = pl.program_id(1)
    n = acc_s.shape[0]
    c = y_refs[0].shape[2]
    per_group = pl.num_programs(1) // len(y_refs)

    @pl.when(ei == 0)
    def _():
        def zero(i, carry):
            acc_s[pl.ds(pl.multiple_of(i * zero_rows, zero_rows), zero_rows)] = jnp.zeros(
                (zero_rows,) + acc_s.shape[1:], F32)
            return carry

        lax.fori_loop(0, n // zero_rows, zero, 0)

    def add_rows(y_ref):
        def rows(i, carry):
            base = i * unroll
            tok = [idx_ref[0, 0, base + u] for u in range(unroll)]
            new = [acc_s[tok[u]] + y_ref[0, 0, base + u] for u in range(unroll)]
            for u in range(unroll):
                acc_s[tok[u]] = new[u]
            return carry

        lax.fori_loop(0, c // unroll, rows, 0)

    for gi, y_ref in enumerate(y_refs):
        pl.when(ei // per_group == gi)(functools.partial(add_rows, y_ref))

    @pl.when(ei == pl.num_programs(1) - 1)
    def _():
        cp = pltpu.make_async_copy(acc_s, out_hbm.at[bi], sem)
        cp.start()
        cp.wait()


def _moe_combine(idx, ys, n):
    b, e, c = idx.shape
    per_group = e // len(ys)
    unroll = 8
    zero_rows = min(256, n)

    def y_spec(gi):
        local = lambda ei: jnp.clip(ei - gi * per_group, 0, per_group - 1)
        return pl.BlockSpec((1, 1, c, SUBLANES, LANES), lambda bi, ei: (bi, local(ei), 0, 0, 0))

    return pl.pallas_call(
        functools.partial(_combine_kernel, unroll=unroll, zero_rows=zero_rows),
        grid=(b, e),
        in_specs=[pl.BlockSpec((1, 1, c), lambda bi, ei: (bi * e + ei, 0, 0), memory_space=pltpu.SMEM)]
        + [y_spec(gi) for gi in range(len(ys))],
        out_specs=pl.BlockSpec(memory_space=pl.ANY),
        out_shape=jax.ShapeDtypeStruct((b, n, SUBLANES, LANES), F32),
        scratch_shapes=[pltpu.VMEM((n, SUBLANES, LANES), F32), pltpu.SemaphoreType.DMA],
        compiler_params=_cparams(("arbitrary", "arbitrary")),
        name="moe_combine",
    )(idx.reshape(b * e, 1, c), *ys)


def _route_kernel(aff_ref, uexcl_ref, uinclt_ref, ones_ref, lstrict_ref, idx_ref, gt_s, eq_s, need_s, *, cap):
    e_n, nch, _ = aff_ref.shape[1:]
    aff = aff_ref[0]

    def count(mask):
        return jnp.sum(jnp.sum(mask.astype(F32), axis=2, keepdims=True), axis=1, keepdims=True)

    def search(_, carry):
        lo, hi = carry
        mid = lo + ((hi - lo + 1) >> 1)
        ok = count(aff >= pltpu.bitcast(mid, F32)) >= cap
        return jnp.where(ok, mid, lo), jnp.where(ok, hi, mid - 1)

    lo0 = jnp.zeros((e_n, 1, 1), jnp.int32)
    hi0 = jnp.full((e_n, 1, 1), F32_INF_BITS, jnp.int32)
    tau_bits, _ = lax.fori_loop(0, F32_VALUE_BITS, search, (lo0, hi0))
    tau = pltpu.bitcast(tau_bits, F32)
    gt = aff > tau
    eq = aff == tau
    gt_s[...] = gt.astype(BF16)
    eq_s[...] = eq.astype(BF16)
    need_s[...] = jnp.broadcast_to(cap - count(gt), need_s.shape)

    n_slot = idx_ref.shape[3]
    slot = lax.broadcasted_iota(jnp.int32, (1, n_slot), 1).astype(F32)
    chunk_id = lax.broadcasted_iota(jnp.int32, (nch, n_slot), 0).astype(F32)
    widen = lambda a: jnp.concatenate([a] * (n_slot // LANES), axis=1)

    def chunk_prefix(mask):
        cnt = jnp.dot(mask, ones_ref[...], preferred_element_type=F32)
        return cnt, jnp.dot(lstrict_ref[...], cnt.astype(BF16), preferred_element_type=F32)

    def per_expert(e, carry):
        eq_e = eq_s[e]
        _, eq_start = chunk_prefix(eq_e)
        eq_rank = eq_start + jnp.dot(eq_e, uexcl_ref[...], preferred_element_type=F32)
        sel = gt_s[e] + jnp.where(eq_rank < need_s[e], eq_e, jnp.zeros_like(eq_e))
        cnt, start = chunk_prefix(sel)
        incl_t = lax.dot_general(uinclt_ref[...], sel, (((1,), (1,)), ((), ())),
                                 preferred_element_type=F32)
        start_w, cnt_w = widen(start), widen(cnt)
        hit = jnp.logical_and(start_w <= slot, slot < start_w + cnt_w)
        hit_f = hit.astype(F32)
        slot_start = jnp.sum(hit_f * start_w, axis=0, keepdims=True)
        slot_chunk = jnp.sum(hit_f * chunk_id, axis=0, keepdims=True)
        incl_of_slot = jnp.dot(incl_t.astype(BF16), hit.astype(BF16), preferred_element_type=F32)
        local = jnp.sum((incl_of_slot <= slot - slot_start).astype(F32), axis=0, keepdims=True)
        idx_ref[0, e] = (slot_chunk * LANES + local).astype(jnp.int32)
        return carry

    lax.fori_loop(0, e_n, per_expert, 0)


def _route(aff, cap):
    b, e, n = aff.shape
    assert n % LANES == 0 and cap % LANES == 0
    nch = n // LANES
    i = jnp.arange(LANES)
    t = jnp.arange(nch)
    uexcl = (i[:, None] < i[None, :]).astype(BF16)
    uinclt = (i[None, :] <= i[:, None]).astype(BF16)
    lstrict = (t[None, :] < t[:, None]).astype(BF16)
    const = lambda shape: pl.BlockSpec(shape, lambda bi: (0,) * len(shape))
    mask_scratch = lambda dt: pltpu.VMEM((e, nch, LANES), dt)
    idx = pl.pallas_call(
        functools.partial(_route_kernel, cap=cap),
        grid=(b,),
        in_specs=[pl.BlockSpec((1, e, nch, LANES), lambda bi: (bi, 0, 0, 0)),
                  const((LANES, LANES)), const((LANES, LANES)), const((LANES, LANES)), const((nch, nch))],
        out_specs=pl.BlockSpec((1, e, 1, cap), lambda bi: (bi, 0, 0, 0)),
        out_shape=jax.ShapeDtypeStruct((b, e, 1, cap), jnp.int32),
        scratch_shapes=[mask_scratch(BF16), mask_scratch(BF16), mask_scratch(F32)],
        compiler_params=_cparams(("parallel",)),
        name="moe_route",
    )(aff.reshape(b, e, nch, LANES), uexcl, uinclt, jnp.ones((LANES, LANES), BF16), lstrict)
    return idx.reshape(b, e, cap)


def _ec_moe(h, aff, wg, wu, wd, layer):
    b, n, d = h.shape
    assert d == SUBLANES * LANES
    cap = EC_CAPACITY * n // N_EXPERTS
    idx = _route(aff, cap)
    g = jnp.take_along_axis(aff, idx, axis=-1)
    per_group = N_EXPERTS // MOE_EXPERT_GROUPS
    ys = []
    for e0 in range(0, N_EXPERTS, per_group):
        idx_g = idx[:, e0:e0 + per_group]
        xs = jax.vmap(lambda hb, ib: hb[ib])(h, idx_g)
        ys.append(_moe_ffn(xs, g[:, e0:e0 + per_group], wg, wu, wd, layer, e0))
    return _moe_combine(idx, ys, n)


def _residual_kernel(x_ref, m_ref, g_ref, o_ref):
    o_ref[0] = x_ref[0] + g_ref[0] * _from_row_tiles(m_ref)


def _residual(x, moe, gate, *, tm):
    b, n, d = x.shape
    tm = min(tm, n)
    tok = lambda: pl.BlockSpec((1, tm, d), lambda bi, i: (bi, i, 0))
    return pl.pallas_call(
        _residual_kernel,
        grid=(b, n // tm),
        in_specs=[tok(), pl.BlockSpec((1, tm, SUBLANES, LANES), lambda bi, i: (bi, i, 0, 0)),
                  pl.BlockSpec((1, 1, d), lambda bi, i: (bi, 0, 0))],
        out_specs=tok(),
        out_shape=jax.ShapeDtypeStruct((b, n, d), F32),
        compiler_params=_cparams(("parallel", "parallel")),
        name="moe_residual",
    )(x, moe, gate)


def _rope_tables(n):
    rows = n // GRID_W
    row = jnp.broadcast_to(jnp.arange(rows, dtype=F32)[:, None], (rows, GRID_W)).reshape(-1)
    col = jnp.broadcast_to(jnp.arange(GRID_W, dtype=F32)[None, :], (rows, GRID_W)).reshape(-1)
    inv = ROPE_THETA ** (-jnp.arange(ROPE_FREQS, dtype=F32) / ROPE_FREQS)
    ar, ac = row[:, None] * inv, col[:, None] * inv
    cos = jnp.concatenate([jnp.cos(ar), jnp.cos(ar), jnp.cos(ac), jnp.cos(ac)], axis=-1)
    sin = jnp.concatenate([-jnp.sin(ar), jnp.sin(ar), -jnp.sin(ac), jnp.sin(ac)], axis=-1)
    return jnp.tile(cos, (1, 2)), jnp.tile(sin, (1, 2))


def _swap_perm():
    f = ROPE_FREQS
    base = jnp.arange(QK_ROPE)
    return jnp.where((base // f) % 2 == 0, base + f, base - f)


def _mla_weights(w_in, q_norm, w_qb, kv_norm, w_kvb, q_gain, k_gain):
    perm = _swap_perm()
    pe = w_in[:, Q_LORA + KV_LORA:]
    pe_sw = pe[:, perm]
    w_in_x = jnp.concatenate([w_in[:, :Q_LORA + KV_LORA], pe, pe, pe_sw, pe_sw], axis=1)
    wq = w_qb.reshape(Q_LORA, MLA_HEADS, QK_DIM)
    wq_rope = wq[:, :, QK_NOPE:]
    w_qb_x = jnp.concatenate([
        wq[:, :, :QK_NOPE].reshape(Q_LORA, -1),
        wq_rope.reshape(Q_LORA, -1),
        wq_rope[:, :, perm].reshape(Q_LORA, -1)], axis=1)
    wkv = w_kvb.reshape(KV_LORA, MLA_HEADS, QK_NOPE + V_DIM)
    w_kvb_x = jnp.concatenate([wkv[:, :, :QK_NOPE].reshape(KV_LORA, -1),
                               wkv[:, :, QK_NOPE:].reshape(KV_LORA, -1)], axis=1)

    def gains(g):
        gr = g[QK_NOPE:]
        return jnp.stack([g[:QK_NOPE], jnp.tile(gr, 2), jnp.tile(gr[perm], 2)])

    return {
        "w_in": w_in_x.astype(BF16), "q_norm": q_norm[None, :], "w_qb": w_qb_x.astype(BF16),
        "kv_norm": kv_norm[None, :], "w_kvb": w_kvb_x.astype(BF16),
        "gq": gains(q_gain), "gk": gains(k_gain),
    }


def _pad_router(rw):
    hi = rw.astype(BF16)
    lo = (rw - hi.astype(F32)).astype(BF16)
    pad = ((0, 0), (0, LANES - rw.shape[1]))
    return jnp.concatenate([jnp.pad(hi, pad), jnp.pad(lo, pad)], axis=1)


def kernel(x, c, ctx, c_ctx, norm_mix, norm_ffn, ada_w, ada_b, mla_w_in, mla_q_norm, mla_w_qb, mla_kv_norm, mla_w_kvb, mla_q_gain, mla_k_gain, mla_w_out, conv_w_in, conv_w, conv_w_out, router_w, exp_w_gate, exp_w_up, exp_w_down):
    b, n, d = x.shape
    nc = ctx.shape[1]
    depth = ada_w.shape[0]
    assert depth == 2 and b < 8

    cond = jnp.concatenate([c, c_ctx[None, :], jnp.zeros((8 - b - 1, d), F32)], axis=0)
    mod_all = _adaln(cond, ada_w, ada_b).reshape(depth, 8, 6, d)
    mod0, mod1 = mod_all[0, :b], mod_all[1, :b]
    mod0_ctx = mod_all[0, b:b + 1]

    w = _mla_weights(mla_w_in[0], mla_q_norm[0], mla_w_qb[0], mla_kv_norm[0], mla_w_kvb[0],
                     mla_q_gain[0], mla_k_gain[0])
    cos, sin = _rope_tables(n)
    gmix0 = norm_mix[0][None, :]
    qn, qr, kx, vx = _mla_pre(x, mod0, False, gmix0, w, cos, sin, with_q=True, tm=PROJ_TOKEN_TILE)
    kc, vc = _mla_pre(ctx, mod0_ctx, True, gmix0, w, jnp.ones((nc, LANES), F32),
                      jnp.zeros((nc, LANES), F32), with_q=False, tm=PROJ_TOKEN_TILE)
    ox = _attention(qn, qr, kc, vc, kx, vx, tq=ATTN_Q_TILE, tk=ATTN_KEY_TILE)
    x1, h0, aff0 = _attn_out(ox, x, mod0, mla_w_out[0].astype(BF16), norm_ffn[0][None, :],
                             _pad_router(router_w[0]), tm=PROJ_TOKEN_TILE)
    moe0 = _ec_moe(h0, aff0, exp_w_gate, exp_w_up, exp_w_down, 0)

    x3, h1, aff1 = _conv_mixer(x1, moe0, mod0[:, 5:6], mod1, norm_mix[1][None, :],
                               conv_w_in[0].astype(BF16), conv_w[0], conv_w_out[0].astype(BF16),
                               norm_ffn[1][None, :], _pad_router(router_w[1]), tm=TOKEN_TILE)
    moe1 = _ec_moe(h1, aff1, exp_w_gate, exp_w_up, exp_w_down, 1)
    return _residual(x3, moe1, mod1[:, 5:6], tm=TOKEN_TILE)
```

```python
import functools

import jax
import jax.numpy as jnp
from jax import lax
from jax.experimental import pallas as pl
from jax.experimental.pallas import tpu as pltpu

F32 = jnp.float32
BF16 = jnp.bfloat16
HIGHEST = lax.Precision.HIGHEST

GRID_W = 64
N_MIXERS = 2
MLA_HEADS = 8
QK_NOPE = 128
QK_ROPE = 64
QK_DIM = QK_NOPE + QK_ROPE
V_DIM = 128
Q_LORA = 384
KV_LORA = 256
ROPE_FREQS = QK_ROPE // 4
ROPE_THETA = 10000.0
ATTN_SCALE = QK_DIM ** -0.5
LOG2_E = 1.4426950408889634
N_EXPERTS = 16
EC_CAPACITY = 2
EPS = 1e-6

LANES = 128
SUBLANES = 8
F32_INF_BITS = 0x7F800000
F32_VALUE_BITS = 31
VMEM_LIMIT = 56 * 1024 * 1024
MAX_UNROLLED_KEY_BLOCKS = 16
TOKEN_TILE = 1024
PROJ_TOKEN_TILE = 1024
MOE_ROW_TILE = 512
MOE_EXPERT_GROUPS = 2
ATTN_Q_TILE = 1024
ATTN_KEY_TILE = 1024
SOFTMAX_ROWS = 32


def _cparams(sem):
    return pltpu.CompilerParams(dimension_semantics=sem, vmem_limit_bytes=VMEM_LIMIT)


def _rms(x):
    return x * lax.rsqrt(jnp.mean(x * x, axis=-1, keepdims=True) + EPS)


def _modulate(x, g, shift, scale):
    return (_rms(x) * g) * (1.0 + scale) + shift


def _silu(a):
    return a * jax.nn.sigmoid(a)


def _adaln_kernel(c_ref, w_ref, b_ref, o_ref):
    s = _silu(c_ref[...])
    o_ref[0] = jnp.dot(s, w_ref[0], precision=HIGHEST, preferred_element_type=F32) + b_ref[0]


def _adaln(cond, ada_w, ada_b):
    depth, d, d6 = ada_w.shape
    tn = 1536
    return pl.pallas_call(
        _adaln_kernel,
        grid=(depth, d6 // tn),
        in_specs=[
            pl.BlockSpec((8, d), lambda l, j: (0, 0)),
            pl.BlockSpec((1, d, tn), lambda l, j: (l, 0, j)),
            pl.BlockSpec((1, 1, tn), lambda l, j: (l, 0, j)),
        ],
        out_specs=pl.BlockSpec((1, 8, tn), lambda l, j: (l, 0, j)),
        out_shape=jax.ShapeDtypeStruct((depth, 8, d6), F32),
        compiler_params=_cparams(("parallel", "parallel")),
        name="adaln",
    )(cond, ada_w, ada_b.reshape(depth, 1, d6))


def _mla_pre_kernel(x_ref, mod_ref, gmix_ref, win_ref, wpet_ref, qnorm_ref, wqb_ref, kvnorm_ref, wkbt_ref,
                    wv_ref, gq_ref, gkt_ref, cos_ref, sin_ref, cost_ref, sint_ref, *out_refs, with_q):
    if with_q:
        qn_ref, qr_ref, kt_ref, v_ref = out_refs
    else:
        kt_ref, v_ref = out_refs
    mod = mod_ref[0]
    hx = _modulate(x_ref[0], gmix_ref[...], mod[0:1], mod[1:2]).astype(BF16)
    lat = jnp.dot(hx, win_ref[...], preferred_element_type=F32)
    cos = cos_ref[...]
    sin = sin_ref[...]
    lo = lax.broadcasted_iota(jnp.int32, (1, LANES), 1) < QK_ROPE

    def half_sums(v):
        v2 = v * v
        return (jnp.sum(jnp.where(lo, v2, 0.0), axis=-1, keepdims=True),
                jnp.sum(jnp.where(lo, 0.0, v2), axis=-1, keepdims=True))

    if with_q:
        gq = gq_ref[...]
        qn_in = (_rms(lat[:, :Q_LORA]) * qnorm_ref[...]).astype(BF16)
        qf = jnp.dot(qn_in, wqb_ref[...], preferred_element_type=F32)
        nope_w = MLA_HEADS * QK_NOPE
        pair_w = (MLA_HEADS // 2) * LANES
        for p in range(MLA_HEADS // 2):
            rp = qf[:, nope_w + p * LANES: nope_w + (p + 1) * LANES]
            sw = qf[:, nope_w + pair_w + p * LANES: nope_w + pair_w + (p + 1) * LANES]
            s_pair = half_sums(rp)
            r_pair = []
            for hh in range(2):
                h = 2 * p + hh
                nope = qf[:, h * QK_NOPE:(h + 1) * QK_NOPE]
                ms = (jnp.sum(nope * nope, axis=-1, keepdims=True) + s_pair[hh]) * (1.0 / QK_DIM)
                r = lax.rsqrt(ms + EPS) * (ATTN_SCALE * LOG2_E)
                r_pair.append(r)
                qn_ref[0, h] = ((nope * r) * gq[0:1]).astype(BF16)
            roped = (rp * gq[1:2]) * cos + (sw * gq[2:3]) * sin
            qr_ref[0, p] = (roped * jnp.where(lo, r_pair[0], r_pair[1])).astype(BF16)

    nt = (((1,), (1,)), ((), ()))
    gkt = gkt_ref[...]
    kvn_in = (_rms(lat[:, Q_LORA:Q_LORA + KV_LORA]) * kvnorm_ref[...]).astype(BF16)
    v = jnp.dot(kvn_in, wv_ref[...], preferred_element_type=F32)
    knt = lax.dot_general(wkbt_ref[...], kvn_in, nt, preferred_element_type=F32)
    pet = lax.dot_general(wpet_ref[...], hx, nt, preferred_element_type=F32)
    kr_t, ks_t = pet[0:LANES], pet[LANES:2 * LANES]
    s_pe = jnp.sum(kr_t[0:QK_ROPE] * kr_t[0:QK_ROPE], axis=0, keepdims=True)
    k_roped_t = (kr_t * gkt[:, 1:2]) * cost_ref[...] + (ks_t * gkt[:, 2:3]) * sint_ref[...]
    row_lo = lax.broadcasted_iota(jnp.int32, (LANES, 1), 0) < QK_ROPE
    for h in range(MLA_HEADS):
        nope_t = knt[h * QK_NOPE:(h + 1) * QK_NOPE]
        ms = (jnp.sum(nope_t * nope_t, axis=0, keepdims=True) + s_pe) * (1.0 / QK_DIM)
        r = lax.rsqrt(ms + EPS)
        kt_ref[0, h, 0:QK_NOPE, :] = ((nope_t * r) * gkt[:, 0:1]).astype(BF16)
        keep = row_lo if h % 2 == 0 else jnp.logical_not(row_lo)
        kt_ref[0, h, QK_NOPE:QK_NOPE + LANES, :] = jnp.where(keep, k_roped_t * r, 0.0).astype(BF16)
        v_ref[0, h, :, 0:V_DIM] = v[:, h * V_DIM:(h + 1) * V_DIM].astype(BF16)
        v_ref[0, h, :, V_DIM:2 * V_DIM] = jnp.ones((v.shape[0], V_DIM), BF16)


def _mla_pre(x, mod, shared_mod, gmix, w, cos, sin, *, with_q, tm):
    b, n, d = x.shape
    tm = min(tm, n)
    const = lambda shape: pl.BlockSpec(shape, lambda bi, i: (0,) * len(shape))
    mod_map = (lambda bi, i: (0, 0, 0)) if shared_mod else (lambda bi, i: (bi, 0, 0))
    in_specs = [
        pl.BlockSpec((1, tm, d), lambda bi, i: (bi, i, 0)),
        pl.BlockSpec((1, 6, d), mod_map),
        const((1, d)),
        const(w["w_in"].shape),
        const(w["w_pet"].shape),
        const((1, Q_LORA)),
        const(w["w_qb"].shape),
        const((1, KV_LORA)),
        const(w["w_kbt"].shape),
        const(w["w_v"].shape),
        const((3, LANES)),
        const((LANES, 3)),
        pl.BlockSpec((tm, LANES), lambda bi, i: (i, 0)),
        pl.BlockSpec((tm, LANES), lambda bi, i: (i, 0)),
        pl.BlockSpec((LANES, tm), lambda bi, i: (0, i)),
        pl.BlockSpec((LANES, tm), lambda bi, i: (0, i)),
    ]
    head_spec = lambda nh, w_: pl.BlockSpec((1, nh, tm, w_), lambda bi, i: (bi, 0, i, 0))
    out_specs = [pl.BlockSpec((1, MLA_HEADS, 2 * LANES, tm), lambda bi, i: (bi, 0, 0, i)),
                 head_spec(MLA_HEADS, 2 * V_DIM)]
    out_shape = [jax.ShapeDtypeStruct((b, MLA_HEADS, 2 * LANES, n), BF16),
                 jax.ShapeDtypeStruct((b, MLA_HEADS, n, 2 * V_DIM), BF16)]
    if with_q:
        out_specs = [head_spec(MLA_HEADS, QK_NOPE), head_spec(MLA_HEADS // 2, LANES)] + out_specs
        out_shape = [jax.ShapeDtypeStruct((b, MLA_HEADS, n, QK_NOPE), BF16),
                     jax.ShapeDtypeStruct((b, MLA_HEADS // 2, n, LANES), BF16)] + out_shape
    return pl.pallas_call(
        functools.partial(_mla_pre_kernel, with_q=with_q),
        grid=(b, n // tm),
        in_specs=in_specs,
        out_specs=out_specs,
        out_shape=out_shape,
        compiler_params=_cparams(("parallel", "parallel")),
        name="mla_pre_q" if with_q else "mla_pre_ctx",
    )(x, mod, gmix, w["w_in"], w["w_pet"], w["q_norm"], w["w_qb"], w["kv_norm"], w["w_kbt"], w["w_v"],
      w["gq"], w["gkt"], cos, sin, cos.T, sin.T)


def _attn_kernel(qn_ref, qr_ref, kc_ref, vc_ref, kx_ref, vx_ref, o_ref,
                 m_s, acc_s, sc_s, pc_s, s0_s, s1_s, p0_s, p1_s, a0_s, a1_s, *, tk):
    q = jnp.concatenate([qn_ref[0, 0], qr_ref[0, 0]], axis=-1)
    nblk = 1 + vx_ref.shape[2] // tk
    s_buf = lambda i: sc_s if i == 0 else (s0_s, s1_s)[i % 2]
    p_buf = lambda i: pc_s if i == 0 else (p0_s, p1_s)[i % 2]
    a_buf = lambda i: (a0_s, a1_s)[i % 2]
    keys_t = lambda i: kc_ref[0, 0] if i == 0 else kx_ref[0, 0, :, (i - 1) * tk:i * tk]
    vals = lambda i: vc_ref[0, 0] if i == 0 else vx_ref[0, 0, (i - 1) * tk:i * tk, :]

    def scores(i):
        s_buf(i)[...] = jnp.dot(q, keys_t(i), preferred_element_type=F32)

    def softmax(i):
        for r in range(0, q.shape[0], SOFTMAX_ROWS):
            rows = slice(r, r + SOFTMAX_ROWS)
            s = s_buf(i)[rows, :]
            m_prev = m_s[rows, :]
            m_new = jnp.maximum(m_prev, jnp.max(s, axis=-1, keepdims=True))
            p_buf(i)[rows, :] = jnp.exp2(s - m_new).astype(BF16)
            a_buf(i)[rows, :] = jnp.exp2(m_prev - m_new)
            m_s[rows, :] = m_new

    def accumulate(i):
        acc_s[...] = a_buf(i)[...] * acc_s[...] + jnp.dot(p_buf(i)[...], vals(i), preferred_element_type=F32)

    m_s[...] = jnp.full(m_s.shape, -jnp.inf, F32)
    acc_s[...] = jnp.zeros(acc_s.shape, F32)
    scores(0)
    scores(1)
    softmax(0)
    for i in range(nblk):
        if i + 2 < nblk:
            scores(i + 2)
        if i + 1 < nblk:
            softmax(i + 1)
        accumulate(i)
    acc = acc_s[...]
    o_ref[0] = (acc[:, :V_DIM] / acc[:, V_DIM:]).astype(o_ref.dtype)


def _attention(qn, qr, kc, vc, kx, vx, *, tq, tk):
    b, h, n, _ = qn.shape
    nc = vc.shape[2]
    tq = min(tq, n)
    tk = min(tk, n)
    assert n // tk <= MAX_UNROLLED_KEY_BLOCKS
    kt_spec = lambda cols: pl.BlockSpec((1, 1, 2 * LANES, cols), lambda bi, hi, i: (bi, hi, 0, 0))
    v_spec = lambda rows: pl.BlockSpec((1, 1, rows, 2 * V_DIM), lambda bi, hi, i: (bi, hi, 0, 0))
    return pl.pallas_call(
        functools.partial(_attn_kernel, tk=tk),
        grid=(b, h, n // tq),
        in_specs=[
            pl.BlockSpec((1, 1, tq, QK_NOPE), lambda bi, hi, i: (bi, hi, i, 0)),
            pl.BlockSpec((1, 1, tq, LANES), lambda bi, hi, i: (bi, hi // 2, i, 0)),
            kt_spec(nc), v_spec(nc), kt_spec(n), v_spec(n),
        ],
        out_specs=pl.BlockSpec((1, tq, V_DIM), lambda bi, hi, i: (bi, i, hi)),
        out_shape=jax.ShapeDtypeStruct((b, n, h * V_DIM), BF16),
        scratch_shapes=[
            pltpu.VMEM((tq, 1), F32), pltpu.VMEM((tq, 2 * V_DIM), F32),
            pltpu.VMEM((tq, nc), F32), pltpu.VMEM((tq, nc), BF16),
            pltpu.VMEM((tq, tk), F32), pltpu.VMEM((tq, tk), F32),
            pltpu.VMEM((tq, tk), BF16), pltpu.VMEM((tq, tk), BF16),
            pltpu.VMEM((tq, 1), F32), pltpu.VMEM((tq, 1), F32),
        ],
        compiler_params=_cparams(("parallel", "parallel", "arbitrary")),
        name="flash_attn",
    )(qn, qr, kc, vc, kx, vx)


def _ffn_pre(x_new, mod, gffn_ref, rw_ref, h_ref, aff_ref):
    h2 = _modulate(x_new, gffn_ref[...], mod[3:4], mod[4:5])
    hi = h2.astype(BF16)
    lo = (h2 - hi.astype(F32)).astype(BF16)
    h_ref[0] = hi
    rw = rw_ref[...]
    t = jnp.dot(hi, rw, preferred_element_type=F32) + jnp.dot(lo, rw, preferred_element_type=F32)
    logits = t[:, :LANES] + t[:, LANES:]
    is_expert = lax.broadcasted_iota(jnp.int32, (1, LANES), 1) < N_EXPERTS
    logits = jnp.where(is_expert, logits, -jnp.inf)
    e = jnp.exp(logits - jnp.max(logits, axis=-1, keepdims=True))
    aff = e / jnp.sum(e, axis=-1, keepdims=True)
    aff_ref[0] = aff.T[:N_EXPERTS]


def _attn_out_kernel(o_ref, x_ref, mod_ref, wout_ref, gffn_ref, rw_ref, x1_ref, h_ref, aff_ref):
    mod = mod_ref[0]
    o = jnp.dot(o_ref[0], wout_ref[...], preferred_element_type=F32)
    x1 = x_ref[0] + mod[2:3] * o
    x1_ref[0] = x1
    _ffn_pre(x1, mod, gffn_ref, rw_ref, h_ref, aff_ref)


def _attn_out(ox, x, mod, w_out, gffn, rw, *, tm):
    b, n, d = x.shape
    tm = min(tm, n)
    const = lambda shape: pl.BlockSpec(shape, lambda bi, i: (0,) * len(shape))
    tok = lambda: pl.BlockSpec((1, tm, d), lambda bi, i: (bi, i, 0))
    return pl.pallas_call(
        _attn_out_kernel,
        grid=(b, n // tm),
        in_specs=[tok(), tok(), pl.BlockSpec((1, 6, d), lambda bi, i: (bi, 0, 0)),
                  const(w_out.shape), const((1, d)), const(rw.shape)],
        out_specs=[tok(), tok(), pl.BlockSpec((1, N_EXPERTS, tm), lambda bi, i: (bi, 0, i))],
        out_shape=[jax.ShapeDtypeStruct((b, n, d), F32),
                   jax.ShapeDtypeStruct((b, n, d), BF16),
                   jax.ShapeDtypeStruct((b, N_EXPERTS, n), F32)],
        compiler_params=_cparams(("parallel", "parallel")),
        name="attn_out",
    )(ox, x, mod, w_out, gffn, rw)


def _conv_kernel(xm_ref, xp_ref, xn_ref, mm_ref, mp_ref, mn_ref, gprev_ref, mod_ref, gmix_ref,
                 win_ref, cw_ref, wout_ref, gffn_ref, rw_ref, x3_ref, h_ref, aff_ref):
    i = pl.program_id(1)
    tm, d = xm_ref.shape[1], xm_ref.shape[2]
    halo = xp_ref.shape[1]
    gprev = gprev_ref[0]
    mod = mod_ref[0]
    xm = xm_ref[0] + gprev * _from_row_tiles(mm_ref)
    xe = jnp.concatenate([xp_ref[0] + gprev * _from_row_tiles(mp_ref), xm,
                          xn_ref[0] + gprev * _from_row_tiles(mn_ref)], axis=0)
    hx = _modulate(xe, gmix_ref[...], mod[0:1], mod[1:2]).astype(BF16)
    proj = jnp.dot(hx, win_ref[...], preferred_element_type=F32)
    u = proj[:, d:2 * d] * proj[:, 2 * d:3 * d]
    row = lax.broadcasted_iota(jnp.int32, (tm + 2 * halo, 1), 0)
    outside = jnp.logical_or(jnp.logical_and(i == 0, row < halo),
                             jnp.logical_and(i == pl.num_programs(1) - 1, row >= tm + halo))
    u = jnp.where(outside, 0.0, u)
    rows = tm + 2 * halo
    u_prev = pltpu.roll(u, 1, axis=0)[halo:halo + tm]
    u_next = pltpu.roll(u, rows - 1, axis=0)[halo:halo + tm]
    cw = cw_ref[...]
    y = cw[0:1] * u_prev + cw[1:2] * u[halo:halo + tm] + cw[2:3] * u_next
    z = (proj[halo:halo + tm, 0:d] * y).astype(BF16)
    x3 = xm + mod[2:3] * jnp.dot(z, wout_ref[...], preferred_element_type=F32)
    x3_ref[0] = x3
    _ffn_pre(x3, mod, gffn_ref, rw_ref, h_ref, aff_ref)


def _conv_mixer(x, moe, gprev, mod, gmix, w_in, cw, w_out, gffn, rw, *, tm):
    b, n, d = x.shape
    tm = min(tm, n)
    halo = 8
    nb = tm // halo
    last = n // halo - 1
    const = lambda shape: pl.BlockSpec(shape, lambda bi, i: (0,) * len(shape))
    tok = lambda: pl.BlockSpec((1, tm, d), lambda bi, i: (bi, i, 0))
    prev = lambda: pl.BlockSpec((1, halo, d), lambda bi, i: (bi, jnp.maximum(i * nb - 1, 0), 0))
    nxt = lambda: pl.BlockSpec((1, halo, d), lambda bi, i: (bi, jnp.minimum((i + 1) * nb, last), 0))
    rt = (SUBLANES, LANES)
    tok_rt = pl.BlockSpec((1, tm) + rt, lambda bi, i: (bi, i, 0, 0))
    prev_rt = pl.BlockSpec((1, halo) + rt, lambda bi, i: (bi, jnp.maximum(i * nb - 1, 0), 0, 0))
    nxt_rt = pl.BlockSpec((1, halo) + rt, lambda bi, i: (bi, jnp.minimum((i + 1) * nb, last), 0, 0))
    return pl.pallas_call(
        _conv_kernel,
        grid=(b, n // tm),
        in_specs=[tok(), prev(), nxt(), tok_rt, prev_rt, nxt_rt,
                  pl.BlockSpec((1, 1, d), lambda bi, i: (bi, 0, 0)),
                  pl.BlockSpec((1, 6, d), lambda bi, i: (bi, 0, 0)),
                  const((1, d)), const(w_in.shape), const(cw.shape), const(w_out.shape),
                  const((1, d)), const(rw.shape)],
        out_specs=[tok(), tok(), pl.BlockSpec((1, N_EXPERTS, tm), lambda bi, i: (bi, 0, i))],
        out_shape=[jax.ShapeDtypeStruct((b, n, d), F32),
                   jax.ShapeDtypeStruct((b, n, d), BF16),
                   jax.ShapeDtypeStruct((b, N_EXPERTS, n), F32)],
        compiler_params=_cparams(("parallel", "parallel")),
        name="conv_mixer",
    )(x, x, x, moe, moe, moe, gprev, mod, gmix, w_in, cw, w_out, gffn, rw)


def _moe_kernel(xs_ref, g_ref, wg_ref, wu_ref, wd_ref, y_ref, wg_s, wu_s, wd_s, *, tr, tf):
    c = xs_ref.shape[2]
    f_total = wg_s.shape[1]

    @pl.when(pl.program_id(1) == 0)
    def _():
        for src, dst in ((wg_ref, wg_s), (wu_ref, wu_s), (wd_ref, wd_s)):
            for r in range(0, dst.shape[0], tr):
                dst[r:r + tr, :] = src[0, 0, r:r + tr, :].astype(BF16)

    for r0 in range(0, c, tr):
        xs = xs_ref[0, 0, r0:r0 + tr, :]
        acc = jnp.zeros((tr, wd_s.shape[1]), F32)
        for f in range(f_total // tf):
            a = jnp.dot(xs, wg_s[:, f * tf:(f + 1) * tf], preferred_element_type=F32)
            u = jnp.dot(xs, wu_s[:, f * tf:(f + 1) * tf], preferred_element_type=F32)
            hm = (_silu(a) * u).astype(BF16)
            acc = acc + jnp.dot(hm, wd_s[f * tf:(f + 1) * tf, :], preferred_element_type=F32)
        y = acc * g_ref[0, 0, r0:r0 + tr, :]
        for k in range(SUBLANES):
            y_ref[0, 0, r0:r0 + tr, k, :] = y[:, k * LANES:(k + 1) * LANES]


def _moe_ffn(xs, g, wg, wu, wd, layer, e0):
    b, e, c, d = xs.shape
    f = wg.shape[3]
    tr = min(MOE_ROW_TILE, c)
    tf = min(512, f)
    w_spec = lambda rows_, cols: pl.BlockSpec((1, 1, rows_, cols), lambda ei, bi: (layer, e0 + ei, 0, 0))
    return pl.pallas_call(
        functools.partial(_moe_kernel, tr=tr, tf=tf),
        grid=(e, b),
        in_specs=[
            pl.BlockSpec((1, 1, c, d), lambda ei, bi: (bi, ei, 0, 0)),
            pl.BlockSpec((1, 1, c, 1), lambda ei, bi: (bi, ei, 0, 0)),
            w_spec(d, f), w_spec(d, f), w_spec(f, d),
        ],
        out_specs=pl.BlockSpec((1, 1, c, SUBLANES, LANES), lambda ei, bi: (bi, ei, 0, 0, 0)),
        out_shape=jax.ShapeDtypeStruct((b, e, c, SUBLANES, LANES), F32),
        scratch_shapes=[pltpu.VMEM((d, f), BF16), pltpu.VMEM((d, f), BF16), pltpu.VMEM((f, d), BF16)],
        compiler_params=_cparams(("arbitrary", "arbitrary")),
        name="moe_ffn",
    )(xs, g.reshape(b, e, c, 1), wg, wu, wd)


def _from_row_tiles(ref):
    return jnp.concatenate([ref[0, :, k, :] for k in range(SUBLANES)], axis=-1)


def _combine_kernel(idx_ref, *refs, unroll, zero_rows):
    y_refs, (out_hbm, acc_s, sem) = refs[:-3], refs[-3:]
    bi = pl.program_id(0)
    ei = pl.program_id(1)
    n = acc_s.shape[0]
    c = y_refs[0].shape[2]
    per_group = pl.num_programs(1) // len(y_refs)

    @pl.when(ei == 0)
    def _():
        def zero(i, carry):
            acc_s[pl.ds(pl.multiple_of(i * zero_rows, zero_rows), zero_rows)] = jnp.zeros(
                (zero_rows,) + acc_s.shape[1:], F32)
            return carry

        lax.fori_loop(0, n // zero_rows, zero, 0)

    def add_rows(y_ref):
        def rows(i, carry):
            base = i * unroll
            tok = [idx_ref[0, 0, base + u] for u in range(unroll)]
            new = [acc_s[tok[u]] + y_ref[0, 0, base + u] for u in range(unroll)]
            for u in range(unroll):
                acc_s[tok[u]] = new[u]
            return carry

        lax.fori_loop(0, c // unroll, rows, 0)

    for gi, y_ref in enumerate(y_refs):
        pl.when(ei // per_group == gi)(functools.partial(add_rows, y_ref))

    @pl.when(ei == pl.num_programs(1) - 1)
    def _():
        cp = pltpu.make_async_copy(acc_s, out_hbm.at[bi], sem)
        cp.start()
        cp.wait()


def _moe_combine(idx, ys, n):
    b, e, c = idx.shape
    per_group = e // len(ys)
    unroll = 8
    zero_rows = min(256, n)

    def y_spec(gi):
        local = lambda ei: jnp.clip(ei - gi * per_group, 0, per_group - 1)
        return pl.BlockSpec((1, 1, c, SUBLANES, LANES), lambda bi, ei: (bi, local(ei), 0, 0, 0))

    return pl.pallas_call(
        functools.partial(_combine_kernel, unroll=unroll, zero_rows=zero_rows),
        grid=(b, e),
        in_specs=[pl.BlockSpec((1, 1, c), lambda bi, ei: (bi * e + ei, 0, 0), memory_space=pltpu.SMEM)]
        + [y_spec(gi) for gi in range(len(ys))],
        out_specs=pl.BlockSpec(memory_space=pl.ANY),
        out_shape=jax.ShapeDtypeStruct((b, n, SUBLANES, LANES), F32),
        scratch_shapes=[pltpu.VMEM((n, SUBLANES, LANES), F32), pltpu.SemaphoreType.DMA],
        compiler_params=_cparams(("arbitrary", "arbitrary")),
        name="moe_combine",
    )(idx.reshape(b * e, 1, c), *ys)


def _route_kernel(aff_ref, uexcl_ref, uinclt_ref, ones_ref, lstrict_ref, idx_ref, gt_s, eq_s, need_s, *, cap):
    e_n, nch, _ = aff_ref.shape[1:]
    aff = aff_ref[0]

    def count(mask):
        return jnp.sum(jnp.sum(mask.astype(F32), axis=2, keepdims=True), axis=1, keepdims=True)

    def search(_, carry):
        lo, hi = carry
        mid = lo + ((hi - lo + 1) >> 1)
        ok = count(aff >= pltpu.bitcast(mid, F32)) >= cap
        return jnp.where(ok, mid, lo), jnp.where(ok, hi, mid - 1)

    lo0 = jnp.zeros((e_n, 1, 1), jnp.int32)
    hi0 = jnp.full((e_n, 1, 1), F32_INF_BITS, jnp.int32)
    tau_bits, _ = lax.fori_loop(0, F32_VALUE_BITS, search, (lo0, hi0))
    tau = pltpu.bitcast(tau_bits, F32)
    gt = aff > tau
    eq = aff == tau
    gt_s[...] = gt.astype(BF16)
    eq_s[...] = eq.astype(BF16)
    need_s[...] = jnp.broadcast_to(cap - count(gt), need_s.shape)

    n_slot = idx_ref.shape[3]
    slot = lax.broadcasted_iota(jnp.int32, (1, n_slot), 1).astype(F32)
    chunk_id = lax.broadcasted_iota(jnp.int32, (nch, n_slot), 0).astype(F32)
    widen = lambda a: jnp.concatenate([a] * (n_slot // LANES), axis=1)

    def chunk_prefix(mask):
        cnt = jnp.dot(mask, ones_ref[...], preferred_element_type=F32)
        return cnt, jnp.dot(lstrict_ref[...], cnt.astype(BF16), preferred_element_type=F32)

    def per_expert(e, carry):
        eq_e = eq_s[e]
        _, eq_start = chunk_prefix(eq_e)
        eq_rank = eq_start + jnp.dot(eq_e, uexcl_ref[...], preferred_element_type=F32)
        sel = gt_s[e] + jnp.where(eq_rank < need_s[e], eq_e, jnp.zeros_like(eq_e))
        cnt, start = chunk_prefix(sel)
        incl_t = lax.dot_general(uinclt_ref[...], sel, (((1,), (1,)), ((), ())),
                                 preferred_element_type=F32)
        start_w, cnt_w = widen(start), widen(cnt)
        hit = jnp.logical_and(start_w <= slot, slot < start_w + cnt_w)
        hit_f = hit.astype(F32)
        slot_start = jnp.sum(hit_f * start_w, axis=0, keepdims=True)
        slot_chunk = jnp.sum(hit_f * chunk_id, axis=0, keepdims=True)
        incl_of_slot = jnp.dot(incl_t.astype(BF16), hit.astype(BF16), preferred_element_type=F32)
        local = jnp.sum((incl_of_slot <= slot - slot_start).astype(F32), axis=0, keepdims=True)
        idx_ref[0, e] = (slot_chunk * LANES + local).astype(jnp.int32)
        return carry

    lax.fori_loop(0, e_n, per_expert, 0)


def _route(aff, cap):
    b, e, n = aff.shape
    assert n % LANES == 0 and cap % LANES == 0
    nch = n // LANES
    i = jnp.arange(LANES)
    t = jnp.arange(nch)
    uexcl = (i[:, None] < i[None, :]).astype(BF16)
    uinclt = (i[None, :] <= i[:, None]).astype(BF16)
    lstrict = (t[None, :] < t[:, None]).astype(BF16)
    const = lambda shape: pl.BlockSpec(shape, lambda bi: (0,) * len(shape))
    mask_scratch = lambda dt: pltpu.VMEM((e, nch, LANES), dt)
    idx = pl.pallas_call(
        functools.partial(_route_kernel, cap=cap),
        grid=(b,),
        in_specs=[pl.BlockSpec((1, e, nch, LANES), lambda bi: (bi, 0, 0, 0)),
                  const((LANES, LANES)), const((LANES, LANES)), const((LANES, LANES)), const((nch, nch))],
        out_specs=pl.BlockSpec((1, e, 1, cap), lambda bi: (bi, 0, 0, 0)),
        out_shape=jax.ShapeDtypeStruct((b, e, 1, cap), jnp.int32),
        scratch_shapes=[mask_scratch(BF16), mask_scratch(BF16), mask_scratch(F32)],
        compiler_params=_cparams(("parallel",)),
        name="moe_route",
    )(aff.reshape(b, e, nch, LANES), uexcl, uinclt, jnp.ones((LANES, LANES), BF16), lstrict)
    return idx.reshape(b, e, cap)


def _ec_moe(h, aff, wg, wu, wd, layer):
    b, n, d = h.shape
    assert d == SUBLANES * LANES
    cap = EC_CAPACITY * n // N_EXPERTS
    idx = _route(aff, cap)
    g = jnp.take_along_axis(aff, idx, axis=-1)
    per_group = N_EXPERTS // MOE_EXPERT_GROUPS
    ys = []
    for e0 in range(0, N_EXPERTS, per_group):
        idx_g = idx[:, e0:e0 + per_group]
        xs = jax.vmap(lambda hb, ib: hb[ib])(h, idx_g)
        ys.append(_moe_ffn(xs, g[:, e0:e0 + per_group], wg, wu, wd, layer, e0))
    return _moe_combine(idx, ys, n)


def _residual_kernel(x_ref, m_ref, g_ref, o_ref):
    o_ref[0] = x_ref[0] + g_ref[0] * _from_row_tiles(m_ref)


def _residual(x, moe, gate, *, tm):
    b, n, d = x.shape
    tm = min(tm, n)
    tok = lambda: pl.BlockSpec((1, tm, d), lambda bi, i: (bi, i, 0))
    return pl.pallas_call(
        _residual_kernel,
        grid=(b, n // tm),
        in_specs=[tok(), pl.BlockSpec((1, tm, SUBLANES, LANES), lambda bi, i: (bi, i, 0, 0)),
                  pl.BlockSpec((1, 1, d), lambda bi, i: (bi, 0, 0))],
        out_specs=tok(),
        out_shape=jax.ShapeDtypeStruct((b, n, d), F32),
        compiler_params=_cparams(("parallel", "parallel")),
        name="moe_residual",
    )(x, moe, gate)


def _rope_tables(n):
    rows = n // GRID_W
    row = jnp.broadcast_to(jnp.arange(rows, dtype=F32)[:, None], (rows, GRID_W)).reshape(-1)
    col = jnp.broadcast_to(jnp.arange(GRID_W, dtype=F32)[None, :], (rows, GRID_W)).reshape(-1)
    inv = ROPE_THETA ** (-jnp.arange(ROPE_FREQS, dtype=F32) / ROPE_FREQS)
    ar, ac = row[:, None] * inv, col[:, None] * inv
    cos = jnp.concatenate([jnp.cos(ar), jnp.cos(ar), jnp.cos(ac), jnp.cos(ac)], axis=-1)
    sin = jnp.concatenate([-jnp.sin(ar), jnp.sin(ar), -jnp.sin(ac), jnp.sin(ac)], axis=-1)
    return jnp.tile(cos, (1, 2)), jnp.tile(sin, (1, 2))


def _swap_perm():
    f = ROPE_FREQS
    base = jnp.arange(QK_ROPE)
    return jnp.where((base // f) % 2 == 0, base + f, base - f)


def _mla_weights(w_in, q_norm, w_qb, kv_norm, w_kvb, q_gain, k_gain):
    perm = _swap_perm()
    pe = w_in[:, Q_LORA + KV_LORA:]
    pe_sw = pe[:, perm]
    w_pet = jnp.concatenate([pe, pe, pe_sw, pe_sw], axis=1).T
    wq = w_qb.reshape(Q_LORA, MLA_HEADS, QK_DIM)
    wq_rope = wq[:, :, QK_NOPE:]
    w_qb_x = jnp.concatenate([
        wq[:, :, :QK_NOPE].reshape(Q_LORA, -1),
        wq_rope.reshape(Q_LORA, -1),
        wq_rope[:, :, perm].reshape(Q_LORA, -1)], axis=1)
    wkv = w_kvb.reshape(KV_LORA, MLA_HEADS, QK_NOPE + V_DIM)
    w_kbt = wkv[:, :, :QK_NOPE].reshape(KV_LORA, -1).T
    w_v = wkv[:, :, QK_NOPE:].reshape(KV_LORA, -1)

    def gains(g):
        gr = g[QK_NOPE:]
        return jnp.stack([g[:QK_NOPE], jnp.tile(gr, 2), jnp.tile(gr[perm], 2)])

    return {
        "w_in": w_in[:, :Q_LORA + KV_LORA].astype(BF16), "w_pet": w_pet.astype(BF16),
        "q_norm": q_norm[None, :], "w_qb": w_qb_x.astype(BF16),
        "kv_norm": kv_norm[None, :], "w_kbt": w_kbt.astype(BF16), "w_v": w_v.astype(BF16),
        "gq": gains(q_gain), "gkt": gains(k_gain).T,
    }


def _pad_router(rw):
    hi = rw.astype(BF16)
    lo = (rw - hi.astype(F32)).astype(BF16)
    pad = ((0, 0), (0, LANES - rw.shape[1]))
    return jnp.concatenate([jnp.pad(hi, pad), jnp.pad(lo, pad)], axis=1)


def kernel(x, c, ctx, c_ctx, norm_mix, norm_ffn, ada_w, ada_b, mla_w_in, mla_q_norm, mla_w_qb, mla_kv_norm, mla_w_kvb, mla_q_gain, mla_k_gain, mla_w_out, conv_w_in, conv_w, conv_w_out, router_w, exp_w_gate, exp_w_up, exp_w_down):
    b, n, d = x.shape
    nc = ctx.shape[1]
    depth = ada_w.shape[0]
    assert depth == 2 and b < 8

    cond = jnp.concatenate([c, c_ctx[None, :], jnp.zeros((8 - b - 1, d), F32)], axis=0)
    mod_all = _adaln(cond, ada_w, ada_b).reshape(depth, 8, 6, d)
    mod0, mod1 = mod_all[0, :b], mod_all[1, :b]
    mod0_ctx = mod_all[0, b:b + 1]

    w = _mla_weights(mla_w_in[0], mla_q_norm[0], mla_w_qb[0], mla_kv_norm[0], mla_w_kvb[0],
                     mla_q_gain[0], mla_k_gain[0])
    cos, sin = _rope_tables(n)
    gmix0 = norm_mix[0][None, :]
    qn, qr, kx, vx = _mla_pre(x, mod0, False, gmix0, w, cos, sin, with_q=True, tm=PROJ_TOKEN_TILE)
    kc, vc = _mla_pre(ctx, mod0_ctx, True, gmix0, w, jnp.ones((nc, LANES), F32),
                      jnp.zeros((nc, LANES), F32), with_q=False, tm=PROJ_TOKEN_TILE)
    ox = _attention(qn, qr, kc, vc, kx, vx, tq=ATTN_Q_TILE, tk=ATTN_KEY_TILE)
    x1, h0, aff0 = _attn_out(ox, x, mod0, mla_w_out[0].astype(BF16), norm_ffn[0][None, :],
                             _pad_router(router_w[0]), tm=PROJ_TOKEN_TILE)
    moe0 = _ec_moe(h0, aff0, exp_w_gate, exp_w_up, exp_w_down, 0)

    x3, h1, aff1 = _conv_mixer(x1, moe0, mod0[:, 5:6], mod1, norm_mix[1][None, :],
                               conv_w_in[0].astype(BF16), conv_w[0], conv_w_out[0].astype(BF16),
                               norm_ffn[1][None, :], _pad_router(router_w[1]), tm=TOKEN_TILE)
    moe1 = _ec_moe(h1, aff1, exp_w_gate, exp_w_up, exp_w_down, 1)
    return _residual(x3, moe1, mod1[:, 5:6], tm=TOKEN_TILE)
```

```python
import functools

import jax
import jax.numpy as jnp
from jax import lax
from jax.experimental import pallas as pl
from jax.experimental.pallas import tpu as pltpu

F32 = jnp.float32
BF16 = jnp.bfloat16
HIGHEST = lax.Precision.HIGHEST

GRID_W = 64
N_MIXERS = 2
MLA_HEADS = 8
QK_NOPE = 128
QK_ROPE = 64
QK_DIM = QK_NOPE + QK_ROPE
V_DIM = 128
Q_LORA = 384
KV_LORA = 256
ROPE_FREQS = QK_ROPE // 4
ROPE_THETA = 10000.0
ATTN_SCALE = QK_DIM ** -0.5
LOG2_E = 1.4426950408889634
N_EXPERTS = 16
EC_CAPACITY = 2
EPS = 1e-6

LANES = 128
SUBLANES = 8
F32_INF_BITS = 0x7F800000
F32_VALUE_BITS = 31
VMEM_LIMIT = 56 * 1024 * 1024
MAX_UNROLLED_KEY_BLOCKS = 16
TOKEN_TILE = 1024
PROJ_TOKEN_TILE = 1024
MOE_ROW_TILE = 512
MOE_EXPERT_GROUPS = 2
ATTN_Q_TILE = 1024
ATTN_KEY_TILE = 1024
SOFTMAX_ROWS = 32


def _cparams(sem):
    return pltpu.CompilerParams(dimension_semantics=sem, vmem_limit_bytes=VMEM_LIMIT)


def _rms(x):
    return x * lax.rsqrt(jnp.mean(x * x, axis=-1, keepdims=True) + EPS)


def _modulate(x, g, shift, scale):
    return (_rms(x) * g) * (1.0 + scale) + shift


def _silu(a):
    return a * jax.nn.sigmoid(a)


def _adaln_kernel(c_ref, w_ref, b_ref, o_ref):
    s = _silu(c_ref[...])
    o_ref[0] = jnp.dot(s, w_ref[0], precision=HIGHEST, preferred_element_type=F32) + b_ref[0]


def _adaln(cond, ada_w, ada_b):
    depth, d, d6 = ada_w.shape
    tn = 1536
    return pl.pallas_call(
        _adaln_kernel,
        grid=(depth, d6 // tn),
        in_specs=[
            pl.BlockSpec((8, d), lambda l, j: (0, 0)),
            pl.BlockSpec((1, d, tn), lambda l, j: (l, 0, j)),
            pl.BlockSpec((1, 1, tn), lambda l, j: (l, 0, j)),
        ],
        out_specs=pl.BlockSpec((1, 8, tn), lambda l, j: (l, 0, j)),
        out_shape=jax.ShapeDtypeStruct((depth, 8, d6), F32),
        compiler_params=_cparams(("parallel", "parallel")),
        name="adaln",
    )(cond, ada_w, ada_b.reshape(depth, 1, d6))


def _mla_pre_kernel(x_ref, mod_ref, gmix_ref, win_ref, wpet_ref, qnorm_ref, wqb_ref, kvnorm_ref, wkbt_ref,
                    wv_ref, gq_ref, gkt_ref, cos_ref, sin_ref, cost_ref, sint_ref, *out_refs, with_q):
    if with_q:
        qn_ref, qr_ref, kt_ref, v_ref = out_refs
    else:
        kt_ref, v_ref = out_refs
    mod = mod_ref[0]
    hx = _modulate(x_ref[0], gmix_ref[...], mod[0:1], mod[1:2]).astype(BF16)
    lat = jnp.dot(hx, win_ref[...], preferred_element_type=F32)
    cos = cos_ref[...]
    sin = sin_ref[...]
    lo = lax.broadcasted_iota(jnp.int32, (1, LANES), 1) < QK_ROPE

    def half_sums(v):
        v2 = v * v
        return (jnp.sum(jnp.where(lo, v2, 0.0), axis=-1, keepdims=True),
                jnp.sum(jnp.where(lo, 0.0, v2), axis=-1, keepdims=True))

    if with_q:
        gq = gq_ref[...]
        qn_in = (_rms(lat[:, :Q_LORA]) * qnorm_ref[...]).astype(BF16)
        qf = jnp.dot(qn_in, wqb_ref[...], preferred_element_type=F32)
        nope_w = MLA_HEADS * QK_NOPE
        pair_w = (MLA_HEADS // 2) * LANES
        for p in range(MLA_HEADS // 2):
            rp = qf[:, nope_w + p * LANES: nope_w + (p + 1) * LANES]
            sw = qf[:, nope_w + pair_w + p * LANES: nope_w + pair_w + (p + 1) * LANES]
            s_pair = half_sums(rp)
            r_pair = []
            for hh in range(2):
                h = 2 * p + hh
                nope = qf[:, h * QK_NOPE:(h + 1) * QK_NOPE]
                ms = (jnp.sum(nope * nope, axis=-1, keepdims=True) + s_pair[hh]) * (1.0 / QK_DIM)
                r = lax.rsqrt(ms + EPS) * (ATTN_SCALE * LOG2_E)
                r_pair.append(r)
                qn_ref[0, h] = ((nope * r) * gq[0:1]).astype(BF16)
            roped = (rp * gq[1:2]) * cos + (sw * gq[2:3]) * sin
            qr_ref[0, p] = (roped * jnp.where(lo, r_pair[0], r_pair[1])).astype(BF16)

    nt = (((1,), (1,)), ((), ()))
    gkt = gkt_ref[...]
    kvn_in = (_rms(lat[:, Q_LORA:Q_LORA + KV_LORA]) * kvnorm_ref[...]).astype(BF16)
    v = jnp.dot(kvn_in, wv_ref[...], preferred_element_type=F32)
    knt = lax.dot_general(wkbt_ref[...], kvn_in, nt, preferred_element_type=F32)
    pet = lax.dot_general(wpet_ref[...], hx, nt, preferred_element_type=F32)
    kr_t, ks_t = pet[0:LANES], pet[LANES:2 * LANES]
    s_pe = jnp.sum(kr_t[0:QK_ROPE] * kr_t[0:QK_ROPE], axis=0, keepdims=True)
    k_roped_t = (kr_t * gkt[:, 1:2]) * cost_ref[...] + (ks_t * gkt[:, 2:3]) * sint_ref[...]
    row_lo = lax.broadcasted_iota(jnp.int32, (LANES, 1), 0) < QK_ROPE
    for h in range(MLA_HEADS):
        nope_t = knt[h * QK_NOPE:(h + 1) * QK_NOPE]
        ms = (jnp.sum(nope_t * nope_t, axis=0, keepdims=True) + s_pe) * (1.0 / QK_DIM)
        r = lax.rsqrt(ms + EPS)
        kt_ref[0, h, 0:QK_NOPE, :] = ((nope_t * r) * gkt[:, 0:1]).astype(BF16)
        keep = row_lo if h % 2 == 0 else jnp.logical_not(row_lo)
        kt_ref[0, h, QK_NOPE:QK_NOPE + LANES, :] = jnp.where(keep, k_roped_t * r, 0.0).astype(BF16)
        v_ref[0, h, :, 0:V_DIM] = v[:, h * V_DIM:(h + 1) * V_DIM].astype(BF16)
        v_ref[0, h, :, V_DIM:2 * V_DIM] = jnp.ones((v.shape[0], V_DIM), BF16)


def _mla_pre(x, mod, shared_mod, gmix, w, cos, sin, *, with_q, tm):
    b, n, d = x.shape
    tm = min(tm, n)
    const = lambda shape: pl.BlockSpec(shape, lambda bi, i: (0,) * len(shape))
    mod_map = (lambda bi, i: (0, 0, 0)) if shared_mod else (lambda bi, i: (bi, 0, 0))
    in_specs = [
        pl.BlockSpec((1, tm, d), lambda bi, i: (bi, i, 0)),
        pl.BlockSpec((1, 6, d), mod_map),
        const((1, d)),
        const(w["w_in"].shape),
        const(w["w_pet"].shape),
        const((1, Q_LORA)),
        const(w["w_qb"].shape),
        const((1, KV_LORA)),
        const(w["w_kbt"].shape),
        const(w["w_v"].shape),
        const((3, LANES)),
        const((LANES, 3)),
        pl.BlockSpec((tm, LANES), lambda bi, i: (i, 0)),
        pl.BlockSpec((tm, LANES), lambda bi, i: (i, 0)),
        pl.BlockSpec((LANES, tm), lambda bi, i: (0, i)),
        pl.BlockSpec((LANES, tm), lambda bi, i: (0, i)),
    ]
    head_spec = lambda nh, w_: pl.BlockSpec((1, nh, tm, w_), lambda bi, i: (bi, 0, i, 0))
    out_specs = [pl.BlockSpec((1, MLA_HEADS, 2 * LANES, tm), lambda bi, i: (bi, 0, 0, i)),
                 head_spec(MLA_HEADS, 2 * V_DIM)]
    out_shape = [jax.ShapeDtypeStruct((b, MLA_HEADS, 2 * LANES, n), BF16),
                 jax.ShapeDtypeStruct((b, MLA_HEADS, n, 2 * V_DIM), BF16)]
    if with_q:
        out_specs = [head_spec(MLA_HEADS, QK_NOPE), head_spec(MLA_HEADS // 2, LANES)] + out_specs
        out_shape = [jax.ShapeDtypeStruct((b, MLA_HEADS, n, QK_NOPE), BF16),
                     jax.ShapeDtypeStruct((b, MLA_HEADS // 2, n, LANES), BF16)] + out_shape
    return pl.pallas_call(
        functools.partial(_mla_pre_kernel, with_q=with_q),
        grid=(b, n // tm),
        in_specs=in_specs,
        out_specs=out_specs,
        out_shape=out_shape,
        compiler_params=_cparams(("parallel", "parallel")),
        name="mla_pre_q" if with_q else "mla_pre_ctx",
    )(x, mod, gmix, w["w_in"], w["w_pet"], w["q_norm"], w["w_qb"], w["kv_norm"], w["w_kbt"], w["w_v"],
      w["gq"], w["gkt"], cos, sin, cos.T, sin.T)


def _attn_kernel(qn_ref, qr_ref, kc_ref, vc_ref, kx_ref, vx_ref, o_ref,
                 m_s, acc_s, sc_s, pc_s, s0_s, s1_s, p0_s, p1_s, a0_s, a1_s, *, tk):
    q = jnp.concatenate([qn_ref[0, 0], qr_ref[0, 0]], axis=-1)
    nblk = 1 + vx_ref.shape[2] // tk
    s_buf = lambda i: sc_s if i == 0 else (s0_s, s1_s)[i % 2]
    p_buf = lambda i: pc_s if i == 0 else (p0_s, p1_s)[i % 2]
    a_buf = lambda i: (a0_s, a1_s)[i % 2]
    keys_t = lambda i: kc_ref[0, 0] if i == 0 else kx_ref[0, 0, :, (i - 1) * tk:i * tk]
    vals = lambda i: vc_ref[0, 0] if i == 0 else vx_ref[0, 0, (i - 1) * tk:i * tk, :]

    def scores(i):
        s_buf(i)[...] = jnp.dot(q, keys_t(i), preferred_element_type=F32)

    def softmax(i):
        for r in range(0, q.shape[0], SOFTMAX_ROWS):
            rows = slice(r, r + SOFTMAX_ROWS)
            s = s_buf(i)[rows, :]
            m_prev = m_s[rows, :]
            m_new = jnp.maximum(m_prev, jnp.max(s, axis=-1, keepdims=True))
            p_buf(i)[rows, :] = jnp.exp2(s - m_new).astype(BF16)
            a_buf(i)[rows, :] = jnp.exp2(m_prev - m_new)
            m_s[rows, :] = m_new

    def accumulate(i):
        acc_s[...] = a_buf(i)[...] * acc_s[...] + jnp.dot(p_buf(i)[...], vals(i), preferred_element_type=F32)

    m_s[...] = jnp.full(m_s.shape, -jnp.inf, F32)
    acc_s[...] = jnp.zeros(acc_s.shape, F32)
    scores(0)
    scores(1)
    softmax(0)
    for i in range(nblk):
        if i + 2 < nblk:
            scores(i + 2)
        if i + 1 < nblk:
            softmax(i + 1)
        accumulate(i)
    acc = acc_s[...]
    o_ref[0] = (acc[:, :V_DIM] / acc[:, V_DIM:]).astype(o_ref.dtype)


def _attention(qn, qr, kc, vc, kx, vx, *, tq, tk):
    b, h, n, _ = qn.shape
    nc = vc.shape[2]
    tq = min(tq, n)
    tk = min(tk, n)
    assert n // tk <= MAX_UNROLLED_KEY_BLOCKS
    kt_spec = lambda cols: pl.BlockSpec((1, 1, 2 * LANES, cols), lambda bi, hi, i: (bi, hi, 0, 0))
    v_spec = lambda rows: pl.BlockSpec((1, 1, rows, 2 * V_DIM), lambda bi, hi, i: (bi, hi, 0, 0))
    return pl.pallas_call(
        functools.partial(_attn_kernel, tk=tk),
        grid=(b, h, n // tq),
        in_specs=[
            pl.BlockSpec((1, 1, tq, QK_NOPE), lambda bi, hi, i: (bi, hi, i, 0)),
            pl.BlockSpec((1, 1, tq, LANES), lambda bi, hi, i: (bi, hi // 2, i, 0)),
            kt_spec(nc), v_spec(nc), kt_spec(n), v_spec(n),
        ],
        out_specs=pl.BlockSpec((1, tq, V_DIM), lambda bi, hi, i: (bi, i, hi)),
        out_shape=jax.ShapeDtypeStruct((b, n, h * V_DIM), BF16),
        scratch_shapes=[
            pltpu.VMEM((tq, 1), F32), pltpu.VMEM((tq, 2 * V_DIM), F32),
            pltpu.VMEM((tq, nc), F32), pltpu.VMEM((tq, nc), BF16),
            pltpu.VMEM((tq, tk), F32), pltpu.VMEM((tq, tk), F32),
            pltpu.VMEM((tq, tk), BF16), pltpu.VMEM((tq, tk), BF16),
            pltpu.VMEM((tq, 1), F32), pltpu.VMEM((tq, 1), F32),
        ],
        compiler_params=_cparams(("parallel", "parallel", "arbitrary")),
        name="flash_attn",
    )(qn, qr, kc, vc, kx, vx)


def _ffn_pre(x_new, mod, gffn_ref, rw_ref, h_ref, aff_ref):
    h2 = _modulate(x_new, gffn_ref[...], mod[3:4], mod[4:5])
    hi = h2.astype(BF16)
    lo = (h2 - hi.astype(F32)).astype(BF16)
    h_ref[0] = hi
    rw = rw_ref[...]
    t = jnp.dot(hi, rw, preferred_element_type=F32) + jnp.dot(lo, rw, preferred_element_type=F32)
    logits = t[:, :LANES] + t[:, LANES:]
    is_expert = lax.broadcasted_iota(jnp.int32, (1, LANES), 1) < N_EXPERTS
    logits = jnp.where(is_expert, logits, -jnp.inf)
    e = jnp.exp(logits - jnp.max(logits, axis=-1, keepdims=True))
    aff = e / jnp.sum(e, axis=-1, keepdims=True)
    aff_ref[0] = aff.T[:N_EXPERTS]


def _attn_out_kernel(o_ref, x_ref, mod_ref, wout_ref, gffn_ref, rw_ref, x1_ref, h_ref, aff_ref):
    mod = mod_ref[0]
    o = jnp.dot(o_ref[0], wout_ref[...], preferred_element_type=F32)
    x1 = x_ref[0] + mod[2:3] * o
    x1_ref[0] = x1
    _ffn_pre(x1, mod, gffn_ref, rw_ref, h_ref, aff_ref)


def _attn_out(ox, x, mod, w_out, gffn, rw, *, tm):
    b, n, d = x.shape
    tm = min(tm, n)
    const = lambda shape: pl.BlockSpec(shape, lambda bi, i: (0,) * len(shape))
    tok = lambda: pl.BlockSpec((1, tm, d), lambda bi, i: (bi, i, 0))
    return pl.pallas_call(
        _attn_out_kernel,
        grid=(b, n // tm),
        in_specs=[tok(), tok(), pl.BlockSpec((1, 6, d), lambda bi, i: (bi, 0, 0)),
                  const(w_out.shape), const((1, d)), const(rw.shape)],
        out_specs=[tok(), tok(), pl.BlockSpec((1, N_EXPERTS, tm), lambda bi, i: (bi, 0, i))],
        out_shape=[jax.ShapeDtypeStruct((b, n, d), F32),
                   jax.ShapeDtypeStruct((b, n, d), BF16),
                   jax.ShapeDtypeStruct((b, N_EXPERTS, n), F32)],
        compiler_params=_cparams(("parallel", "parallel")),
        name="attn_out",
    )(ox, x, mod, w_out, gffn, rw)


def _conv_kernel(xm_ref, xp_ref, xn_ref, mm_ref, mp_ref, mn_ref, gprev_ref, mod_ref, gmix_ref,
                 win_ref, cw_ref, wout_ref, gffn_ref, rw_ref, x3_ref, h_ref, aff_ref):
    i = pl.program_id(1)
    tm, d = xm_ref.shape[1], xm_ref.shape[2]
    halo = xp_ref.shape[1]
    gprev = gprev_ref[0]
    mod = mod_ref[0]
    xm = xm_ref[0] + gprev * _from_row_tiles(mm_ref)
    xe = jnp.concatenate([xp_ref[0] + gprev * _from_row_tiles(mp_ref), xm,
                          xn_ref[0] + gprev * _from_row_tiles(mn_ref)], axis=0)
    hx = _modulate(xe, gmix_ref[...], mod[0:1], mod[1:2]).astype(BF16)
    proj = jnp.dot(hx, win_ref[...], preferred_element_type=F32)
    u = proj[:, d:2 * d] * proj[:, 2 * d:3 * d]
    row = lax.broadcasted_iota(jnp.int32, (tm + 2 * halo, 1), 0)
    outside = jnp.logical_or(jnp.logical_and(i == 0, row < halo),
                             jnp.logical_and(i == pl.num_programs(1) - 1, row >= tm + halo))
    u = jnp.where(outside, 0.0, u)
    rows = tm + 2 * halo
    u_prev = pltpu.roll(u, 1, axis=0)[halo:halo + tm]
    u_next = pltpu.roll(u, rows - 1, axis=0)[halo:halo + tm]
    cw = cw_ref[...]
    y = cw[0:1] * u_prev + cw[1:2] * u[halo:halo + tm] + cw[2:3] * u_next
    z = (proj[halo:halo + tm, 0:d] * y).astype(BF16)
    x3 = xm + mod[2:3] * jnp.dot(z, wout_ref[...], preferred_element_type=F32)
    x3_ref[0] = x3
    _ffn_pre(x3, mod, gffn_ref, rw_ref, h_ref, aff_ref)


def _conv_mixer(x, moe, gprev, mod, gmix, w_in, cw, w_out, gffn, rw, *, tm):
    b, n, d = x.shape
    tm = min(tm, n)
    halo = 8
    nb = tm // halo
    last = n // halo - 1
    const = lambda shape: pl.BlockSpec(shape, lambda bi, i: (0,) * len(shape))
    tok = lambda: pl.BlockSpec((1, tm, d), lambda bi, i: (bi, i, 0))
    prev = lambda: pl.BlockSpec((1, halo, d), lambda bi, i: (bi, jnp.maximum(i * nb - 1, 0), 0))
    nxt = lambda: pl.BlockSpec((1, halo, d), lambda bi, i: (bi, jnp.minimum((i + 1) * nb, last), 0))
    tok_rt = pl.BlockSpec((1, SUBLANES, tm, LANES), lambda bi, i: (bi, 0, i, 0))
    prev_rt = pl.BlockSpec((1, SUBLANES, halo, LANES), lambda bi, i: (bi, 0, jnp.maximum(i * nb - 1, 0), 0))
    nxt_rt = pl.BlockSpec((1, SUBLANES, halo, LANES), lambda bi, i: (bi, 0, jnp.minimum((i + 1) * nb, last), 0))
    return pl.pallas_call(
        _conv_kernel,
        grid=(b, n // tm),
        in_specs=[tok(), prev(), nxt(), tok_rt, prev_rt, nxt_rt,
                  pl.BlockSpec((1, 1, d), lambda bi, i: (bi, 0, 0)),
                  pl.BlockSpec((1, 6, d), lambda bi, i: (bi, 0, 0)),
                  const((1, d)), const(w_in.shape), const(cw.shape), const(w_out.shape),
                  const((1, d)), const(rw.shape)],
        out_specs=[tok(), tok(), pl.BlockSpec((1, N_EXPERTS, tm), lambda bi, i: (bi, 0, i))],
        out_shape=[jax.ShapeDtypeStruct((b, n, d), F32),
                   jax.ShapeDtypeStruct((b, n, d), BF16),
                   jax.ShapeDtypeStruct((b, N_EXPERTS, n), F32)],
        compiler_params=_cparams(("parallel", "parallel")),
        name="conv_mixer",
    )(x, x, x, moe, moe, moe, gprev, mod, gmix, w_in, cw, w_out, gffn, rw)


def _moe_kernel(xs_ref, g_ref, wg_ref, wu_ref, wd_ref, y_ref, wg_s, wu_s, wd_s, *, tr, tf):
    c = xs_ref.shape[2]
    f_total = wg_s.shape[1]

    @pl.when(pl.program_id(1) == 0)
    def _():
        for src, dst in ((wg_ref, wg_s), (wu_ref, wu_s), (wd_ref, wd_s)):
            for r in range(0, dst.shape[0], tr):
                dst[r:r + tr, :] = src[0, 0, r:r + tr, :].astype(BF16)

    for r0 in range(0, c, tr):
        xs = xs_ref[0, 0, r0:r0 + tr, :]
        acc = jnp.zeros((tr, wd_s.shape[1]), F32)
        for f in range(f_total // tf):
            a = jnp.dot(xs, wg_s[:, f * tf:(f + 1) * tf], preferred_element_type=F32)
            u = jnp.dot(xs, wu_s[:, f * tf:(f + 1) * tf], preferred_element_type=F32)
            hm = (_silu(a) * u).astype(BF16)
            acc = acc + jnp.dot(hm, wd_s[f * tf:(f + 1) * tf, :], preferred_element_type=F32)
        y = acc * g_ref[0, 0, r0:r0 + tr, :]
        for k in range(SUBLANES):
            y_ref[0, 0, r0:r0 + tr, k, :] = y[:, k * LANES:(k + 1) * LANES]


def _moe_ffn(xs, g, wg, wu, wd, layer, e0):
    b, e, c, d = xs.shape
    f = wg.shape[3]
    tr = min(MOE_ROW_TILE, c)
    tf = min(512, f)
    w_spec = lambda rows_, cols: pl.BlockSpec((1, 1, rows_, cols), lambda ei, bi: (layer, e0 + ei, 0, 0))
    return pl.pallas_call(
        functools.partial(_moe_kernel, tr=tr, tf=tf),
        grid=(e, b),
        in_specs=[
            pl.BlockSpec((1, 1, c, d), lambda ei, bi: (bi, ei, 0, 0)),
            pl.BlockSpec((1, 1, c, 1), lambda ei, bi: (bi, ei, 0, 0)),
            w_spec(d, f), w_spec(d, f), w_spec(f, d),
        ],
        out_specs=pl.BlockSpec((1, 1, c, SUBLANES, LANES), lambda ei, bi: (bi, ei, 0, 0, 0)),
        out_shape=jax.ShapeDtypeStruct((b, e, c, SUBLANES, LANES), F32),
        scratch_shapes=[pltpu.VMEM((d, f), BF16), pltpu.VMEM((d, f), BF16), pltpu.VMEM((f, d), BF16)],
        compiler_params=_cparams(("arbitrary", "arbitrary")),
        name="moe_ffn",
    )(xs, g.reshape(b, e, c, 1), wg, wu, wd)


def _from_row_tiles(ref):
    return jnp.concatenate([ref[0, k] for k in range(SUBLANES)], axis=-1)


def _combine_kernel(idx_ref, *refs, unroll, zero_rows):
    y_refs, (out_hbm, acc_s, sem) = refs[:-3], refs[-3:]
    bi = pl.program_id(0)
    ei = pl.program_id(1)
    n = acc_s.shape[0]
    c = y_refs[0].shape[2]
    per_group = pl.num_programs(1) // len(y_refs)

    @pl.when(ei == 0)
    def _():
        def zero(i, carry):
            acc_s[pl.ds(pl.multiple_of(i * zero_rows, zero_rows), zero_rows)] = jnp.zeros(
                (zero_rows,) + acc_s.shape[1:], F32)
            return carry

        lax.fori_loop(0, n // zero_rows, zero, 0)

    def add_rows(y_ref):
        def rows(i, carry):
            base = i * unroll
            tok = [idx_ref[0, 0, base + u] for u in range(unroll)]
            new = [acc_s[tok[u]] + y_ref[0, 0, base + u] for u in range(unroll)]
            for u in range(unroll):
                acc_s[tok[u]] = new[u]
            return carry

        lax.fori_loop(0, c // unroll, rows, 0)

    for gi, y_ref in enumerate(y_refs):
        pl.when(ei // per_group == gi)(functools.partial(add_rows, y_ref))

    @pl.when(ei == pl.num_programs(1) - 1)
    def _():
        copies = [pltpu.make_async_copy(acc_s.at[:, k, :], out_hbm.at[bi, k], sem.at[k])
                  for k in range(SUBLANES)]
        for cp in copies:
            cp.start()
        for cp in copies:
            cp.wait()


def _moe_combine(idx, ys, n):
    b, e, c = idx.shape
    per_group = e // len(ys)
    unroll = 8
    zero_rows = min(256, n)

    def y_spec(gi):
        local = lambda ei: jnp.clip(ei - gi * per_group, 0, per_group - 1)
        return pl.BlockSpec((1, 1, c, SUBLANES, LANES), lambda bi, ei: (bi, local(ei), 0, 0, 0))

    return pl.pallas_call(
        functools.partial(_combine_kernel, unroll=unroll, zero_rows=zero_rows),
        grid=(b, e),
        in_specs=[pl.BlockSpec((1, 1, c), lambda bi, ei: (bi * e + ei, 0, 0), memory_space=pltpu.SMEM)]
        + [y_spec(gi) for gi in range(len(ys))],
        out_specs=pl.BlockSpec(memory_space=pl.ANY),
        out_shape=jax.ShapeDtypeStruct((b, SUBLANES, n, LANES), F32),
        scratch_shapes=[pltpu.VMEM((n, SUBLANES, LANES), F32), pltpu.SemaphoreType.DMA((SUBLANES,))],
        compiler_params=_cparams(("arbitrary", "arbitrary")),
        name="moe_combine",
    )(idx.reshape(b * e, 1, c), *ys)


def _route_kernel(aff_ref, uexcl_ref, uinclt_ref, ones_ref, lstrict_ref, idx_ref, gt_s, eq_s, need_s, *, cap):
    e_n, nch, _ = aff_ref.shape[1:]
    aff = aff_ref[0]

    def count(mask):
        return jnp.sum(jnp.sum(mask.astype(F32), axis=2, keepdims=True), axis=1, keepdims=True)

    def search(_, carry):
        lo, hi = carry
        mid = lo + ((hi - lo + 1) >> 1)
        ok = count(aff >= pltpu.bitcast(mid, F32)) >= cap
        return jnp.where(ok, mid, lo), jnp.where(ok, hi, mid - 1)

    lo0 = jnp.zeros((e_n, 1, 1), jnp.int32)
    hi0 = jnp.full((e_n, 1, 1), F32_INF_BITS, jnp.int32)
    tau_bits, _ = lax.fori_loop(0, F32_VALUE_BITS, search, (lo0, hi0))
    tau = pltpu.bitcast(tau_bits, F32)
    gt = aff > tau
    eq = aff == tau
    gt_s[...] = gt.astype(BF16)
    eq_s[...] = eq.astype(BF16)
    need_s[...] = jnp.broadcast_to(cap - count(gt), need_s.shape)

    n_slot = idx_ref.shape[3]
    slot = lax.broadcasted_iota(jnp.int32, (1, n_slot), 1).astype(F32)
    chunk_id = lax.broadcasted_iota(jnp.int32, (nch, n_slot), 0).astype(F32)
    widen = lambda a: jnp.concatenate([a] * (n_slot // LANES), axis=1)

    def chunk_prefix(mask):
        cnt = jnp.dot(mask, ones_ref[...], preferred_element_type=F32)
        return cnt, jnp.dot(lstrict_ref[...], cnt.astype(BF16), preferred_element_type=F32)

    def per_expert(e, carry):
        eq_e = eq_s[e]
        _, eq_start = chunk_prefix(eq_e)
        eq_rank = eq_start + jnp.dot(eq_e, uexcl_ref[...], preferred_element_type=F32)
        sel = gt_s[e] + jnp.where(eq_rank < need_s[e], eq_e, jnp.zeros_like(eq_e))
        cnt, start = chunk_prefix(sel)
        incl_t = lax.dot_general(uinclt_ref[...], sel, (((1,), (1,)), ((), ())),
                                 preferred_element_type=F32)
        start_w, cnt_w = widen(start), widen(cnt)
        hit = jnp.logical_and(start_w <= slot, slot < start_w + cnt_w)
        hit_f = hit.astype(F32)
        slot_start = jnp.sum(hit_f * start_w, axis=0, keepdims=True)
        slot_chunk = jnp.sum(hit_f * chunk_id, axis=0, keepdims=True)
        incl_of_slot = jnp.dot(incl_t.astype(BF16), hit.astype(BF16), preferred_element_type=F32)
        local = jnp.sum((incl_of_slot <= slot - slot_start).astype(F32), axis=0, keepdims=True)
        idx_ref[0, e] = (slot_chunk * LANES + local).astype(jnp.int32)
        return carry

    lax.fori_loop(0, e_n, per_expert, 0)


def _route(aff, cap):
    b, e, n = aff.shape
    assert n % LANES == 0 and cap % LANES == 0
    nch = n // LANES
    i = jnp.arange(LANES)
    t = jnp.arange(nch)
    uexcl = (i[:, None] < i[None, :]).astype(BF16)
    uinclt = (i[None, :] <= i[:, None]).astype(BF16)
    lstrict = (t[None, :] < t[:, None]).astype(BF16)
    const = lambda shape: pl.BlockSpec(shape, lambda bi: (0,) * len(shape))
    mask_scratch = lambda dt: pltpu.VMEM((e, nch, LANES), dt)
    idx = pl.pallas_call(
        functools.partial(_route_kernel, cap=cap),
        grid=(b,),
        in_specs=[pl.BlockSpec((1, e, nch, LANES), lambda bi: (bi, 0, 0, 0)),
                  const((LANES, LANES)), const((LANES, LANES)), const((LANES, LANES)), const((nch, nch))],
        out_specs=pl.BlockSpec((1, e, 1, cap), lambda bi: (bi, 0, 0, 0)),
        out_shape=jax.ShapeDtypeStruct((b, e, 1, cap), jnp.int32),
        scratch_shapes=[mask_scratch(BF16), mask_scratch(BF16), mask_scratch(F32)],
        compiler_params=_cparams(("parallel",)),
        name="moe_route",
    )(aff.reshape(b, e, nch, LANES), uexcl, uinclt, jnp.ones((LANES, LANES), BF16), lstrict)
    return idx.reshape(b, e, cap)


def _ec_moe(h, aff, wg, wu, wd, layer):
    b, n, d = h.shape
    assert d == SUBLANES * LANES
    cap = EC_CAPACITY * n // N_EXPERTS
    idx = _route(aff, cap)
    g = jnp.take_along_axis(aff, idx, axis=-1)
    per_group = N_EXPERTS // MOE_EXPERT_GROUPS
    ys = []
    for e0 in range(0, N_EXPERTS, per_group):
        idx_g = idx[:, e0:e0 + per_group]
        xs = jax.vmap(lambda hb, ib: hb[ib])(h, idx_g)
        ys.append(_moe_ffn(xs, g[:, e0:e0 + per_group], wg, wu, wd, layer, e0))
    return _moe_combine(idx, ys, n)


def _residual_kernel(x_ref, m_ref, g_ref, o_ref):
    o_ref[0] = x_ref[0] + g_ref[0] * _from_row_tiles(m_ref)


def _residual(x, moe, gate, *, tm):
    b, n, d = x.shape
    tm = min(tm, n)
    tok = lambda: pl.BlockSpec((1, tm, d), lambda bi, i: (bi, i, 0))
    return pl.pallas_call(
        _residual_kernel,
        grid=(b, n // tm),
        in_specs=[tok(), pl.BlockSpec((1, SUBLANES, tm, LANES), lambda bi, i: (bi, 0, i, 0)),
                  pl.BlockSpec((1, 1, d), lambda bi, i: (bi, 0, 0))],
        out_specs=tok(),
        out_shape=jax.ShapeDtypeStruct((b, n, d), F32),
        compiler_params=_cparams(("parallel", "parallel")),
        name="moe_residual",
    )(x, moe, gate)


def _rope_tables(n):
    rows = n // GRID_W
    row = jnp.broadcast_to(jnp.arange(rows, dtype=F32)[:, None], (rows, GRID_W)).reshape(-1)
    col = jnp.broadcast_to(jnp.arange(GRID_W, dtype=F32)[None, :], (rows, GRID_W)).reshape(-1)
    inv = ROPE_THETA ** (-jnp.arange(ROPE_FREQS, dtype=F32) / ROPE_FREQS)
    ar, ac = row[:, None] * inv, col[:, None] * inv
    cos = jnp.concatenate([jnp.cos(ar), jnp.cos(ar), jnp.cos(ac), jnp.cos(ac)], axis=-1)
    sin = jnp.concatenate([-jnp.sin(ar), jnp.sin(ar), -jnp.sin(ac), jnp.sin(ac)], axis=-1)
    return jnp.tile(cos, (1, 2)), jnp.tile(sin, (1, 2))


def _swap_perm():
    f = ROPE_FREQS
    base = jnp.arange(QK_ROPE)
    return jnp.where((base // f) % 2 == 0, base + f, base - f)


def _mla_weights(w_in, q_norm, w_qb, kv_norm, w_kvb, q_gain, k_gain):
    perm = _swap_perm()
    pe = w_in[:, Q_LORA + KV_LORA:]
    pe_sw = pe[:, perm]
    w_pet = jnp.concatenate([pe, pe, pe_sw, pe_sw], axis=1).T
    wq = w_qb.reshape(Q_LORA, MLA_HEADS, QK_DIM)
    wq_rope = wq[:, :, QK_NOPE:]
    w_qb_x = jnp.concatenate([
        wq[:, :, :QK_NOPE].reshape(Q_LORA, -1),
        wq_rope.reshape(Q_LORA, -1),
        wq_rope[:, :, perm].reshape(Q_LORA, -1)], axis=1)
    wkv = w_kvb.reshape(KV_LORA, MLA_HEADS, QK_NOPE + V_DIM)
    w_kbt = wkv[:, :, :QK_NOPE].reshape(KV_LORA, -1).T
    w_v = wkv[:, :, QK_NOPE:].reshape(KV_LORA, -1)

    def gains(g):
        gr = g[QK_NOPE:]
        return jnp.stack([g[:QK_NOPE], jnp.tile(gr, 2), jnp.tile(gr[perm], 2)])

    return {
        "w_in": w_in[:, :Q_LORA + KV_LORA].astype(BF16), "w_pet": w_pet.astype(BF16),
        "q_norm": q_norm[None, :], "w_qb": w_qb_x.astype(BF16),
        "kv_norm": kv_norm[None, :], "w_kbt": w_kbt.astype(BF16), "w_v": w_v.astype(BF16),
        "gq": gains(q_gain), "gkt": gains(k_gain).T,
    }


def _pad_router(rw):
    hi = rw.astype(BF16)
    lo = (rw - hi.astype(F32)).astype(BF16)
    pad = ((0, 0), (0, LANES - rw.shape[1]))
    return jnp.concatenate([jnp.pad(hi, pad), jnp.pad(lo, pad)], axis=1)


def kernel(x, c, ctx, c_ctx, norm_mix, norm_ffn, ada_w, ada_b, mla_w_in, mla_q_norm, mla_w_qb, mla_kv_norm, mla_w_kvb, mla_q_gain, mla_k_gain, mla_w_out, conv_w_in, conv_w, conv_w_out, router_w, exp_w_gate, exp_w_up, exp_w_down):
    b, n, d = x.shape
    nc = ctx.shape[1]
    depth = ada_w.shape[0]
    assert depth == 2 and b < 8

    cond = jnp.concatenate([c, c_ctx[None, :], jnp.zeros((8 - b - 1, d), F32)], axis=0)
    mod_all = _adaln(cond, ada_w, ada_b).reshape(depth, 8, 6, d)
    mod0, mod1 = mod_all[0, :b], mod_all[1, :b]
    mod0_ctx = mod_all[0, b:b + 1]

    w = _mla_weights(mla_w_in[0], mla_q_norm[0], mla_w_qb[0], mla_kv_norm[0], mla_w_kvb[0],
                     mla_q_gain[0], mla_k_gain[0])
    cos, sin = _rope_tables(n)
    gmix0 = norm_mix[0][None, :]
    qn, qr, kx, vx = _mla_pre(x, mod0, False, gmix0, w, cos, sin, with_q=True, tm=PROJ_TOKEN_TILE)
    kc, vc = _mla_pre(ctx, mod0_ctx, True, gmix0, w, jnp.ones((nc, LANES), F32),
                      jnp.zeros((nc, LANES), F32), with_q=False, tm=PROJ_TOKEN_TILE)
    ox = _attention(qn, qr, kc, vc, kx, vx, tq=ATTN_Q_TILE, tk=ATTN_KEY_TILE)
    x1, h0, aff0 = _attn_out(ox, x, mod0, mla_w_out[0].astype(BF16), norm_ffn[0][None, :],
                             _pad_router(router_w[0]), tm=PROJ_TOKEN_TILE)
    moe0 = _ec_moe(h0, aff0, exp_w_gate, exp_w_up, exp_w_down, 0)

    x3, h1, aff1 = _conv_mixer(x1, moe0, mod0[:, 5:6], mod1, norm_mix[1][None, :],
                               conv_w_in[0].astype(BF16), conv_w[0], conv_w_out[0].astype(BF16),
                               norm_ffn[1][None, :], _pad_router(router_w[1]), tm=TOKEN_TILE)
    moe1 = _ec_moe(h1, aff1, exp_w_gate, exp_w_up, exp_w_down, 1)
    return _residual(x3, moe1, mod1[:, 5:6], tm=TOKEN_TILE)
```

```python
import functools

import jax
import jax.numpy as jnp
from jax import lax
from jax.experimental import pallas as pl
from jax.experimental.pallas import tpu as pltpu

F32 = jnp.float32
BF16 = jnp.bfloat16
HIGHEST = lax.Precision.HIGHEST

GRID_W = 64
N_MIXERS = 2
MLA_HEADS = 8
QK_NOPE = 128
QK_ROPE = 64
QK_DIM = QK_NOPE + QK_ROPE
V_DIM = 128
Q_LORA = 384
KV_LORA = 256
ROPE_FREQS = QK_ROPE // 4
ROPE_THETA = 10000.0
ATTN_SCALE = QK_DIM ** -0.5
LOG2_E = 1.4426950408889634
N_EXPERTS = 16
EC_CAPACITY = 2
EPS = 1e-6

LANES = 128
SUBLANES = 8
F32_INF_BITS = 0x7F800000
F32_VALUE_BITS = 31
VMEM_LIMIT = 56 * 1024 * 1024
MAX_UNROLLED_KEY_BLOCKS = 16
TOKEN_TILE = 1024
PROJ_TOKEN_TILE = 1024
MOE_ROW_TILE = 512
MOE_EXPERT_GROUPS = 2
ATTN_Q_TILE = 1024
ATTN_KEY_TILE = 2048
SOFTMAX_ROWS = 32


def _cparams(sem):
    return pltpu.CompilerParams(dimension_semantics=sem, vmem_limit_bytes=VMEM_LIMIT)


def _rms(x):
    return x * lax.rsqrt(jnp.mean(x * x, axis=-1, keepdims=True) + EPS)


def _modulate(x, g, shift, scale):
    return (_rms(x) * g) * (1.0 + scale) + shift


def _silu(a):
    return a * jax.nn.sigmoid(a)


def _adaln_kernel(c_ref, w_ref, b_ref, o_ref):
    s = _silu(c_ref[...])
    o_ref[0] = jnp.dot(s, w_ref[0], precision=HIGHEST, preferred_element_type=F32) + b_ref[0]


def _adaln(cond, ada_w, ada_b):
    depth, d, d6 = ada_w.shape
    tn = 1536
    return pl.pallas_call(
        _adaln_kernel,
        grid=(depth, d6 // tn),
        in_specs=[
            pl.BlockSpec((8, d), lambda l, j: (0, 0)),
            pl.BlockSpec((1, d, tn), lambda l, j: (l, 0, j)),
            pl.BlockSpec((1, 1, tn), lambda l, j: (l, 0, j)),
        ],
        out_specs=pl.BlockSpec((1, 8, tn), lambda l, j: (l, 0, j)),
        out_shape=jax.ShapeDtypeStruct((depth, 8, d6), F32),
        compiler_params=_cparams(("parallel", "parallel")),
        name="adaln",
    )(cond, ada_w, ada_b.reshape(depth, 1, d6))


def _mla_pre_kernel(x_ref, mod_ref, gmix_ref, win_ref, wpet_ref, qnorm_ref, wqb_ref, kvnorm_ref, wkbt_ref,
                    wv_ref, gq_ref, gkt_ref, cos_ref, sin_ref, cost_ref, sint_ref, *out_refs, with_q):
    if with_q:
        qn_ref, qr_ref, kt_ref, v_ref = out_refs
    else:
        kt_ref, v_ref = out_refs
    mod = mod_ref[0]
    hx = _modulate(x_ref[0], gmix_ref[...], mod[0:1], mod[1:2]).astype(BF16)
    lat = jnp.dot(hx, win_ref[...], preferred_element_type=F32)
    cos = cos_ref[...]
    sin = sin_ref[...]
    lo = lax.broadcasted_iota(jnp.int32, (1, LANES), 1) < QK_ROPE

    def half_sums(v):
        v2 = v * v
        return (jnp.sum(jnp.where(lo, v2, 0.0), axis=-1, keepdims=True),
                jnp.sum(jnp.where(lo, 0.0, v2), axis=-1, keepdims=True))

    if with_q:
        gq = gq_ref[...]
        qn_in = (_rms(lat[:, :Q_LORA]) * qnorm_ref[...]).astype(BF16)
        qf = jnp.dot(qn_in, wqb_ref[...], preferred_element_type=F32)
        nope_w = MLA_HEADS * QK_NOPE
        pair_w = (MLA_HEADS // 2) * LANES
        for p in range(MLA_HEADS // 2):
            rp = qf[:, nope_w + p * LANES: nope_w + (p + 1) * LANES]
            sw = qf[:, nope_w + pair_w + p * LANES: nope_w + pair_w + (p + 1) * LANES]
            s_pair = half_sums(rp)
            r_pair = []
            for hh in range(2):
                h = 2 * p + hh
                nope = qf[:, h * QK_NOPE:(h + 1) * QK_NOPE]
                ms = (jnp.sum(nope * nope, axis=-1, keepdims=True) + s_pair[hh]) * (1.0 / QK_DIM)
                r = lax.rsqrt(ms + EPS) * (ATTN_SCALE * LOG2_E)
                r_pair.append(r)
                qn_ref[0, h] = ((nope * r) * gq[0:1]).astype(BF16)
            roped = (rp * gq[1:2]) * cos + (sw * gq[2:3]) * sin
            qr_ref[0, p] = (roped * jnp.where(lo, r_pair[0], r_pair[1])).astype(BF16)

    nt = (((1,), (1,)), ((), ()))
    gkt = gkt_ref[...]
    kvn_in = (_rms(lat[:, Q_LORA:Q_LORA + KV_LORA]) * kvnorm_ref[...]).astype(BF16)
    v = jnp.dot(kvn_in, wv_ref[...], preferred_element_type=F32)
    knt = lax.dot_general(wkbt_ref[...], kvn_in, nt, preferred_element_type=F32)
    pet = lax.dot_general(wpet_ref[...], hx, nt, preferred_element_type=F32)
    kr_t, ks_t = pet[0:LANES], pet[LANES:2 * LANES]
    s_pe = jnp.sum(kr_t[0:QK_ROPE] * kr_t[0:QK_ROPE], axis=0, keepdims=True)
    k_roped_t = (kr_t * gkt[:, 1:2]) * cost_ref[...] + (ks_t * gkt[:, 2:3]) * sint_ref[...]
    row_lo = lax.broadcasted_iota(jnp.int32, (LANES, 1), 0) < QK_ROPE
    for h in range(MLA_HEADS):
        nope_t = knt[h * QK_NOPE:(h + 1) * QK_NOPE]
        ms = (jnp.sum(nope_t * nope_t, axis=0, keepdims=True) + s_pe) * (1.0 / QK_DIM)
        r = lax.rsqrt(ms + EPS)
        kt_ref[0, h, 0:QK_NOPE, :] = ((nope_t * r) * gkt[:, 0:1]).astype(BF16)
        keep = row_lo if h % 2 == 0 else jnp.logical_not(row_lo)
        kt_ref[0, h, QK_NOPE:QK_NOPE + LANES, :] = jnp.where(keep, k_roped_t * r, 0.0).astype(BF16)
        v_ref[0, h, :, 0:V_DIM] = v[:, h * V_DIM:(h + 1) * V_DIM].astype(BF16)
        v_ref[0, h, :, V_DIM:2 * V_DIM] = jnp.ones((v.shape[0], V_DIM), BF16)


def _mla_pre(x, mod, shared_mod, gmix, w, cos, sin, *, with_q, tm):
    b, n, d = x.shape
    tm = min(tm, n)
    const = lambda shape: pl.BlockSpec(shape, lambda bi, i: (0,) * len(shape))
    mod_map = (lambda bi, i: (0, 0, 0)) if shared_mod else (lambda bi, i: (bi, 0, 0))
    in_specs = [
        pl.BlockSpec((1, tm, d), lambda bi, i: (bi, i, 0)),
        pl.BlockSpec((1, 6, d), mod_map),
        const((1, d)),
        const(w["w_in"].shape),
        const(w["w_pet"].shape),
        const((1, Q_LORA)),
        const(w["w_qb"].shape),
        const((1, KV_LORA)),
        const(w["w_kbt"].shape),
        const(w["w_v"].shape),
        const((3, LANES)),
        const((LANES, 3)),
        pl.BlockSpec((tm, LANES), lambda bi, i: (i, 0)),
        pl.BlockSpec((tm, LANES), lambda bi, i: (i, 0)),
        pl.BlockSpec((LANES, tm), lambda bi, i: (0, i)),
        pl.BlockSpec((LANES, tm), lambda bi, i: (0, i)),
    ]
    head_spec = lambda nh, w_: pl.BlockSpec((1, nh, tm, w_), lambda bi, i: (bi, 0, i, 0))
    out_specs = [pl.BlockSpec((1, MLA_HEADS, 2 * LANES, tm), lambda bi, i: (bi, 0, 0, i)),
                 head_spec(MLA_HEADS, 2 * V_DIM)]
    out_shape = [jax.ShapeDtypeStruct((b, MLA_HEADS, 2 * LANES, n), BF16),
                 jax.ShapeDtypeStruct((b, MLA_HEADS, n, 2 * V_DIM), BF16)]
    if with_q:
        out_specs = [head_spec(MLA_HEADS, QK_NOPE), head_spec(MLA_HEADS // 2, LANES)] + out_specs
        out_shape = [jax.ShapeDtypeStruct((b, MLA_HEADS, n, QK_NOPE), BF16),
                     jax.ShapeDtypeStruct((b, MLA_HEADS // 2, n, LANES), BF16)] + out_shape
    return pl.pallas_call(
        functools.partial(_mla_pre_kernel, with_q=with_q),
        grid=(b, n // tm),
        in_specs=in_specs,
        out_specs=out_specs,
        out_shape=out_shape,
        compiler_params=_cparams(("parallel", "parallel")),
        name="mla_pre_q" if with_q else "mla_pre_ctx",
    )(x, mod, gmix, w["w_in"], w["w_pet"], w["q_norm"], w["w_qb"], w["kv_norm"], w["w_kbt"], w["w_v"],
      w["gq"], w["gkt"], cos, sin, cos.T, sin.T)


def _attn_kernel(qn_ref, qr_ref, kc_ref, vc_ref, kx_ref, vx_ref, o_ref,
                 m_s, acc_s, sc_s, pc_s, s0_s, s1_s, p0_s, p1_s, a0_s, a1_s, *, tk):
    q = jnp.concatenate([qn_ref[0, 0], qr_ref[0, 0]], axis=-1)
    nblk = 1 + vx_ref.shape[2] // tk
    s_buf = lambda i: sc_s if i == 0 else (s0_s, s1_s)[i % 2]
    p_buf = lambda i: pc_s if i == 0 else (p0_s, p1_s)[i % 2]
    a_buf = lambda i: (a0_s, a1_s)[i % 2]
    keys_t = lambda i: kc_ref[0, 0] if i == 0 else kx_ref[0, 0, :, (i - 1) * tk:i * tk]
    vals = lambda i: vc_ref[0, 0] if i == 0 else vx_ref[0, 0, (i - 1) * tk:i * tk, :]

    def scores(i):
        s_buf(i)[...] = jnp.dot(q, keys_t(i), preferred_element_type=F32)

    def softmax(i):
        for r in range(0, q.shape[0], SOFTMAX_ROWS):
            rows = slice(r, r + SOFTMAX_ROWS)
            s = s_buf(i)[rows, :]
            m_prev = m_s[rows, :]
            m_new = jnp.maximum(m_prev, jnp.max(s, axis=-1, keepdims=True))
            p_buf(i)[rows, :] = jnp.exp2(s - m_new).astype(BF16)
            a_buf(i)[rows, :] = jnp.exp2(m_prev - m_new)
            m_s[rows, :] = m_new

    def accumulate(i):
        acc_s[...] = a_buf(i)[...] * acc_s[...] + jnp.dot(p_buf(i)[...], vals(i), preferred_element_type=F32)

    m_s[...] = jnp.full(m_s.shape, -jnp.inf, F32)
    acc_s[...] = jnp.zeros(acc_s.shape, F32)
    scores(0)
    scores(1)
    softmax(0)
    for i in range(nblk):
        if i + 2 < nblk:
            scores(i + 2)
        if i + 1 < nblk:
            softmax(i + 1)
        accumulate(i)
    acc = acc_s[...]
    o_ref[0] = (acc[:, :V_DIM] / acc[:, V_DIM:]).astype(o_ref.dtype)


def _attention(qn, qr, kc, vc, kx, vx, *, tq, tk):
    b, h, n, _ = qn.shape
    nc = vc.shape[2]
    tq = min(tq, n)
    tk = min(tk, n)
    assert n // tk <= MAX_UNROLLED_KEY_BLOCKS
    kt_spec = lambda cols: pl.BlockSpec((1, 1, 2 * LANES, cols), lambda bi, hi, i: (bi, hi, 0, 0))
    v_spec = lambda rows: pl.BlockSpec((1, 1, rows, 2 * V_DIM), lambda bi, hi, i: (bi, hi, 0, 0))
    return pl.pallas_call(
        functools.partial(_attn_kernel, tk=tk),
        grid=(b, h, n // tq),
        in_specs=[
            pl.BlockSpec((1, 1, tq, QK_NOPE), lambda bi, hi, i: (bi, hi, i, 0)),
            pl.BlockSpec((1, 1, tq, LANES), lambda bi, hi, i: (bi, hi // 2, i, 0)),
            kt_spec(nc), v_spec(nc), kt_spec(n), v_spec(n),
        ],
        out_specs=pl.BlockSpec((1, tq, V_DIM), lambda bi, hi, i: (bi, i, hi)),
        out_shape=jax.ShapeDtypeStruct((b, n, h * V_DIM), BF16),
        scratch_shapes=[
            pltpu.VMEM((tq, 1), F32), pltpu.VMEM((tq, 2 * V_DIM), F32),
            pltpu.VMEM((tq, nc), F32), pltpu.VMEM((tq, nc), BF16),
            pltpu.VMEM((tq, tk), F32), pltpu.VMEM((tq, tk), F32),
            pltpu.VMEM((tq, tk), BF16), pltpu.VMEM((tq, tk), BF16),
            pltpu.VMEM((tq, 1), F32), pltpu.VMEM((tq, 1), F32),
        ],
        compiler_params=_cparams(("parallel", "parallel", "arbitrary")),
        name="flash_attn",
    )(qn, qr, kc, vc, kx, vx)


def _ffn_pre(x_new, mod, gffn_ref, rw_ref, h_ref, aff_ref):
    h2 = _modulate(x_new, gffn_ref[...], mod[3:4], mod[4:5])
    hi = h2.astype(BF16)
    lo = (h2 - hi.astype(F32)).astype(BF16)
    h_ref[0] = hi
    rw = rw_ref[...]
    t = jnp.dot(hi, rw, preferred_element_type=F32) + jnp.dot(lo, rw, preferred_element_type=F32)
    logits = t[:, :LANES] + t[:, LANES:]
    is_expert = lax.broadcasted_iota(jnp.int32, (1, LANES), 1) < N_EXPERTS
    logits = jnp.where(is_expert, logits, -jnp.inf)
    e = jnp.exp(logits - jnp.max(logits, axis=-1, keepdims=True))
    aff = e / jnp.sum(e, axis=-1, keepdims=True)
    aff_ref[0] = aff.T[:N_EXPERTS]


def _attn_out_kernel(o_ref, x_ref, mod_ref, wout_ref, gffn_ref, rw_ref, x1_ref, h_ref, aff_ref):
    mod = mod_ref[0]
    o = jnp.dot(o_ref[0], wout_ref[...], preferred_element_type=F32)
    x1 = x_ref[0] + mod[2:3] * o
    x1_ref[0] = x1
    _ffn_pre(x1, mod, gffn_ref, rw_ref, h_ref, aff_ref)


def _attn_out(ox, x, mod, w_out, gffn, rw, *, tm):
    b, n, d = x.shape
    tm = min(tm, n)
    const = lambda shape: pl.BlockSpec(shape, lambda bi, i: (0,) * len(shape))
    tok = lambda: pl.BlockSpec((1, tm, d), lambda bi, i: (bi, i, 0))
    return pl.pallas_call(
        _attn_out_kernel,
        grid=(b, n // tm),
        in_specs=[tok(), tok(), pl.BlockSpec((1, 6, d), lambda bi, i: (bi, 0, 0)),
                  const(w_out.shape), const((1, d)), const(rw.shape)],
        out_specs=[tok(), tok(), pl.BlockSpec((1, N_EXPERTS, tm), lambda bi, i: (bi, 0, i))],
        out_shape=[jax.ShapeDtypeStruct((b, n, d), F32),
                   jax.ShapeDtypeStruct((b, n, d), BF16),
                   jax.ShapeDtypeStruct((b, N_EXPERTS, n), F32)],
        compiler_params=_cparams(("parallel", "parallel")),
        name="attn_out",
    )(ox, x, mod, w_out, gffn, rw)


def _conv_kernel(xm_ref, xp_ref, xn_ref, mm_ref, mp_ref, mn_ref, gprev_ref, mod_ref, gmix_ref,
                 win_ref, cw_ref, wout_ref, gffn_ref, rw_ref, x3_ref, h_ref, aff_ref):
    i = pl.program_id(1)
    tm, d = xm_ref.shape[1], xm_ref.shape[2]
    halo = xp_ref.shape[1]
    gprev = gprev_ref[0]
    mod = mod_ref[0]
    xm = xm_ref[0] + gprev * _from_row_tiles(mm_ref)
    xe = jnp.concatenate([xp_ref[0] + gprev * _from_row_tiles(mp_ref), xm,
                          xn_ref[0] + gprev * _from_row_tiles(mn_ref)], axis=0)
    hx = _modulate(xe, gmix_ref[...], mod[0:1], mod[1:2]).astype(BF16)
    proj = jnp.dot(hx, win_ref[...], preferred_element_type=F32)
    u = proj[:, d:2 * d] * proj[:, 2 * d:3 * d]
    row = lax.broadcasted_iota(jnp.int32, (tm + 2 * halo, 1), 0)
    outside = jnp.logical_or(jnp.logical_and(i == 0, row < halo),
                             jnp.logical_and(i == pl.num_programs(1) - 1, row >= tm + halo))
    u = jnp.where(outside, 0.0, u)
    rows = tm + 2 * halo
    u_prev = pltpu.roll(u, 1, axis=0)[halo:halo + tm]
    u_next = pltpu.roll(u, rows - 1, axis=0)[halo:halo + tm]
    cw = cw_ref[...]
    y = cw[0:1] * u_prev + cw[1:2] * u[halo:halo + tm] + cw[2:3] * u_next
    z = (proj[halo:halo + tm, 0:d] * y).astype(BF16)
    x3 = xm + mod[2:3] * jnp.dot(z, wout_ref[...], preferred_element_type=F32)
    x3_ref[0] = x3
    _ffn_pre(x3, mod, gffn_ref, rw_ref, h_ref, aff_ref)


def _conv_mixer(x, moe, gprev, mod, gmix, w_in, cw, w_out, gffn, rw, *, tm):
    b, n, d = x.shape
    tm = min(tm, n)
    halo = 8
    nb = tm // halo
    last = n // halo - 1
    const = lambda shape: pl.BlockSpec(shape, lambda bi, i: (0,) * len(shape))
    tok = lambda: pl.BlockSpec((1, tm, d), lambda bi, i: (bi, i, 0))
    prev = lambda: pl.BlockSpec((1, halo, d), lambda bi, i: (bi, jnp.maximum(i * nb - 1, 0), 0))
    nxt = lambda: pl.BlockSpec((1, halo, d), lambda bi, i: (bi, jnp.minimum((i + 1) * nb, last), 0))
    tok_rt = pl.BlockSpec((1, SUBLANES, tm, LANES), lambda bi, i: (bi, 0, i, 0))
    prev_rt = pl.BlockSpec((1, SUBLANES, halo, LANES), lambda bi, i: (bi, 0, jnp.maximum(i * nb - 1, 0), 0))
    nxt_rt = pl.BlockSpec((1, SUBLANES, halo, LANES), lambda bi, i: (bi, 0, jnp.minimum((i + 1) * nb, last), 0))
    return pl.pallas_call(
        _conv_kernel,
        grid=(b, n // tm),
        in_specs=[tok(), prev(), nxt(), tok_rt, prev_rt, nxt_rt,
                  pl.BlockSpec((1, 1, d), lambda bi, i: (bi, 0, 0)),
                  pl.BlockSpec((1, 6, d), lambda bi, i: (bi, 0, 0)),
                  const((1, d)), const(w_in.shape), const(cw.shape), const(w_out.shape),
                  const((1, d)), const(rw.shape)],
        out_specs=[tok(), tok(), pl.BlockSpec((1, N_EXPERTS, tm), lambda bi, i: (bi, 0, i))],
        out_shape=[jax.ShapeDtypeStruct((b, n, d), F32),
                   jax.ShapeDtypeStruct((b, n, d), BF16),
                   jax.ShapeDtypeStruct((b, N_EXPERTS, n), F32)],
        compiler_params=_cparams(("parallel", "parallel")),
        name="conv_mixer",
    )(x, x, x, moe, moe, moe, gprev, mod, gmix, w_in, cw, w_out, gffn, rw)


def _moe_kernel(xs_ref, g_ref, wg_ref, wu_ref, wd_ref, y_ref, wg_s, wu_s, wd_s, *, tr, tf):
    c = xs_ref.shape[2]
    f_total = wg_s.shape[1]

    @pl.when(pl.program_id(1) == 0)
    def _():
        for src, dst in ((wg_ref, wg_s), (wu_ref, wu_s), (wd_ref, wd_s)):
            for r in range(0, dst.shape[0], tr):
                dst[r:r + tr, :] = src[0, 0, r:r + tr, :].astype(BF16)

    for r0 in range(0, c, tr):
        xs = xs_ref[0, 0, r0:r0 + tr, :]
        acc = jnp.zeros((tr, wd_s.shape[1]), F32)
        for f in range(f_total // tf):
            a = jnp.dot(xs, wg_s[:, f * tf:(f + 1) * tf], preferred_element_type=F32)
            u = jnp.dot(xs, wu_s[:, f * tf:(f + 1) * tf], preferred_element_type=F32)
            hm = (_silu(a) * u).astype(BF16)
            acc = acc + jnp.dot(hm, wd_s[f * tf:(f + 1) * tf, :], preferred_element_type=F32)
        y = acc * g_ref[0, 0, r0:r0 + tr, :]
        for k in range(SUBLANES):
            y_ref[0, 0, r0:r0 + tr, k, :] = y[:, k * LANES:(k + 1) * LANES]


def _moe_ffn(xs, g, wg, wu, wd, layer, e0):
    b, e, c, d = xs.shape
    f = wg.shape[3]
    tr = min(MOE_ROW_TILE, c)
    tf = min(512, f)
    w_spec = lambda rows_, cols: pl.BlockSpec((1, 1, rows_, cols), lambda ei, bi: (layer, e0 + ei, 0, 0))
    return pl.pallas_call(
        functools.partial(_moe_kernel, tr=tr, tf=tf),
        grid=(e, b),
        in_specs=[
            pl.BlockSpec((1, 1, c, d), lambda ei, bi: (bi, ei, 0, 0)),
            pl.BlockSpec((1, 1, c, 1), lambda ei, bi: (bi, ei, 0, 0)),
            w_spec(d, f), w_spec(d, f), w_spec(f, d),
        ],
        out_specs=pl.BlockSpec((1, 1, c, SUBLANES, LANES), lambda ei, bi: (bi, ei, 0, 0, 0)),
        out_shape=jax.ShapeDtypeStruct((b, e, c, SUBLANES, LANES), F32),
        scratch_shapes=[pltpu.VMEM((d, f), BF16), pltpu.VMEM((d, f), BF16), pltpu.VMEM((f, d), BF16)],
        compiler_params=_cparams(("arbitrary", "arbitrary")),
        name="moe_ffn",
    )(xs, g.reshape(b, e, c, 1), wg, wu, wd)


def _from_row_tiles(ref):
    return jnp.concatenate([ref[0, k] for k in range(SUBLANES)], axis=-1)


def _combine_kernel(idx_ref, *refs, unroll, zero_rows):
    y_refs, (out_hbm, acc_s, sem) = refs[:-3], refs[-3:]
    bi = pl.program_id(0)
    ei = pl.program_id(1)
    n = acc_s.shape[0]
    c = y_refs[0].shape[2]
    per_group = pl.num_programs(1) // len(y_refs)

    @pl.when(ei == 0)
    def _():
        def zero(i, carry):
            acc_s[pl.ds(pl.multiple_of(i * zero_rows, zero_rows), zero_rows)] = jnp.zeros(
                (zero_rows,) + acc_s.shape[1:], F32)
            return carry

        lax.fori_loop(0, n // zero_rows, zero, 0)

    def add_rows(y_ref):
        def rows(i, carry):
            base = i * unroll
            tok = [idx_ref[0, 0, base + u] for u in range(unroll)]
            new = [acc_s[tok[u]] + y_ref[0, 0, base + u] for u in range(unroll)]
            for u in range(unroll):
                acc_s[tok[u]] = new[u]
            return carry

        lax.fori_loop(0, c // unroll, rows, 0)

    for gi, y_ref in enumerate(y_refs):
        pl.when(ei // per_group == gi)(functools.partial(add_rows, y_ref))

    @pl.when(ei == pl.num_programs(1) - 1)
    def _():
        copies = [pltpu.make_async_copy(acc_s.at[:, k, :], out_hbm.at[bi, k], sem.at[k])
                  for k in range(SUBLANES)]
        for cp in copies:
            cp.start()
        for cp in copies:
            cp.wait()


def _moe_combine(idx, ys, n):
    b, e, c = idx.shape
    per_group = e // len(ys)
    unroll = 8
    zero_rows = min(256, n)

    def y_spec(gi):
        local = lambda ei: jnp.clip(ei - gi * per_group, 0, per_group - 1)
        return pl.BlockSpec((1, 1, c, SUBLANES, LANES), lambda bi, ei: (bi, local(ei), 0, 0, 0))

    return pl.pallas_call(
        functools.partial(_combine_kernel, unroll=unroll, zero_rows=zero_rows),
        grid=(b, e),
        in_specs=[pl.BlockSpec((1, 1, c), lambda bi, ei: (bi * e + ei, 0, 0), memory_space=pltpu.SMEM)]
        + [y_spec(gi) for gi in range(len(ys))],
        out_specs=pl.BlockSpec(memory_space=pl.ANY),
        out_shape=jax.ShapeDtypeStruct((b, SUBLANES, n, LANES), F32),
        scratch_shapes=[pltpu.VMEM((n, SUBLANES, LANES), F32), pltpu.SemaphoreType.DMA((SUBLANES,))],
        compiler_params=_cparams(("arbitrary", "arbitrary")),
        name="moe_combine",
    )(idx.reshape(b * e, 1, c), *ys)


def _route_kernel(aff_ref, uexcl_ref, uinclt_ref, ones_ref, lstrict_ref, idx_ref, gt_s, eq_s, need_s, *, cap):
    e_n, nch, _ = aff_ref.shape[1:]
    aff = aff_ref[0]

    def count(mask):
        return jnp.sum(jnp.sum(mask.astype(F32), axis=2, keepdims=True), axis=1, keepdims=True)

    def search(_, carry):
        lo, hi = carry
        mid = lo + ((hi - lo + 1) >> 1)
        ok = count(aff >= pltpu.bitcast(mid, F32)) >= cap
        return jnp.where(ok, mid, lo), jnp.where(ok, hi, mid - 1)

    lo0 = jnp.zeros((e_n, 1, 1), jnp.int32)
    hi0 = jnp.full((e_n, 1, 1), F32_INF_BITS, jnp.int32)
    tau_bits, _ = lax.fori_loop(0, F32_VALUE_BITS, search, (lo0, hi0))
    tau = pltpu.bitcast(tau_bits, F32)
    gt = aff > tau
    eq = aff == tau
    gt_s[...] = gt.astype(BF16)
    eq_s[...] = eq.astype(BF16)
    need_s[...] = jnp.broadcast_to(cap - count(gt), need_s.shape)

    n_slot = idx_ref.shape[3]
    slot = lax.broadcasted_iota(jnp.int32, (1, n_slot), 1).astype(F32)
    chunk_id = lax.broadcasted_iota(jnp.int32, (nch, n_slot), 0).astype(F32)
    widen = lambda a: jnp.concatenate([a] * (n_slot // LANES), axis=1)

    def chunk_prefix(mask):
        cnt = jnp.dot(mask, ones_ref[...], preferred_element_type=F32)
        return cnt, jnp.dot(lstrict_ref[...], cnt.astype(BF16), preferred_element_type=F32)

    def per_expert(e, carry):
        eq_e = eq_s[e]
        _, eq_start = chunk_prefix(eq_e)
        eq_rank = eq_start + jnp.dot(eq_e, uexcl_ref[...], preferred_element_type=F32)
        sel = gt_s[e] + jnp.where(eq_rank < need_s[e], eq_e, jnp.zeros_like(eq_e))
        cnt, start = chunk_prefix(sel)
        incl_t = lax.dot_general(uinclt_ref[...], sel, (((1,), (1,)), ((), ())),
                                 preferred_element_type=F32)
        start_w, cnt_w = widen(start), widen(cnt)
        hit = jnp.logical_and(start_w <= slot, slot < start_w + cnt_w)
        hit_f = hit.astype(F32)
        slot_start = jnp.sum(hit_f * start_w, axis=0, keepdims=True)
        slot_chunk = jnp.sum(hit_f * chunk_id, axis=0, keepdims=True)
        incl_of_slot = jnp.dot(incl_t.astype(BF16), hit.astype(BF16), preferred_element_type=F32)
        local = jnp.sum((incl_of_slot <= slot - slot_start).astype(F32), axis=0, keepdims=True)
        idx_ref[0, e] = (slot_chunk * LANES + local).astype(jnp.int32)
        return carry

    lax.fori_loop(0, e_n, per_expert, 0)


def _route(aff, cap):
    b, e, n = aff.shape
    assert n % LANES == 0 and cap % LANES == 0
    nch = n // LANES
    i = jnp.arange(LANES)
    t = jnp.arange(nch)
    uexcl = (i[:, None] < i[None, :]).astype(BF16)
    uinclt = (i[None, :] <= i[:, None]).astype(BF16)
    lstrict = (t[None, :] < t[:, None]).astype(BF16)
    const = lambda shape: pl.BlockSpec(shape, lambda bi: (0,) * len(shape))
    mask_scratch = lambda dt: pltpu.VMEM((e, nch, LANES), dt)
    idx = pl.pallas_call(
        functools.partial(_route_kernel, cap=cap),
        grid=(b,),
        in_specs=[pl.BlockSpec((1, e, nch, LANES), lambda bi: (bi, 0, 0, 0)),
                  const((LANES, LANES)), const((LANES, LANES)), const((LANES, LANES)), const((nch, nch))],
        out_specs=pl.BlockSpec((1, e, 1, cap), lambda bi: (bi, 0, 0, 0)),
        out_shape=jax.ShapeDtypeStruct((b, e, 1, cap), jnp.int32),
        scratch_shapes=[mask_scratch(BF16), mask_scratch(BF16), mask_scratch(F32)],
        compiler_params=_cparams(("parallel",)),
        name="moe_route",
    )(aff.reshape(b, e, nch, LANES), uexcl, uinclt, jnp.ones((LANES, LANES), BF16), lstrict)
    return idx.reshape(b, e, cap)


def _ec_moe(h, aff, wg, wu, wd, layer):
    b, n, d = h.shape
    assert d == SUBLANES * LANES
    cap = EC_CAPACITY * n // N_EXPERTS
    idx = _route(aff, cap)
    g = jnp.take_along_axis(aff, idx, axis=-1)
    per_group = N_EXPERTS // MOE_EXPERT_GROUPS
    ys = []
    for e0 in range(0, N_EXPERTS, per_group):
        idx_g = idx[:, e0:e0 + per_group]
        xs = jax.vmap(lambda hb, ib: hb[ib])(h, idx_g)
        ys.append(_moe_ffn(xs, g[:, e0:e0 + per_group], wg, wu, wd, layer, e0))
    return _moe_combine(idx, ys, n)


def _residual_kernel(x_ref, m_ref, g_ref, o_ref):
    o_ref[0] = x_ref[0] + g_ref[0] * _from_row_tiles(m_ref)


def _residual(x, moe, gate, *, tm):
    b, n, d = x.shape
    tm = min(tm, n)
    tok = lambda: pl.BlockSpec((1, tm, d), lambda bi, i: (bi, i, 0))
    return pl.pallas_call(
        _residual_kernel,
        grid=(b, n // tm),
        in_specs=[tok(), pl.BlockSpec((1, SUBLANES, tm, LANES), lambda bi, i: (bi, 0, i, 0)),
                  pl.BlockSpec((1, 1, d), lambda bi, i: (bi, 0, 0))],
        out_specs=tok(),
        out_shape=jax.ShapeDtypeStruct((b, n, d), F32),
        compiler_params=_cparams(("parallel", "parallel")),
        name="moe_residual",
    )(x, moe, gate)


def _rope_tables(n):
    rows = n // GRID_W
    row = jnp.broadcast_to(jnp.arange(rows, dtype=F32)[:, None], (rows, GRID_W)).reshape(-1)
    col = jnp.broadcast_to(jnp.arange(GRID_W, dtype=F32)[None, :], (rows, GRID_W)).reshape(-1)
    inv = ROPE_THETA ** (-jnp.arange(ROPE_FREQS, dtype=F32) / ROPE_FREQS)
    ar, ac = row[:, None] * inv, col[:, None] * inv
    cos = jnp.concatenate([jnp.cos(ar), jnp.cos(ar), jnp.cos(ac), jnp.cos(ac)], axis=-1)
    sin = jnp.concatenate([-jnp.sin(ar), jnp.sin(ar), -jnp.sin(ac), jnp.sin(ac)], axis=-1)
    return jnp.tile(cos, (1, 2)), jnp.tile(sin, (1, 2))


def _swap_perm():
    f = ROPE_FREQS
    base = jnp.arange(QK_ROPE)
    return jnp.where((base // f) % 2 == 0, base + f, base - f)


def _mla_weights(w_in, q_norm, w_qb, kv_norm, w_kvb, q_gain, k_gain):
    perm = _swap_perm()
    pe = w_in[:, Q_LORA + KV_LORA:]
    pe_sw = pe[:, perm]
    w_pet = jnp.concatenate([pe, pe, pe_sw, pe_sw], axis=1).T
    wq = w_qb.reshape(Q_LORA, MLA_HEADS, QK_DIM)
    wq_rope = wq[:, :, QK_NOPE:]
    w_qb_x = jnp.concatenate([
        wq[:, :, :QK_NOPE].reshape(Q_LORA, -1),
        wq_rope.reshape(Q_LORA, -1),
        wq_rope[:, :, perm].reshape(Q_LORA, -1)], axis=1)
    wkv = w_kvb.reshape(KV_LORA, MLA_HEADS, QK_NOPE + V_DIM)
    w_kbt = wkv[:, :, :QK_NOPE].reshape(KV_LORA, -1).T
    w_v = wkv[:, :, QK_NOPE:].reshape(KV_LORA, -1)

    def gains(g):
        gr = g[QK_NOPE:]
        return jnp.stack([g[:QK_NOPE], jnp.tile(gr, 2), jnp.tile(gr[perm], 2)])

    return {
        "w_in": w_in[:, :Q_LORA + KV_LORA].astype(BF16), "w_pet": w_pet.astype(BF16),
        "q_norm": q_norm[None, :], "w_qb": w_qb_x.astype(BF16),
        "kv_norm": kv_norm[None, :], "w_kbt": w_kbt.astype(BF16), "w_v": w_v.astype(BF16),
        "gq": gains(q_gain), "gkt": gains(k_gain).T,
    }


def _pad_router(rw):
    hi = rw.astype(BF16)
    lo = (rw - hi.astype(F32)).astype(BF16)
    pad = ((0, 0), (0, LANES - rw.shape[1]))
    return jnp.concatenate([jnp.pad(hi, pad), jnp.pad(lo, pad)], axis=1)


def kernel(x, c, ctx, c_ctx, norm_mix, norm_ffn, ada_w, ada_b, mla_w_in, mla_q_norm, mla_w_qb, mla_kv_norm, mla_w_kvb, mla_q_gain, mla_k_gain, mla_w_out, conv_w_in, conv_w, conv_w_out, router_w, exp_w_gate, exp_w_up, exp_w_down):
    b, n, d = x.shape
    nc = ctx.shape[1]
    depth = ada_w.shape[0]
    assert depth == 2 and b < 8

    cond = jnp.concatenate([c, c_ctx[None, :], jnp.zeros((8 - b - 1, d), F32)], axis=0)
    mod_all = _adaln(cond, ada_w, ada_b).reshape(depth, 8, 6, d)
    mod0, mod1 = mod_all[0, :b], mod_all[1, :b]
    mod0_ctx = mod_all[0, b:b + 1]

    w = _mla_weights(mla_w_in[0], mla_q_norm[0], mla_w_qb[0], mla_kv_norm[0], mla_w_kvb[0],
                     mla_q_gain[0], mla_k_gain[0])
    cos, sin = _rope_tables(n)
    gmix0 = norm_mix[0][None, :]
    qn, qr, kx, vx = _mla_pre(x, mod0, False, gmix0, w, cos, sin, with_q=True, tm=PROJ_TOKEN_TILE)
    kc, vc = _mla_pre(ctx, mod0_ctx, True, gmix0, w, jnp.ones((nc, LANES), F32),
                      jnp.zeros((nc, LANES), F32), with_q=False, tm=PROJ_TOKEN_TILE)
    ox = _attention(qn, qr, kc, vc, kx, vx, tq=ATTN_Q_TILE, tk=ATTN_KEY_TILE)
    x1, h0, aff0 = _attn_out(ox, x, mod0, mla_w_out[0].astype(BF16), norm_ffn[0][None, :],
                             _pad_router(router_w[0]), tm=PROJ_TOKEN_TILE)
    moe0 = _ec_moe(h0, aff0, exp_w_gate, exp_w_up, exp_w_down, 0)

    x3, h1, aff1 = _conv_mixer(x1, moe0, mod0[:, 5:6], mod1, norm_mix[1][None, :],
                               conv_w_in[0].astype(BF16), conv_w[0], conv_w_out[0].astype(BF16),
                               norm_ffn[1][None, :], _pad_router(router_w[1]), tm=TOKEN_TILE)
    moe1 = _ec_moe(h1, aff1, exp_w_gate, exp_w_up, exp_w_down, 1)
    return _residual(x3, moe1, mod1[:, 5:6], tm=TOKEN_TILE)
```
